```python
import math
import jax
import jax.numpy as jnp
from jax import lax
import numpy as np

D_MODEL = 2048
BATCH = 4
SEQ = 8192
DEPTH = 1

PLE_DIM = 256
HG_HEADS = 8
HG_HEAD_K = 128
HG_HEAD_V = 128
HG_WIDTH = HG_HEADS * HG_HEAD_K
SC_GROUPS = 8
SC_GROUP_DIM = 128
SC_WIDTH = SC_GROUPS * SC_GROUP_DIM
CONV_WIDTH = 3
MIX_WIDTH = HG_WIDTH + SC_WIDTH
IN_COLS = 4 * HG_WIDTH + 3 * SC_WIDTH
CHUNK = 64
N_GROUPS = 4
EXPERTS_PER_GROUP = 8
N_EXPERTS = N_GROUPS * EXPERTS_PER_GROUP
TOP_K = 2
EXPERT_FF = 512
BLK = 256
EPS = 1e-6

kernel_name = "hymba_hgrn2_shortconv_hmoe_ple"


def rms_norm(x, gain):
    xf = x.astype(jnp.float32)
    xf = xf * lax.rsqrt(jnp.mean(xf * xf, axis=-1, keepdims=True) + EPS)
    return (xf * gain.astype(jnp.float32)).astype(x.dtype)


def _to_chunks(t, d):
    b, T, _ = t.shape
    return t.reshape(b, T // CHUNK, CHUNK, HG_HEADS, d).transpose(1, 0, 3, 2, 4)


def hgrn2_group(q_raw, f_raw, i_raw, g_raw, lb, norm_gain):
    dt = q_raw.dtype
    b, T, _ = q_raw.shape
    f32 = jnp.float32
    q = jax.nn.silu(q_raw.astype(f32)) * (HG_HEAD_K ** -0.5)
    fz = f_raw.astype(f32)
    lb = lb.astype(f32)
    log_f = jnp.logaddexp(jnp.log(lb), jnp.log1p(-lb) + jax.nn.log_sigmoid(fz))
    k = (1.0 - lb) * jax.nn.sigmoid(-fz)
    v = i_raw.astype(f32)

    qc, kc, vc, gc = (_to_chunks(q, HG_HEAD_K), _to_chunks(k, HG_HEAD_K),
                      _to_chunks(v, HG_HEAD_V), _to_chunks(log_f, HG_HEAD_K))
    causal = jnp.tril(jnp.ones((CHUNK, CHUNK), dtype=bool))

    def step(S, inp):
        qb, kb, vb, gb = inp
        A = jnp.cumsum(gb, axis=2)
        o_inter = jnp.einsum('bhtk,bhkv->bhtv', qb * jnp.exp(A), S)
        diff = A[:, :, :, None, :] - A[:, :, None, :, :]
        decay = jnp.exp(jnp.where(causal[:, :, None], diff, -jnp.inf))
        scores = jnp.einsum('bhtk,bhsk,bhtsk->bhts', qb, kb, decay)
        o_intra = jnp.einsum('bhts,bhsv->bhtv', scores, vb)
        A_last = A[:, :, -1:, :]
        S_new = (jnp.exp(A_last[:, :, 0, :])[..., None] * S
                 + jnp.einsum('bhsk,bhsv->bhkv', kb * jnp.exp(A_last - A), vb))
        return S_new, o_inter + o_intra

    S0 = jnp.zeros((b, HG_HEADS, HG_HEAD_K, HG_HEAD_V), f32)
    _, o = lax.scan(step, S0, (qc, kc, vc, gc))
    o = o.transpose(1, 0, 3, 2, 4).reshape(b, T, HG_HEADS, HG_HEAD_V)
    o = o * lax.rsqrt(jnp.mean(o * o, axis=-1, keepdims=True) + EPS) * norm_gain.astype(f32)
    g = jax.nn.silu(g_raw.astype(f32)).reshape(b, T, HG_HEADS, HG_HEAD_V)
    return (o * g).reshape(b, T, HG_HEADS * HG_HEAD_V).astype(dt)


def short_conv_group(b_gate, c_gate, h, conv_w, out_gain):
    u = c_gate * h
    w = conv_w.astype(u.dtype)[:, None, :]
    y = lax.conv_general_dilated(u, w, window_strides=(1,), padding=((CONV_WIDTH - 1, 0),),
                                 dimension_numbers=('NWC', 'WIO', 'NWC'),
                                 feature_group_count=SC_WIDTH)
    return rms_norm(b_gate * y, out_gain)


def hier_moe(h, w_rg, w_re, w_gate, w_up, w_down):
    N, D = h.shape
    f32 = jnp.float32
    hf = h.astype(f32)
    g_prob = jax.nn.softmax(hf @ w_rg.astype(f32), axis=-1)
    g_p, g_idx = lax.top_k(g_prob, 1)
    e_logits = (hf @ w_re.astype(f32)).reshape(N, N_GROUPS, EXPERTS_PER_GROUP)
    e_logits = jnp.take_along_axis(e_logits, g_idx[:, :, None], axis=1)[:, 0]
    e_prob = jax.nn.softmax(e_logits, axis=-1)
    e_p, e_idx = lax.top_k(e_prob, TOP_K)
    weights = g_p * e_p / jnp.sum(e_p, axis=-1, keepdims=True)
    expert_id = g_idx * EXPERTS_PER_GROUP + e_idx

    A = N * TOP_K
    flat_e = expert_id.reshape(A)
    flat_tok = jnp.repeat(jnp.arange(N, dtype=jnp.int32), TOP_K)
    flat_w = weights.reshape(A)
    order = jnp.argsort(flat_e)
    sorted_e = flat_e[order]
    counts = jnp.bincount(flat_e, length=N_EXPERTS)
    padded = (counts + BLK - 1) // BLK * BLK
    start = jnp.cumsum(counts) - counts
    pstart = jnp.cumsum(padded) - padded
    dest = pstart[sorted_e] + jnp.arange(A, dtype=jnp.int32) - start[sorted_e]
    n_blocks = -(-(A + N_EXPERTS * (BLK - 1)) // BLK)
    P = n_blocks * BLK
    buf_tok = jnp.full((P,), N, jnp.int32).at[dest].set(flat_tok[order])
    buf_w = jnp.zeros((P,), f32).at[dest].set(flat_w[order])
    block_e = jnp.minimum(
        jnp.searchsorted(jnp.cumsum(padded), jnp.arange(n_blocks) * BLK, side='right'),
        N_EXPERTS - 1).astype(jnp.int32)
    h_pad = jnp.concatenate([h, jnp.zeros((1, D), h.dtype)], axis=0)
    xs = h_pad[buf_tok].reshape(n_blocks, BLK, D)

    def expert_block(args):
        xb, e = args
        a = xb @ w_gate[e]
        bb = xb @ w_up[e]
        return (jax.nn.silu(a) * bb) @ w_down[e]

    ys = lax.map(expert_block, (xs, block_e)).reshape(P, D)
    out = jnp.zeros((N + 1, D), h.dtype).at[buf_tok].add(ys * buf_w[:, None].astype(h.dtype))
    return out[:N]


def setup_inputs(seed: int = 0) -> dict:
    key = jax.random.key(seed)
    ks = jax.random.split(key, 20)
    f32 = jnp.float32
    nrm = lambda k, shape, fan_in: jax.random.normal(k, shape, f32) * (fan_in ** -0.5)
    gain = lambda k, shape: 1.0 + 0.02 * jax.random.normal(k, shape, f32)
    return {
        "x": jax.random.normal(ks[0], (BATCH, SEQ, D_MODEL), f32),
        "p": jax.random.normal(ks[1], (DEPTH, BATCH, SEQ, PLE_DIM), f32),
        "g_mix": gain(ks[2], (DEPTH, D_MODEL)),
        "w_in": nrm(ks[3], (DEPTH, D_MODEL, IN_COLS), D_MODEL),
        "lb_logits": 0.5 * jax.random.normal(ks[4], (DEPTH + 1, HG_WIDTH), f32),
        "hg_norm": gain(ks[5], (DEPTH, HG_HEAD_V)),
        "conv_w": nrm(ks[6], (DEPTH, CONV_WIDTH, SC_WIDTH), CONV_WIDTH),
        "sc_norm": gain(ks[7], (DEPTH, SC_WIDTH)),
        "w_out": nrm(ks[8], (DEPTH, MIX_WIDTH, D_MODEL), MIX_WIDTH),
        "g_ffn": gain(ks[9], (DEPTH, D_MODEL)),
        "w_router_group": nrm(ks[10], (DEPTH, D_MODEL, N_GROUPS), D_MODEL),
        "w_router_expert": nrm(ks[11], (DEPTH, D_MODEL, N_EXPERTS), D_MODEL),
        "w_gate": nrm(ks[12], (DEPTH, N_EXPERTS, D_MODEL, EXPERT_FF), D_MODEL),
        "w_up": nrm(ks[13], (DEPTH, N_EXPERTS, D_MODEL, EXPERT_FF), D_MODEL),
        "w_down": nrm(ks[14], (DEPTH, N_EXPERTS, EXPERT_FF, D_MODEL), EXPERT_FF),
        "w_ple": nrm(ks[15], (DEPTH, PLE_DIM, D_MODEL), PLE_DIM),
        "g_ple": gain(ks[16], (DEPTH, D_MODEL)),
        "w_ple_gate": nrm(ks[17], (DEPTH, D_MODEL, D_MODEL), D_MODEL),
        "g_final": gain(ks[18], (D_MODEL,)),
    }


def reference(x, p, g_mix, w_in, lb_logits, hg_norm, conv_w, sc_norm, w_out, g_ffn,
              w_router_group, w_router_expert, w_gate, w_up, w_down, w_ple, g_ple,
              w_ple_gate, g_final):
    b, T, D = x.shape
    lower_bounds = jnp.cumsum(jax.nn.softmax(lb_logits.astype(jnp.float32), axis=0), axis=0)
    for l in range(DEPTH):
        xn = rms_norm(x, g_mix[l])
        z = xn @ w_in[l]
        q_raw, f_raw, i_raw, g_raw, b_gate, c_gate, h_sc = jnp.split(
            z, np.cumsum([HG_WIDTH] * 4 + [SC_WIDTH] * 2).tolist(), axis=-1)
        o_hg = hgrn2_group(q_raw, f_raw, i_raw, g_raw, lower_bounds[l], hg_norm[l])
        o_sc = short_conv_group(b_gate, c_gate, h_sc, conv_w[l], sc_norm[l])
        x = x + jnp.concatenate([o_hg, o_sc], axis=-1) @ w_out[l]
        hn = rms_norm(x, g_ffn[l]).reshape(b * T, D)
        x = x + hier_moe(hn, w_router_group[l], w_router_expert[l],
                         w_gate[l], w_up[l], w_down[l]).reshape(b, T, D)
        ple = rms_norm(p[l] @ w_ple[l], g_ple[l])
        x = x + jax.nn.sigmoid(x @ w_ple_gate[l]) * ple
    return rms_norm(x, g_final)
```

```python
import functools

import jax
import jax.numpy as jnp
from jax import lax
from jax.experimental import pallas as pl
from jax.experimental.pallas import tpu as pltpu

F32 = jnp.float32
BF16 = jnp.bfloat16
EPS = 1e-6

HEAD_DIM = 128
N_HEADS = 8
HG_WIDTH = N_HEADS * HEAD_DIM
SC_WIDTH = 1024
CHUNK = 64
N_GROUPS = 4
EXPERTS_PER_GROUP = 8
N_EXPERTS = N_GROUPS * EXPERTS_PER_GROUP
LANES = 128
EXPERT_LANE0 = N_GROUPS
NEG_BIG = -1e30
VMEM_LIMIT = 56 * 1024 * 1024


def _rms(v, gain):
    return v * lax.rsqrt(jnp.mean(v * v, axis=-1, keepdims=True) + EPS) * gain


def _params(sem):
    return pltpu.CompilerParams(dimension_semantics=sem, vmem_limit_bytes=VMEM_LIMIT)


def _inproj_kernel(x_ref, g_ref, w_ref, z_ref, xn_ref):
    @pl.when(pl.program_id(1) == 0)
    def _():
        xn_ref[...] = _rms(x_ref[...], g_ref[...]).astype(BF16)

    z_ref[...] = jnp.dot(xn_ref[...], w_ref[...], preferred_element_type=F32).astype(z_ref.dtype)


def _inproj(x2d, g_mix, w_in_bf16, tm, tn):
    n, d = x2d.shape
    cols = w_in_bf16.shape[1]
    return pl.pallas_call(
        _inproj_kernel,
        out_shape=jax.ShapeDtypeStruct((n, cols), BF16),
        grid=(n // tm, cols // tn),
        in_specs=[
            pl.BlockSpec((tm, d), lambda i, j: (i, 0)),
            pl.BlockSpec((1, d), lambda i, j: (0, 0)),
            pl.BlockSpec((d, tn), lambda i, j: (0, j)),
        ],
        out_specs=pl.BlockSpec((tm, tn), lambda i, j: (i, j)),
        scratch_shapes=[pltpu.VMEM((tm, d), BF16)],
        compiler_params=_params(("arbitrary", "arbitrary")),
        name="inproj",
    )(x2d, g_mix, w_in_bf16)


def _hgrn_kernel(q_ref, f_ref, i_ref, g_ref, lb_ref, gain_ref, o_ref, st_ref, *, nchunks):
    @pl.when(pl.program_id(2) == 0)
    def _():
        st_ref[...] = jnp.zeros_like(st_ref)

    c_len = CHUNK
    lb = lb_ref[...]
    one_minus_lb = 1.0 - lb
    gain = gain_ref[...]
    row = lax.broadcasted_iota(jnp.int32, (c_len, c_len), 0)
    col = lax.broadcasted_iota(jnp.int32, (c_len, c_len), 1)
    causal = row >= col
    tri = causal.astype(BF16)
    tri3 = jnp.concatenate([tri, tri, tri], axis=1)
    scale = HEAD_DIM ** -0.5
    nt = (((1,), (1,)), ((), ()))

    def body(c, st):
        sl = pl.ds(pl.multiple_of(c * c_len, c_len), c_len)
        fz = f_ref[sl, :].astype(F32)
        sg = jax.nn.sigmoid(fz)
        log_f = jnp.log(lb + one_minus_lb * sg)
        k = one_minus_lb * (1.0 - sg)
        h1 = log_f.astype(BF16)
        r1 = log_f - h1.astype(F32)
        h2 = r1.astype(BF16)
        h3 = (r1 - h2.astype(F32)).astype(BF16)
        a = jnp.dot(tri3, jnp.concatenate([h1, h2, h3], axis=0), preferred_element_type=F32)
        a_mid = a[c_len // 2 - 1:c_len // 2, :]
        a_last = a[c_len - 1:c_len, :]
        qz = q_ref[sl, :].astype(F32)
        q = qz * jax.nn.sigmoid(qz) * scale
        qs = (q * jnp.exp(a - a_mid)).astype(BF16)
        ks = (k * jnp.exp(a_mid - a)).astype(BF16)
        qa = (q * jnp.exp(a)).astype(BF16)
        kd = (k * jnp.exp(a_last - a)).astype(BF16)
        v = i_ref[sl, :]
        scores = lax.dot_general(qs, ks, nt, preferred_element_type=F32)
        p = jnp.where(causal, scores, 0.0).astype(BF16)
        o = lax.dot_general(qa, st.astype(BF16), nt, preferred_element_type=F32)
        o = o + jnp.dot(p, v, preferred_element_type=F32)
        v_t = v.astype(F32).T.astype(BF16)
        st_new = st * jnp.exp(a_last) + jnp.dot(v_t, kd, preferred_element_type=F32)
        gz = g_ref[sl, :].astype(F32)
        o_ref[sl, :] = (_rms(o, gain) * (gz * jax.nn.sigmoid(gz))).astype(o_ref.dtype)
        return st_new

    st_ref[...] = lax.fori_loop(0, nchunks, body, st_ref[...])


def _hgrn(z3, lb, gain, tc):
    b, t, _ = z3.shape
    hb = HG_WIDTH // HEAD_DIM

    def zspec(k):
        return pl.BlockSpec((None, tc, HEAD_DIM), lambda bi, h, ti, k=k: (bi, ti, h + k * hb))

    return pl.pallas_call(
        functools.partial(_hgrn_kernel, nchunks=tc // CHUNK),
        out_shape=jax.ShapeDtypeStruct((b, t, HG_WIDTH), BF16),
        grid=(b, N_HEADS, t // tc),
        in_specs=[
            zspec(0), zspec(1), zspec(2), zspec(3),
            pl.BlockSpec((1, HEAD_DIM), lambda bi, h, ti: (0, h)),
            pl.BlockSpec((1, HEAD_DIM), lambda bi, h, ti: (0, 0)),
        ],
        out_specs=pl.BlockSpec((None, tc, HEAD_DIM), lambda bi, h, ti: (bi, ti, h)),
        scratch_shapes=[pltpu.VMEM((HEAD_DIM, HEAD_DIM), F32)],
        compiler_params=_params(("arbitrary", "arbitrary", "arbitrary")),
        name="hgrn",
    )(z3, z3, z3, z3, lb, gain)


SC_HALO = 8


def _sconv_kernel(b_ref, c_ref, h_ref, w_ref, gain_ref, o_ref, u_ref, *, tc):
    @pl.when(pl.program_id(1) == 0)
    def _():
        u_ref[0:SC_HALO, :] = jnp.zeros((SC_HALO, SC_WIDTH), F32)

    u = c_ref[...].astype(F32) * h_ref[...].astype(F32)
    u_ref[SC_HALO:SC_HALO + tc, :] = u
    w = w_ref[...]
    y = (w[0:1, :] * u_ref[SC_HALO - 2:SC_HALO - 2 + tc, :]
         + w[1:2, :] * u_ref[SC_HALO - 1:SC_HALO - 1 + tc, :]
         + w[2:3, :] * u)
    o_ref[...] = _rms(b_ref[...].astype(F32) * y, gain_ref[...]).astype(o_ref.dtype)
    u_ref[0:SC_HALO, :] = u_ref[tc:tc + SC_HALO, :]


def _sconv(z3, conv_w, gain, tc):
    b, t, _ = z3.shape
    first = 4 * HG_WIDTH // SC_WIDTH

    def zspec(k):
        return pl.BlockSpec((None, tc, SC_WIDTH), lambda bi, ti, k=k: (bi, ti, first + k))

    return pl.pallas_call(
        functools.partial(_sconv_kernel, tc=tc),
        out_shape=jax.ShapeDtypeStruct((b, t, SC_WIDTH), BF16),
        grid=(b, t // tc),
        in_specs=[
            zspec(0), zspec(1), zspec(2),
            pl.BlockSpec((3, SC_WIDTH), lambda bi, ti: (0, 0)),
            pl.BlockSpec((1, SC_WIDTH), lambda bi, ti: (0, 0)),
        ],
        out_specs=pl.BlockSpec((None, tc, SC_WIDTH), lambda bi, ti: (bi, ti, 0)),
        scratch_shapes=[pltpu.VMEM((tc + SC_HALO, SC_WIDTH), F32)],
        compiler_params=_params(("arbitrary", "arbitrary")),
        name="sconv",
    )(z3, z3, z3, conv_w, gain)


R_E1, R_E2, R_W1, R_W2, R_RANK1, R_RANK2 = 0, 1, 2, 3, 4, 5


def _outproj_kernel(x_ref, ohg_ref, osc_ref, wo1_ref, wo2_ref, g_ref, wr_hl_ref, wr_h_ref,
                    x1_ref, hn_ref, route_ref, cnt_ref, carry_ref, *, tm):
    @pl.when(pl.program_id(0) == 0)
    def _():
        carry_ref[...] = jnp.zeros_like(carry_ref)

    acc = jnp.dot(ohg_ref[...], wo1_ref[...], preferred_element_type=F32)
    acc = acc + jnp.dot(osc_ref[...], wo2_ref[...], preferred_element_type=F32)
    x1 = x_ref[...] + acc
    x1_ref[...] = x1
    hn = _rms(x1, g_ref[...])
    hn_ref[...] = hn

    hi = hn.astype(BF16)
    lo = (hn - hi.astype(F32)).astype(BF16)
    both = jnp.dot(hi, wr_hl_ref[...], preferred_element_type=F32)
    logits = both[:, :LANES] + both[:, LANES:] + jnp.dot(lo, wr_h_ref[...], preferred_element_type=F32)

    lane = lax.broadcasted_iota(jnp.int32, (tm, LANES), 1)

    def first_argmax(vals, vmax):
        return jnp.min(jnp.where(vals == vmax, lane, LANES), axis=-1, keepdims=True)

    is_group = lane < N_GROUPS
    gl = jnp.where(is_group, logits, NEG_BIG)
    gmax = jnp.max(gl, axis=-1, keepdims=True)
    gidx = first_argmax(gl, gmax)
    g_p = 1.0 / jnp.sum(jnp.where(is_group, jnp.exp(gl - gmax), 0.0), axis=-1, keepdims=True)

    e_lane = lane - EXPERT_LANE0
    in_group = (e_lane >= 0) & (e_lane < N_EXPERTS) & ((e_lane // EXPERTS_PER_GROUP) == gidx)
    el = jnp.where(in_group, logits, NEG_BIG)
    m1 = jnp.max(el, axis=-1, keepdims=True)
    i1 = first_argmax(el, m1)
    el2 = jnp.where(lane == i1, NEG_BIG, el)
    m2 = jnp.max(el2, axis=-1, keepdims=True)
    i2 = first_argmax(el2, m2)
    r = jnp.exp(m2 - m1)
    w1 = g_p / (1.0 + r)
    w2 = g_p * r / (1.0 + r)

    oh1 = lane == i1
    oh2 = lane == i2
    onehot = (oh1 | oh2).astype(BF16)
    trow = lax.broadcasted_iota(jnp.int32, (tm, tm), 0)
    tcol = lax.broadcasted_iota(jnp.int32, (tm, tm), 1)
    before = jnp.dot((trow > tcol).astype(BF16), onehot, preferred_element_type=F32) + carry_ref[...]
    rank1 = jnp.sum(jnp.where(oh1, before, 0.0), axis=-1, keepdims=True)
    rank2 = jnp.sum(jnp.where(oh2, before, 0.0), axis=-1, keepdims=True)
    carry_ref[...] = carry_ref[...] + jnp.sum(onehot.astype(F32), axis=0, keepdims=True)
    cnt_ref[...] = carry_ref[...]

    table = jnp.zeros((tm, LANES), F32)
    for slot, val in ((R_E1, (i1 - EXPERT_LANE0).astype(F32)), (R_E2, (i2 - EXPERT_LANE0).astype(F32)),
                      (R_W1, w1), (R_W2, w2), (R_RANK1, rank1), (R_RANK2, rank2)):
        table = jnp.where(lane == slot, val, table)
    route_ref[...] = table


def _outproj(x2d, ohg, osc, wo1, wo2, g_ffn, wr_hl, wr_h, tm):
    n, d = x2d.shape
    const = lambda i: (0, 0)
    return pl.pallas_call(
        functools.partial(_outproj_kernel, tm=tm),
        out_shape=(
            jax.ShapeDtypeStruct((n, d), F32),
            jax.ShapeDtypeStruct((n, d), F32),
            jax.ShapeDtypeStruct((n, LANES), F32),
            jax.ShapeDtypeStruct((1, LANES), F32),
        ),
        grid=(n // tm,),
        in_specs=[
            pl.BlockSpec((tm, d), lambda i: (i, 0)),
            pl.BlockSpec((tm, HG_WIDTH), lambda i: (i, 0)),
            pl.BlockSpec((tm, SC_WIDTH), lambda i: (i, 0)),
            pl.BlockSpec((HG_WIDTH, d), const),
            pl.BlockSpec((SC_WIDTH, d), const),
            pl.BlockSpec((1, d), const),
            pl.BlockSpec((d, 2 * LANES), const),
            pl.BlockSpec((d, LANES), const),
        ],
        out_specs=(
            pl.BlockSpec((tm, d), lambda i: (i, 0)),
            pl.BlockSpec((tm, d), lambda i: (i, 0)),
            pl.BlockSpec((tm, LANES), lambda i: (i, 0)),
            pl.BlockSpec((1, LANES), const),
        ),
        scratch_shapes=[pltpu.VMEM((1, LANES), F32)],
        compiler_params=_params(("arbitrary",)),
        name="outproj",
    )(x2d, ohg, osc, wo1, wo2, g_ffn, wr_hl, wr_h)


def _dispatch_kernel(cnt_ref, pad_ref, start_ref, dest_hbm, hn_ref, xs_hbm, idx_ref, zero_ref, idx_sem, sem,
                     *, td, nsteps):
    i = pl.program_id(0)
    idx_copy = pltpu.make_async_copy(dest_hbm.at[pl.ds(i * 2 * td, 2 * td)], idx_ref, idx_sem)
    idx_copy.start()
    idx_copy.wait()

    def row_copy(t, slot):
        return pltpu.make_async_copy(hn_ref.at[pl.ds(t, 1)], xs_hbm.at[pl.ds(slot, 1)], sem)

    def issue(t, carry):
        row_copy(t, idx_ref[2 * t]).start()
        row_copy(t, idx_ref[2 * t + 1]).start()
        return carry

    lax.fori_loop(0, td, issue, 0)

    def drain(t, carry):
        row_copy(t, 0).wait()
        row_copy(t, 0).wait()
        return carry

    lax.fori_loop(0, td, drain, 0)

    @pl.when(i == nsteps - 1)
    def _():
        zero_ref[...] = jnp.zeros_like(zero_ref)

        def pad_copy(slot):
            return pltpu.make_async_copy(zero_ref.at[pl.ds(0, 1)], xs_hbm.at[pl.ds(slot, 1)], sem)

        def per_expert(e, carry):
            base = start_ref[e]

            def issue_pad(s, c2):
                pad_copy(base + s).start()
                return c2

            def drain_pad(s, c2):
                pad_copy(0).wait()
                return c2

            lax.fori_loop(cnt_ref[e], pad_ref[e], issue_pad, 0)
            lax.fori_loop(cnt_ref[e], pad_ref[e], drain_pad, 0)
            return carry

        lax.fori_loop(0, N_EXPERTS + 1, per_expert, 0)


def _dispatch(counts, padded, pstart, dest_flat, hn, p_rows, td):
    n, d = hn.shape
    nsteps = n // td
    grid_spec = pltpu.PrefetchScalarGridSpec(
        num_scalar_prefetch=3,
        grid=(nsteps,),
        in_specs=[
            pl.BlockSpec(memory_space=pl.ANY),
            pl.BlockSpec((td, d), lambda i, *_: (i, 0)),
        ],
        out_specs=pl.BlockSpec(memory_space=pl.ANY),
        scratch_shapes=[
            pltpu.SMEM((2 * td,), jnp.int32),
            pltpu.VMEM((8, d), F32),
            pltpu.SemaphoreType.DMA,
            pltpu.SemaphoreType.DMA,
        ],
    )
    return pl.pallas_call(
        functools.partial(_dispatch_kernel, td=td, nsteps=nsteps),
        out_shape=jax.ShapeDtypeStruct((p_rows, d), F32),
        grid_spec=grid_spec,
        compiler_params=_params(("arbitrary",)),
        name="dispatch",
    )(counts, padded, pstart, dest_flat, hn)


def _experts_kernel(blk_e_ref, nused_ref, xs_ref, wg_ref, wu_ref, wd_ref, y_ref):
    @pl.when(pl.program_id(0) >= nused_ref[0])
    def _():
        y_ref[...] = jnp.zeros_like(y_ref)

    @pl.when(pl.program_id(0) < nused_ref[0])
    def _():
        x = xs_ref[...].astype(BF16)
        a = jnp.dot(x, wg_ref[...], preferred_element_type=F32)
        b = jnp.dot(x, wu_ref[...], preferred_element_type=F32)
        h = (a * jax.nn.sigmoid(a) * b).astype(BF16)
        y_ref[...] = jnp.dot(h, wd_ref[...], preferred_element_type=F32)


def _experts(block_e, nused, xs, wg, wu, wd, tb):
    p_rows, d = xs.shape
    ff = wg.shape[2]
    nblocks = p_rows // tb

    def row_map(i, blk_e, nu):
        return (jnp.minimum(i, nu[0] - 1), 0)

    def w_map(i, blk_e, nu):
        return (blk_e[i], 0, 0)

    grid_spec = pltpu.PrefetchScalarGridSpec(
        num_scalar_prefetch=2,
        grid=(nblocks,),
        in_specs=[
            pl.BlockSpec((tb, d), row_map),
            pl.BlockSpec((None, d, ff), w_map),
            pl.BlockSpec((None, d, ff), w_map),
            pl.BlockSpec((None, ff, d), w_map),
        ],
        out_specs=pl.BlockSpec((tb, d), lambda i, blk_e, nu: (i, 0)),
    )
    return pl.pallas_call(
        _experts_kernel,
        out_shape=jax.ShapeDtypeStruct((p_rows, d), F32),
        grid_spec=grid_spec,
        compiler_params=_params(("arbitrary",)),
        name="experts",
    )(block_e, nused, xs, wg, wu, wd)


def _final_kernel(dest_hbm, route_ref, x1_ref, p_ref, wple_ref, gple_ref, wgate_ref, gfin_ref, y_hbm,
                  o_ref, idx_ref, ybuf_ref, idx_sem, sem, *, tf):
    i = pl.program_id(0)
    idx_copy = pltpu.make_async_copy(dest_hbm.at[pl.ds(i * 2 * tf, 2 * tf)], idx_ref, idx_sem)
    idx_copy.start()
    idx_copy.wait()

    def row_copy(t, k, slot):
        return pltpu.make_async_copy(y_hbm.at[pl.ds(slot, 1)], ybuf_ref.at[k, pl.ds(t, 1)], sem)

    def issue(t, carry):
        row_copy(t, 0, idx_ref[2 * t]).start()
        row_copy(t, 1, idx_ref[2 * t + 1]).start()
        return carry

    lax.fori_loop(0, tf, issue, 0)

    ple = _rms(jnp.dot(p_ref[...].astype(BF16), wple_ref[...], preferred_element_type=F32), gple_ref[...])

    def drain(t, carry):
        row_copy(t, 0, 0).wait()
        row_copy(t, 1, 0).wait()
        return carry

    lax.fori_loop(0, tf, drain, 0)

    route = route_ref[...]
    w1 = route[:, R_W1:R_W1 + 1]
    w2 = route[:, R_W2:R_W2 + 1]
    x2 = x1_ref[...] + w1 * ybuf_ref[0] + w2 * ybuf_ref[1]
    gate = jax.nn.sigmoid(jnp.dot(x2.astype(BF16), wgate_ref[...], preferred_element_type=F32))
    o_ref[...] = _rms(x2 + gate * ple, gfin_ref[...])


def _final(dest_flat, route, x1, p2d, wple, gple, wgate, gfin, y, tf):
    n, d = x1.shape
    pd = p2d.shape[1]
    const = lambda i: (0, 0)
    return pl.pallas_call(
        functools.partial(_final_kernel, tf=tf),
        out_shape=jax.ShapeDtypeStruct((n, d), F32),
        grid=(n // tf,),
        in_specs=[
            pl.BlockSpec(memory_space=pl.ANY),
            pl.BlockSpec((tf, LANES), lambda i: (i, 0)),
            pl.BlockSpec((tf, d), lambda i: (i, 0)),
            pl.BlockSpec((tf, pd), lambda i: (i, 0)),
            pl.BlockSpec((pd, d), const),
            pl.BlockSpec((1, d), const),
            pl.BlockSpec((d, d), const),
            pl.BlockSpec((1, d), const),
            pl.BlockSpec(memory_space=pl.ANY),
        ],
        out_specs=pl.BlockSpec((tf, d), lambda i: (i, 0)),
        scratch_shapes=[
            pltpu.SMEM((2 * tf,), jnp.int32),
            pltpu.VMEM((2, tf, d), F32),
            pltpu.SemaphoreType.DMA,
            pltpu.SemaphoreType.DMA,
        ],
        compiler_params=_params(("arbitrary",)),
        name="final",
    )(dest_flat, route, x1, p2d, wple, gple, wgate, gfin, y)


def _tile(n, want):
    t = min(n, want)
    assert n % t == 0, (n, want)
    return t


def kernel(x, p, g_mix, w_in, lb_logits, hg_norm, conv_w, sc_norm, w_out, g_ffn, w_router_group,
           w_router_expert, w_gate, w_up, w_down, w_ple, g_ple, w_ple_gate, g_final):
    b, t, d = x.shape
    n = b * t
    layer = 0
    x2d = x.reshape(n, d)

    lower_bounds = jnp.cumsum(jax.nn.softmax(lb_logits.astype(F32), axis=0), axis=0)
    lb = lower_bounds[layer].reshape(1, HG_WIDTH)

    z = _inproj(x2d, g_mix[layer].reshape(1, d), w_in[layer].astype(BF16), _tile(n, 1024), 1024)
    z3 = z.reshape(b, t, z.shape[1])
    ohg = _hgrn(z3, lb, hg_norm[layer].reshape(1, HEAD_DIM), _tile(t, 1024)).reshape(n, HG_WIDTH)
    osc = _sconv(z3, conv_w[layer], sc_norm[layer].reshape(1, SC_WIDTH), _tile(t, 512)).reshape(n, SC_WIDTH)

    wo = w_out[layer].astype(BF16)
    wr = jnp.concatenate([w_router_group[layer], w_router_expert[layer]], axis=1).astype(F32)
    wr = jnp.pad(wr, ((0, 0), (0, LANES - wr.shape[1])))
    wr_hi = wr.astype(BF16)
    wr_lo = (wr - wr_hi.astype(F32)).astype(BF16)
    x1, hn, route, cnt = _outproj(
        x2d, ohg, osc, wo[:HG_WIDTH], wo[HG_WIDTH:], g_ffn[layer].reshape(1, d),
        jnp.concatenate([wr_hi, wr_lo], axis=1), wr_hi, _tile(n, 512))

    tb = 256
    counts = cnt[0, EXPERT_LANE0:EXPERT_LANE0 + N_EXPERTS].astype(jnp.int32)
    padded = (counts + tb - 1) // tb * tb
    pend = jnp.cumsum(padded)
    pstart = pend - padded
    e_ids = route[:, R_E1:R_E2 + 1].astype(jnp.int32)
    ranks = route[:, R_RANK1:R_RANK2 + 1].astype(jnp.int32)
    dest_flat = (pstart[e_ids] + ranks).reshape(2 * n)
    nblocks = -(-(2 * n + N_EXPERTS * (tb - 1)) // tb)
    nused = (pend[-1] // tb).astype(jnp.int32)
    blk = jnp.minimum(jnp.arange(nblocks, dtype=jnp.int32), nused - 1) * tb
    block_e = jnp.minimum(jnp.searchsorted(pend, blk, side="right"), N_EXPERTS - 1).astype(jnp.int32)

    p_rows = nblocks * tb
    zero = jnp.zeros((1,), jnp.int32)
    xs = _dispatch(jnp.concatenate([counts, zero]), jnp.concatenate([padded, p_rows - pend[-1:]]),
                   jnp.concatenate([pstart, pend[-1:]]).astype(jnp.int32), dest_flat, hn, p_rows, _tile(n, 256))
    y = _experts(block_e, nused.reshape(1), xs, w_gate[layer].astype(BF16), w_up[layer].astype(BF16),
                 w_down[layer].astype(BF16), tb)

    out = _final(dest_flat, route, x1, p[layer].reshape(n, -1), w_ple[layer].astype(BF16),
                 g_ple[layer].reshape(1, d), w_ple_gate[layer].astype(BF16), g_final.reshape(1, d), y,
                 _tile(n, 256))
    return out.reshape(b, t, d)
```

```python
import functools

import jax
import jax.numpy as jnp
from jax import lax
from jax.experimental import pallas as pl
from jax.experimental.pallas import tpu as pltpu

F32 = jnp.float32
BF16 = jnp.bfloat16
EPS = 1e-6

HEAD_DIM = 128
N_HEADS = 8
HG_WIDTH = N_HEADS * HEAD_DIM
SC_WIDTH = 1024
CHUNK = 128
N_GROUPS = 4
EXPERTS_PER_GROUP = 8
N_EXPERTS = N_GROUPS * EXPERTS_PER_GROUP
LANES = 128
EXPERT_LANE0 = N_GROUPS
NEG_BIG = -1e30
VMEM_LIMIT = 56 * 1024 * 1024


def _rms(v, gain):
    return v * lax.rsqrt(jnp.mean(v * v, axis=-1, keepdims=True) + EPS) * gain


def _token_rows(d):
    return d // (2 * LANES)


def _pack_tokens(v, out_ref):
    rows, d = v.shape
    tr = _token_rows(d)
    bits = lax.bitcast_convert_type(v.astype(BF16).astype(F32), jnp.uint32)
    words = bits[:, :d // 2] | (bits[:, d // 2:] >> 16)
    for s in range(tr):
        out_ref[pl.ds(s, rows, stride=tr), :] = words[:, s * LANES:(s + 1) * LANES]


def _unpack_tokens(load_rows, tr):
    words = [load_rows(s) for s in range(tr)]
    high = [lax.bitcast_convert_type(w & jnp.uint32(0xFFFF0000), F32) for w in words]
    low = [lax.bitcast_convert_type(w << 16, F32) for w in words]
    return jnp.concatenate(high + low, axis=1)


def _params(sem):
    return pltpu.CompilerParams(dimension_semantics=sem, vmem_limit_bytes=VMEM_LIMIT)


def _inproj_kernel(x_ref, g_ref, w_ref, z_ref, xn_ref):
    @pl.when(pl.program_id(1) == 0)
    def _():
        xn_ref[...] = _rms(x_ref[...], g_ref[...]).astype(BF16)

    z_ref[...] = jnp.dot(xn_ref[...], w_ref[...], preferred_element_type=F32).astype(z_ref.dtype)


def _inproj(x2d, g_mix, w_in_bf16, tm, tn):
    n, d = x2d.shape
    cols = w_in_bf16.shape[1]
    return pl.pallas_call(
        _inproj_kernel,
        out_shape=jax.ShapeDtypeStruct((n, cols), BF16),
        grid=(n // tm, cols // tn),
        in_specs=[
            pl.BlockSpec((tm, d), lambda i, j: (i, 0)),
            pl.BlockSpec((1, d), lambda i, j: (0, 0)),
            pl.BlockSpec((d, tn), lambda i, j: (0, j)),
        ],
        out_specs=pl.BlockSpec((tm, tn), lambda i, j: (i, j)),
        scratch_shapes=[pltpu.VMEM((tm, d), BF16)],
        compiler_params=_params(("arbitrary", "arbitrary")),
        name="inproj",
    )(x2d, g_mix, w_in_bf16)


def _hgrn_kernel(q_ref, f_ref, i_ref, g_ref, lb_ref, gain_ref, o_ref, st_ref, *, nchunks):
    @pl.when(pl.program_id(2) == 0)
    def _():
        st_ref[...] = jnp.zeros_like(st_ref)

    c_len = CHUNK
    n_levels = c_len.bit_length() - 1
    lb = lb_ref[...]
    one_minus_lb = 1.0 - lb
    gain = gain_ref[...]
    row = lax.broadcasted_iota(jnp.int32, (c_len, c_len), 0)
    col = lax.broadcasted_iota(jnp.int32, (c_len, c_len), 1)
    tri = (row >= col).astype(BF16)
    tri2 = jnp.concatenate([tri, tri], axis=1)
    differ = jnp.where(col < row, row ^ col, 0)
    sub8 = lax.broadcasted_iota(jnp.int32, (c_len // 8, 8, HEAD_DIM), 1)
    scale = HEAD_DIM ** -0.5
    nt = (((1,), (1,)), ((), ()))
    sign_bit = jnp.uint32(0x80000000)

    def neg_abs(d):
        return lax.bitcast_convert_type(lax.bitcast_convert_type(d, jnp.uint32) | sign_bit, F32)

    def boundary(a, half):
        if half >= 4:
            nb = c_len // (2 * half)
            a3 = a.reshape(nb, 2 * half, HEAD_DIM)
            return jnp.broadcast_to(a3[:, half - 1:half, :], a3.shape).reshape(c_len, HEAD_DIM)
        assert half == 2
        a3 = a.reshape(c_len // 8, 8, HEAD_DIM)
        return jnp.where(sub8 < 4, a3[:, 1:2, :], a3[:, 5:6, :]).reshape(c_len, HEAD_DIM)

    def body(c, st):
        sl = pl.ds(pl.multiple_of(c * c_len, c_len), c_len)
        fz = f_ref[sl, :].astype(F32)
        sg = jax.nn.sigmoid(fz)
        f = lb + one_minus_lb * sg
        log2_f = jnp.log2(f)
        k = (one_minus_lb * (1.0 - sg)).astype(BF16)
        h1 = log2_f.astype(BF16)
        h2 = (log2_f - h1.astype(F32)).astype(BF16)
        a = jnp.dot(tri2, jnp.concatenate([h1, h2], axis=0), preferred_element_type=F32)
        a_last = a[c_len - 1:c_len, :]
        qz = q_ref[sl, :]
        q = qz * jax.nn.sigmoid(qz) * scale
        qa = q * jnp.exp2(a).astype(BF16)
        kd = k * jnp.exp2(a_last - a).astype(BF16)
        v = i_ref[sl, :]
        diag = jnp.sum(q.astype(F32) * k.astype(F32), axis=-1, keepdims=True)
        scores = jnp.where(row == col, diag, 0.0)
        for level in range(n_levels):
            if level == 0:
                ql, kl = q * f.astype(BF16), k
            else:
                decay = jnp.exp2(neg_abs(a - boundary(a, 1 << level))).astype(BF16)
                ql, kl = q * decay, k * decay
            gram = lax.dot_general(ql, kl, nt, preferred_element_type=F32)
            scores = jnp.where(differ >= (1 << level), gram, scores)
        p = scores.astype(BF16)
        o = lax.dot_general(qa, st.astype(BF16), nt, preferred_element_type=F32)
        o = o + jnp.dot(p, v, preferred_element_type=F32)
        v_t = v.astype(F32).T.astype(BF16)
        st_new = st * jnp.exp2(a_last) + jnp.dot(v_t, kd, preferred_element_type=F32)
        gz = g_ref[sl, :]
        gate = (gz * jax.nn.sigmoid(gz)).astype(F32)
        o_ref[sl, :] = (_rms(o, gain) * gate).astype(o_ref.dtype)
        return st_new

    st_ref[...] = lax.fori_loop(0, nchunks, body, st_ref[...], unroll=True)


def _hgrn(z3, lb, gain, tc):
    b, t, _ = z3.shape
    hb = HG_WIDTH // HEAD_DIM

    def zspec(k):
        return pl.BlockSpec((None, tc, HEAD_DIM), lambda bi, h, ti, k=k: (bi, ti, h + k * hb))

    return pl.pallas_call(
        functools.partial(_hgrn_kernel, nchunks=tc // CHUNK),
        out_shape=jax.ShapeDtypeStruct((b, t, HG_WIDTH), BF16),
        grid=(b, N_HEADS, t // tc),
        in_specs=[
            zspec(0), zspec(1), zspec(2), zspec(3),
            pl.BlockSpec((1, HEAD_DIM), lambda bi, h, ti: (0, h)),
            pl.BlockSpec((1, HEAD_DIM), lambda bi, h, ti: (0, 0)),
        ],
        out_specs=pl.BlockSpec((None, tc, HEAD_DIM), lambda bi, h, ti: (bi, ti, h)),
        scratch_shapes=[pltpu.VMEM((HEAD_DIM, HEAD_DIM), F32)],
        compiler_params=_params(("arbitrary", "arbitrary", "arbitrary")),
        name="hgrn",
    )(z3, z3, z3, z3, lb, gain)


SC_HALO = 8


def _sconv_kernel(b_ref, c_ref, h_ref, w_ref, gain_ref, o_ref, u_ref, *, tc):
    @pl.when(pl.program_id(1) == 0)
    def _():
        u_ref[0:SC_HALO, :] = jnp.zeros((SC_HALO, SC_WIDTH), F32)

    u = c_ref[...].astype(F32) * h_ref[...].astype(F32)
    u_ref[SC_HALO:SC_HALO + tc, :] = u
    w = w_ref[...]
    y = (w[0:1, :] * u_ref[SC_HALO - 2:SC_HALO - 2 + tc, :]
         + w[1:2, :] * u_ref[SC_HALO - 1:SC_HALO - 1 + tc, :]
         + w[2:3, :] * u)
    o_ref[...] = _rms(b_ref[...].astype(F32) * y, gain_ref[...]).astype(o_ref.dtype)
    u_ref[0:SC_HALO, :] = u_ref[tc:tc + SC_HALO, :]


def _sconv(z3, conv_w, gain, tc):
    b, t, _ = z3.shape
    first = 4 * HG_WIDTH // SC_WIDTH

    def zspec(k):
        return pl.BlockSpec((None, tc, SC_WIDTH), lambda bi, ti, k=k: (bi, ti, first + k))

    return pl.pallas_call(
        functools.partial(_sconv_kernel, tc=tc),
        out_shape=jax.ShapeDtypeStruct((b, t, SC_WIDTH), BF16),
        grid=(b, t // tc),
        in_specs=[
            zspec(0), zspec(1), zspec(2),
            pl.BlockSpec((3, SC_WIDTH), lambda bi, ti: (0, 0)),
            pl.BlockSpec((1, SC_WIDTH), lambda bi, ti: (0, 0)),
        ],
        out_specs=pl.BlockSpec((None, tc, SC_WIDTH), lambda bi, ti: (bi, ti, 0)),
        scratch_shapes=[pltpu.VMEM((tc + SC_HALO, SC_WIDTH), F32)],
        compiler_params=_params(("arbitrary", "arbitrary")),
        name="sconv",
    )(z3, z3, z3, conv_w, gain)


R_E1, R_E2, R_W1, R_W2, R_RANK1, R_RANK2 = 0, 1, 2, 3, 4, 5


def _outproj_kernel(x_ref, ohg_ref, osc_ref, wo1_ref, wo2_ref, g_ref, wr_hl_ref, wr_h_ref,
                    x1_ref, hn_ref, route_ref, cnt_ref, carry_ref, *, tm):
    @pl.when(pl.program_id(0) == 0)
    def _():
        carry_ref[...] = jnp.zeros_like(carry_ref)

    acc = jnp.dot(ohg_ref[...], wo1_ref[...], preferred_element_type=F32)
    acc = acc + jnp.dot(osc_ref[...], wo2_ref[...], preferred_element_type=F32)
    x1 = x_ref[...] + acc
    x1_ref[...] = x1
    hn = _rms(x1, g_ref[...])
    _pack_tokens(hn, hn_ref)

    hi = hn.astype(BF16)
    lo = (hn - hi.astype(F32)).astype(BF16)
    both = jnp.dot(hi, wr_hl_ref[...], preferred_element_type=F32)
    logits = both[:, :LANES] + both[:, LANES:] + jnp.dot(lo, wr_h_ref[...], preferred_element_type=F32)

    lane = lax.broadcasted_iota(jnp.int32, (tm, LANES), 1)

    def first_argmax(vals, vmax):
        return jnp.min(jnp.where(vals == vmax, lane, LANES), axis=-1, keepdims=True)

    is_group = lane < N_GROUPS
    gl = jnp.where(is_group, logits, NEG_BIG)
    gmax = jnp.max(gl, axis=-1, keepdims=True)
    gidx = first_argmax(gl, gmax)
    g_p = 1.0 / jnp.sum(jnp.where(is_group, jnp.exp(gl - gmax), 0.0), axis=-1, keepdims=True)

    e_lane = lane - EXPERT_LANE0
    in_group = (e_lane >= 0) & (e_lane < N_EXPERTS) & ((e_lane // EXPERTS_PER_GROUP) == gidx)
    el = jnp.where(in_group, logits, NEG_BIG)
    m1 = jnp.max(el, axis=-1, keepdims=True)
    i1 = first_argmax(el, m1)
    el2 = jnp.where(lane == i1, NEG_BIG, el)
    m2 = jnp.max(el2, axis=-1, keepdims=True)
    i2 = first_argmax(el2, m2)
    r = jnp.exp(m2 - m1)
    w1 = g_p / (1.0 + r)
    w2 = g_p * r / (1.0 + r)

    oh1 = lane == i1
    oh2 = lane == i2
    onehot = (oh1 | oh2).astype(BF16)
    trow = lax.broadcasted_iota(jnp.int32, (tm, tm), 0)
    tcol = lax.broadcasted_iota(jnp.int32, (tm, tm), 1)
    before = jnp.dot((trow > tcol).astype(BF16), onehot, preferred_element_type=F32) + carry_ref[...]
    rank1 = jnp.sum(jnp.where(oh1, before, 0.0), axis=-1, keepdims=True)
    rank2 = jnp.sum(jnp.where(oh2, before, 0.0), axis=-1, keepdims=True)
    carry_ref[...] = carry_ref[...] + jnp.sum(onehot.astype(F32), axis=0, keepdims=True)
    cnt_ref[...] = carry_ref[...]

    table = jnp.zeros((tm, LANES), F32)
    for slot, val in ((R_E1, (i1 - EXPERT_LANE0).astype(F32)), (R_E2, (i2 - EXPERT_LANE0).astype(F32)),
                      (R_W1, w1), (R_W2, w2), (R_RANK1, rank1), (R_RANK2, rank2)):
        table = jnp.where(lane == slot, val, table)
    route_ref[...] = table


def _outproj(x2d, ohg, osc, wo1, wo2, g_ffn, wr_hl, wr_h, tm):
    n, d = x2d.shape
    const = lambda i: (0, 0)
    return pl.pallas_call(
        functools.partial(_outproj_kernel, tm=tm),
        out_shape=(
            jax.ShapeDtypeStruct((n, d), F32),
            jax.ShapeDtypeStruct((n * _token_rows(d), LANES), jnp.uint32),
            jax.ShapeDtypeStruct((n, LANES), F32),
            jax.ShapeDtypeStruct((1, LANES), F32),
        ),
        grid=(n // tm,),
        in_specs=[
            pl.BlockSpec((tm, d), lambda i: (i, 0)),
            pl.BlockSpec((tm, HG_WIDTH), lambda i: (i, 0)),
            pl.BlockSpec((tm, SC_WIDTH), lambda i: (i, 0)),
            pl.BlockSpec((HG_WIDTH, d), const),
            pl.BlockSpec((SC_WIDTH, d), const),
            pl.BlockSpec((1, d), const),
            pl.BlockSpec((d, 2 * LANES), const),
            pl.BlockSpec((d, LANES), const),
        ],
        out_specs=(
            pl.BlockSpec((tm, d), lambda i: (i, 0)),
            pl.BlockSpec((tm * _token_rows(d), LANES), lambda i: (i, 0)),
            pl.BlockSpec((tm, LANES), lambda i: (i, 0)),
            pl.BlockSpec((1, LANES), const),
        ),
        scratch_shapes=[pltpu.VMEM((1, LANES), F32)],
        compiler_params=_params(("arbitrary",)),
        name="outproj",
    )(x2d, ohg, osc, wo1, wo2, g_ffn, wr_hl, wr_h)


def _dispatch_kernel(cnt_ref, pad_ref, start_ref, dest_hbm, hn_ref, xs_hbm, idx_ref, zero_ref, idx_sem, sem,
                     *, td, tr, nsteps):
    i = pl.program_id(0)
    idx_copy = pltpu.make_async_copy(dest_hbm.at[pl.ds(i * 2 * td, 2 * td)], idx_ref, idx_sem)
    idx_copy.start()
    idx_copy.wait()

    def token_copy(t, slot):
        src = hn_ref.at[pl.ds(pl.multiple_of(t * tr, tr), tr)]
        return pltpu.make_async_copy(src, xs_hbm.at[pl.ds(pl.multiple_of(slot * tr, tr), tr)], sem)

    def issue(t, carry):
        token_copy(t, idx_ref[2 * t]).start()
        token_copy(t, idx_ref[2 * t + 1]).start()
        return carry

    lax.fori_loop(0, td, issue, 0, unroll=8)
    tile_copy = pltpu.make_async_copy(hn_ref, xs_hbm.at[pl.ds(0, td * tr)], sem)
    tile_copy.wait()
    tile_copy.wait()

    @pl.when(i == nsteps - 1)
    def _():
        zero_ref[...] = jnp.zeros_like(zero_ref)

        def pad_copy(slot):
            return pltpu.make_async_copy(zero_ref, xs_hbm.at[pl.ds(pl.multiple_of(slot * tr, tr), tr)], sem)

        def per_expert(e, carry):
            base = start_ref[e]

            def issue_pad(s, c2):
                pad_copy(base + s).start()
                return c2

            def drain_pad(s, c2):
                pad_copy(0).wait()
                return c2

            lax.fori_loop(cnt_ref[e], pad_ref[e], issue_pad, 0)
            lax.fori_loop(cnt_ref[e], pad_ref[e], drain_pad, 0)
            return carry

        lax.fori_loop(0, N_EXPERTS + 1, per_expert, 0)


def _dispatch(counts, padded, pstart, dest_flat, hn_packed, n, tr, p_rows, td):
    nsteps = n // td
    grid_spec = pltpu.PrefetchScalarGridSpec(
        num_scalar_prefetch=3,
        grid=(nsteps,),
        in_specs=[
            pl.BlockSpec(memory_space=pl.ANY),
            pl.BlockSpec((td * tr, LANES), lambda i, *_: (i, 0)),
        ],
        out_specs=pl.BlockSpec(memory_space=pl.ANY),
        scratch_shapes=[
            pltpu.SMEM((2 * td,), jnp.int32),
            pltpu.VMEM((tr, LANES), jnp.uint32),
            pltpu.SemaphoreType.DMA,
            pltpu.SemaphoreType.DMA,
        ],
    )
    return pl.pallas_call(
        functools.partial(_dispatch_kernel, td=td, tr=tr, nsteps=nsteps),
        out_shape=jax.ShapeDtypeStruct((p_rows * tr, LANES), jnp.uint32),
        grid_spec=grid_spec,
        compiler_params=_params(("arbitrary",)),
        name="dispatch",
    )(counts, padded, pstart, dest_flat, hn_packed)


def _experts_kernel(blk_e_ref, nused_ref, xs_ref, wg_ref, wu_ref, wd_ref, y_ref, *, tb, tr):
    @pl.when(pl.program_id(0) >= nused_ref[0])
    def _():
        y_ref[...] = jnp.zeros_like(y_ref)

    @pl.when(pl.program_id(0) < nused_ref[0])
    def _():
        x = _unpack_tokens(lambda s: xs_ref[pl.ds(s, tb, stride=tr), :], tr).astype(BF16)
        a = jnp.dot(x, wg_ref[...], preferred_element_type=F32)
        b = jnp.dot(x, wu_ref[...], preferred_element_type=F32)
        h = (a * jax.nn.sigmoid(a) * b).astype(BF16)
        _pack_tokens(jnp.dot(h, wd_ref[...], preferred_element_type=F32), y_ref)


def _experts(block_e, nused, xs_packed, wg, wu, wd, tb):
    d, ff = wg.shape[1], wg.shape[2]
    tr = _token_rows(d)
    nblocks = xs_packed.shape[0] // (tb * tr)

    def row_map(i, blk_e, nu):
        return (jnp.minimum(i, nu[0] - 1), 0)

    def w_map(i, blk_e, nu):
        return (blk_e[i], 0, 0)

    grid_spec = pltpu.PrefetchScalarGridSpec(
        num_scalar_prefetch=2,
        grid=(nblocks,),
        in_specs=[
            pl.BlockSpec((tb * tr, LANES), row_map),
            pl.BlockSpec((None, d, ff), w_map),
            pl.BlockSpec((None, d, ff), w_map),
            pl.BlockSpec((None, ff, d), w_map),
        ],
        out_specs=pl.BlockSpec((tb * tr, LANES), lambda i, blk_e, nu: (i, 0)),
    )
    return pl.pallas_call(
        functools.partial(_experts_kernel, tb=tb, tr=tr),
        out_shape=jax.ShapeDtypeStruct(xs_packed.shape, jnp.uint32),
        grid_spec=grid_spec,
        compiler_params=_params(("arbitrary",)),
        name="experts",
    )(block_e, nused, xs_packed, wg, wu, wd)


def _final_kernel(dest_hbm, route_ref, x1_ref, p_ref, wple_ref, gple_ref, wgate_ref, gfin_ref, y_hbm,
                  o_ref, idx_ref, ybuf_ref, idx_sem, sem, *, tf, tr):
    i = pl.program_id(0)
    idx_copy = pltpu.make_async_copy(dest_hbm.at[pl.ds(i * 2 * tf, 2 * tf)], idx_ref, idx_sem)
    idx_copy.start()
    idx_copy.wait()

    def token_copy(t, k, slot):
        src = y_hbm.at[pl.ds(pl.multiple_of(slot * tr, tr), tr)]
        return pltpu.make_async_copy(src, ybuf_ref.at[k, pl.ds(pl.multiple_of(t * tr, tr), tr)], sem)

    def issue(t, carry):
        token_copy(t, 0, idx_ref[2 * t]).start()
        token_copy(t, 1, idx_ref[2 * t + 1]).start()
        return carry

    lax.fori_loop(0, tf, issue, 0, unroll=8)

    ple = _rms(jnp.dot(p_ref[...].astype(BF16), wple_ref[...], preferred_element_type=F32), gple_ref[...])

    for k in range(2):
        pltpu.make_async_copy(y_hbm.at[pl.ds(0, tf * tr)], ybuf_ref.at[k], sem).wait()

    route = route_ref[...]
    w1 = route[:, R_W1:R_W1 + 1]
    w2 = route[:, R_W2:R_W2 + 1]
    y1 = _unpack_tokens(lambda s: ybuf_ref[0, pl.ds(s, tf, stride=tr), :], tr)
    y2 = _unpack_tokens(lambda s: ybuf_ref[1, pl.ds(s, tf, stride=tr), :], tr)
    x2 = x1_ref[...] + w1 * y1 + w2 * y2
    gate = jax.nn.sigmoid(jnp.dot(x2.astype(BF16), wgate_ref[...], preferred_element_type=F32))
    o_ref[...] = _rms(x2 + gate * ple, gfin_ref[...])


def _final(dest_flat, route, x1, p2d, wple, gple, wgate, gfin, y, tf):
    n, d = x1.shape
    pd = p2d.shape[1]
    tr = _token_rows(d)
    const = lambda i: (0, 0)
    return pl.pallas_call(
        functools.partial(_final_kernel, tf=tf, tr=tr),
        out_shape=jax.ShapeDtypeStruct((n, d), F32),
        grid=(n // tf,),
        in_specs=[
            pl.BlockSpec(memory_space=pl.ANY),
            pl.BlockSpec((tf, LANES), lambda i: (i, 0)),
            pl.BlockSpec((tf, d), lambda i: (i, 0)),
            pl.BlockSpec((tf, pd), lambda i: (i, 0)),
            pl.BlockSpec((pd, d), const),
            pl.BlockSpec((1, d), const),
            pl.BlockSpec((d, d), const),
            pl.BlockSpec((1, d), const),
            pl.BlockSpec(memory_space=pl.ANY),
        ],
        out_specs=pl.BlockSpec((tf, d), lambda i: (i, 0)),
        scratch_shapes=[
            pltpu.SMEM((2 * tf,), jnp.int32),
            pltpu.VMEM((2, tf * tr, LANES), jnp.uint32),
            pltpu.SemaphoreType.DMA,
            pltpu.SemaphoreType.DMA,
        ],
        compiler_params=_params(("arbitrary",)),
        name="final",
    )(dest_flat, route, x1, p2d, wple, gple, wgate, gfin, y)


def _tile(n, want):
    t = min(n, want)
    assert n % t == 0, (n, want)
    return t


def kernel(x, p, g_mix, w_in, lb_logits, hg_norm, conv_w, sc_norm, w_out, g_ffn, w_router_group,
           w_router_expert, w_gate, w_up, w_down, w_ple, g_ple, w_ple_gate, g_final):
    b, t, d = x.shape
    n = b * t
    layer = 0
    x2d = x.reshape(n, d)

    lower_bounds = jnp.cumsum(jax.nn.softmax(lb_logits.astype(F32), axis=0), axis=0)
    lb = lower_bounds[layer].reshape(1, HG_WIDTH)

    z = _inproj(x2d, g_mix[layer].reshape(1, d), w_in[layer].astype(BF16), _tile(n, 1024), 1024)
    z3 = z.reshape(b, t, z.shape[1])
    ohg = _hgrn(z3, lb, hg_norm[layer].reshape(1, HEAD_DIM), _tile(t, 1024)).reshape(n, HG_WIDTH)
    osc = _sconv(z3, conv_w[layer], sc_norm[layer].reshape(1, SC_WIDTH), _tile(t, 512)).reshape(n, SC_WIDTH)

    wo = w_out[layer].astype(BF16)
    wr = jnp.concatenate([w_router_group[layer], w_router_expert[layer]], axis=1).astype(F32)
    wr = jnp.pad(wr, ((0, 0), (0, LANES - wr.shape[1])))
    wr_hi = wr.astype(BF16)
    wr_lo = (wr - wr_hi.astype(F32)).astype(BF16)
    x1, hn, route, cnt = _outproj(
        x2d, ohg, osc, wo[:HG_WIDTH], wo[HG_WIDTH:], g_ffn[layer].reshape(1, d),
        jnp.concatenate([wr_hi, wr_lo], axis=1), wr_hi, _tile(n, 512))

    tb = 256
    counts = cnt[0, EXPERT_LANE0:EXPERT_LANE0 + N_EXPERTS].astype(jnp.int32)
    padded = (counts + tb - 1) // tb * tb
    pend = jnp.cumsum(padded)
    pstart = pend - padded
    e_ids = route[:, R_E1:R_E2 + 1].astype(jnp.int32)
    ranks = route[:, R_RANK1:R_RANK2 + 1].astype(jnp.int32)
    expert_iota = jnp.arange(N_EXPERTS, dtype=jnp.int32)
    seg_start = jnp.sum(jnp.where(e_ids[..., None] == expert_iota, pstart.astype(jnp.int32), 0), axis=-1)
    dest_flat = (seg_start + ranks).reshape(2 * n)
    nblocks = -(-(2 * n + N_EXPERTS * (tb - 1)) // tb)
    nused = (pend[-1] // tb).astype(jnp.int32)
    blk = jnp.minimum(jnp.arange(nblocks, dtype=jnp.int32), nused - 1) * tb
    block_e = jnp.minimum(jnp.sum((blk[:, None] >= pend[None, :]).astype(jnp.int32), axis=1), N_EXPERTS - 1)

    p_rows = nblocks * tb
    zero = jnp.zeros((1,), jnp.int32)
    xs = _dispatch(jnp.concatenate([counts, zero]), jnp.concatenate([padded, p_rows - pend[-1:]]),
                   jnp.concatenate([pstart, pend[-1:]]).astype(jnp.int32), dest_flat, hn, n, _token_rows(d),
                   p_rows, _tile(n, 256))
    y = _experts(block_e, nused.reshape(1), xs, w_gate[layer].astype(BF16), w_up[layer].astype(BF16),
                 w_down[layer].astype(BF16), tb)

    out = _final(dest_flat, route, x1, p[layer].reshape(n, -1), w_ple[layer].astype(BF16),
                 g_ple[layer].reshape(1, d), w_ple_gate[layer].astype(BF16), g_final.reshape(1, d), y,
                 _tile(n, 256))
    return out.reshape(b, t, d)
```

```python
import functools

import jax
import jax.numpy as jnp
from jax import lax
from jax.experimental import pallas as pl
from jax.experimental.pallas import tpu as pltpu

F32 = jnp.float32
BF16 = jnp.bfloat16
EPS = 1e-6

HEAD_DIM = 128
N_HEADS = 8
HG_WIDTH = N_HEADS * HEAD_DIM
SC_WIDTH = 1024
CHUNK = 128
N_GROUPS = 4
EXPERTS_PER_GROUP = 8
N_EXPERTS = N_GROUPS * EXPERTS_PER_GROUP
LANES = 128
EXPERT_LANE0 = N_GROUPS
NEG_BIG = -1e30
VMEM_LIMIT = 56 * 1024 * 1024


def _rms(v, gain):
    return v * lax.rsqrt(jnp.mean(v * v, axis=-1, keepdims=True) + EPS) * gain


def _token_rows(d):
    return d // (2 * LANES)


def _pack_tokens(v, out_ref):
    rows, d = v.shape
    tr = _token_rows(d)
    bits = lax.bitcast_convert_type(v.astype(BF16).astype(F32), jnp.uint32)
    words = bits[:, :d // 2] | (bits[:, d // 2:] >> 16)
    for s in range(tr):
        out_ref[pl.ds(s, rows, stride=tr), :] = words[:, s * LANES:(s + 1) * LANES]


def _unpack_tokens(load_rows, tr):
    words = [load_rows(s) for s in range(tr)]
    high = [lax.bitcast_convert_type(w & jnp.uint32(0xFFFF0000), F32) for w in words]
    low = [lax.bitcast_convert_type(w << 16, F32) for w in words]
    return jnp.concatenate(high + low, axis=1)


def _params(sem):
    return pltpu.CompilerParams(dimension_semantics=sem, vmem_limit_bytes=VMEM_LIMIT)


def _inproj_kernel(x_ref, g_ref, w_ref, z_ref, xn_ref):
    @pl.when(pl.program_id(1) == 0)
    def _():
        xn_ref[...] = _rms(x_ref[...], g_ref[...]).astype(BF16)

    z_ref[...] = jnp.dot(xn_ref[...], w_ref[...], preferred_element_type=F32).astype(z_ref.dtype)


def _inproj(x2d, g_mix, w_in_bf16, tm, tn):
    n, d = x2d.shape
    cols = w_in_bf16.shape[1]
    return pl.pallas_call(
        _inproj_kernel,
        out_shape=jax.ShapeDtypeStruct((n, cols), BF16),
        grid=(n // tm, cols // tn),
        in_specs=[
            pl.BlockSpec((tm, d), lambda i, j: (i, 0)),
            pl.BlockSpec((1, d), lambda i, j: (0, 0)),
            pl.BlockSpec((d, tn), lambda i, j: (0, j)),
        ],
        out_specs=pl.BlockSpec((tm, tn), lambda i, j: (i, j)),
        scratch_shapes=[pltpu.VMEM((tm, d), BF16)],
        compiler_params=_params(("arbitrary", "arbitrary")),
        name="inproj",
    )(x2d, g_mix, w_in_bf16)


def _hgrn_kernel(q_ref, f_ref, i_ref, g_ref, lb_ref, gain_ref, o_ref, st_ref, *, nchunks):
    @pl.when(pl.program_id(2) == 0)
    def _():
        st_ref[...] = jnp.zeros_like(st_ref)

    c_len = CHUNK
    n_levels = c_len.bit_length() - 1
    lb = lb_ref[...]
    one_minus_lb = 1.0 - lb
    gain = gain_ref[...]
    row = lax.broadcasted_iota(jnp.int32, (c_len, c_len), 0)
    col = lax.broadcasted_iota(jnp.int32, (c_len, c_len), 1)
    tri = (row >= col).astype(BF16)
    tri2 = jnp.concatenate([tri, tri], axis=1)
    differ = jnp.where(col < row, row ^ col, 0)
    sub8 = lax.broadcasted_iota(jnp.int32, (c_len // 8, 8, HEAD_DIM), 1)
    scale = HEAD_DIM ** -0.5
    nt = (((1,), (1,)), ((), ()))
    sign_bit = jnp.uint32(0x80000000)

    def neg_abs(d):
        return lax.bitcast_convert_type(lax.bitcast_convert_type(d, jnp.uint32) | sign_bit, F32)

    def boundary(a, half):
        if half >= 4:
            nb = c_len // (2 * half)
            a3 = a.reshape(nb, 2 * half, HEAD_DIM)
            return jnp.broadcast_to(a3[:, half - 1:half, :], a3.shape).reshape(c_len, HEAD_DIM)
        assert half == 2
        a3 = a.reshape(c_len // 8, 8, HEAD_DIM)
        return jnp.where(sub8 < 4, a3[:, 1:2, :], a3[:, 5:6, :]).reshape(c_len, HEAD_DIM)

    def body(c, st):
        sl = pl.ds(pl.multiple_of(c * c_len, c_len), c_len)
        fz = f_ref[sl, :].astype(F32)
        sg = jax.nn.sigmoid(fz)
        f = lb + one_minus_lb * sg
        log2_f = jnp.log2(f)
        k = (one_minus_lb * (1.0 - sg)).astype(BF16)
        h1 = log2_f.astype(BF16)
        h2 = (log2_f - h1.astype(F32)).astype(BF16)
        a = jnp.dot(tri2, jnp.concatenate([h1, h2], axis=0), preferred_element_type=F32)
        a_last = a[c_len - 1:c_len, :]
        qz = q_ref[sl, :]
        q = qz * jax.nn.sigmoid(qz) * scale
        qa = q * jnp.exp2(a).astype(BF16)
        kd = k * jnp.exp2(a_last - a).astype(BF16)
        v = i_ref[sl, :]
        diag = jnp.sum(q.astype(F32) * k.astype(F32), axis=-1, keepdims=True)
        scores = jnp.where(row == col, diag, 0.0)
        for level in range(n_levels):
            if level == 0:
                ql, kl = q * f.astype(BF16), k
            else:
                decay = jnp.exp2(neg_abs(a - boundary(a, 1 << level))).astype(BF16)
                ql, kl = q * decay, k * decay
            gram = lax.dot_general(ql, kl, nt, preferred_element_type=F32)
            scores = jnp.where(differ >= (1 << level), gram, scores)
        p = scores.astype(BF16)
        o = lax.dot_general(qa, st.astype(BF16), nt, preferred_element_type=F32)
        o = o + jnp.dot(p, v, preferred_element_type=F32)
        v_t = v.astype(F32).T.astype(BF16)
        st_new = st * jnp.exp2(a_last) + jnp.dot(v_t, kd, preferred_element_type=F32)
        gz = g_ref[sl, :]
        gate = (gz * jax.nn.sigmoid(gz)).astype(F32)
        o_ref[sl, :] = (_rms(o, gain) * gate).astype(o_ref.dtype)
        return st_new

    st_ref[...] = lax.fori_loop(0, nchunks, body, st_ref[...], unroll=True)


def _hgrn(z3, lb, gain, tc):
    b, t, _ = z3.shape
    hb = HG_WIDTH // HEAD_DIM

    def zspec(k):
        return pl.BlockSpec((None, tc, HEAD_DIM), lambda bi, h, ti, k=k: (bi, ti, h + k * hb))

    return pl.pallas_call(
        functools.partial(_hgrn_kernel, nchunks=tc // CHUNK),
        out_shape=jax.ShapeDtypeStruct((b, t, HG_WIDTH), BF16),
        grid=(b, N_HEADS, t // tc),
        in_specs=[
            zspec(0), zspec(1), zspec(2), zspec(3),
            pl.BlockSpec((1, HEAD_DIM), lambda bi, h, ti: (0, h)),
            pl.BlockSpec((1, HEAD_DIM), lambda bi, h, ti: (0, 0)),
        ],
        out_specs=pl.BlockSpec((None, tc, HEAD_DIM), lambda bi, h, ti: (bi, ti, h)),
        scratch_shapes=[pltpu.VMEM((HEAD_DIM, HEAD_DIM), F32)],
        compiler_params=_params(("arbitrary", "arbitrary", "arbitrary")),
        name="hgrn",
    )(z3, z3, z3, z3, lb, gain)


SC_HALO = 8


def _sconv_kernel(b_ref, c_ref, h_ref, w_ref, gain_ref, o_ref, u_ref, *, tc):
    @pl.when(pl.program_id(1) == 0)
    def _():
        u_ref[0:SC_HALO, :] = jnp.zeros((SC_HALO, SC_WIDTH), F32)

    u = c_ref[...].astype(F32) * h_ref[...].astype(F32)
    u_ref[SC_HALO:SC_HALO + tc, :] = u
    w = w_ref[...]
    y = (w[0:1, :] * u_ref[SC_HALO - 2:SC_HALO - 2 + tc, :]
         + w[1:2, :] * u_ref[SC_HALO - 1:SC_HALO - 1 + tc, :]
         + w[2:3, :] * u)
    o_ref[...] = _rms(b_ref[...].astype(F32) * y, gain_ref[...]).astype(o_ref.dtype)
    u_ref[0:SC_HALO, :] = u_ref[tc:tc + SC_HALO, :]


def _sconv(z3, conv_w, gain, tc):
    b, t, _ = z3.shape
    first = 4 * HG_WIDTH // SC_WIDTH

    def zspec(k):
        return pl.BlockSpec((None, tc, SC_WIDTH), lambda bi, ti, k=k: (bi, ti, first + k))

    return pl.pallas_call(
        functools.partial(_sconv_kernel, tc=tc),
        out_shape=jax.ShapeDtypeStruct((b, t, SC_WIDTH), BF16),
        grid=(b, t // tc),
        in_specs=[
            zspec(0), zspec(1), zspec(2),
            pl.BlockSpec((3, SC_WIDTH), lambda bi, ti: (0, 0)),
            pl.BlockSpec((1, SC_WIDTH), lambda bi, ti: (0, 0)),
        ],
        out_specs=pl.BlockSpec((None, tc, SC_WIDTH), lambda bi, ti: (bi, ti, 0)),
        scratch_shapes=[pltpu.VMEM((tc + SC_HALO, SC_WIDTH), F32)],
        compiler_params=_params(("arbitrary", "arbitrary")),
        name="sconv",
    )(z3, z3, z3, conv_w, gain)


R_E1, R_E2, R_W1, R_W2, R_RANK1, R_RANK2 = 0, 1, 2, 3, 4, 5


def _outproj_kernel(x_ref, ohg_ref, osc_ref, wo1_ref, wo2_ref, g_ref, wr_hl_ref, wr_h_ref,
                    x1_ref, hn_ref, route_ref, cnt_ref, carry_ref, *, tm):
    @pl.when(pl.program_id(0) == 0)
    def _():
        carry_ref[...] = jnp.zeros_like(carry_ref)

    acc = jnp.dot(ohg_ref[...], wo1_ref[...], preferred_element_type=F32)
    acc = acc + jnp.dot(osc_ref[...], wo2_ref[...], preferred_element_type=F32)
    x1 = x_ref[...] + acc
    x1_ref[...] = x1
    hn = _rms(x1, g_ref[...])
    _pack_tokens(hn, hn_ref)

    hi = hn.astype(BF16)
    lo = (hn - hi.astype(F32)).astype(BF16)
    both = jnp.dot(hi, wr_hl_ref[...], preferred_element_type=F32)
    logits = both[:, :LANES] + both[:, LANES:] + jnp.dot(lo, wr_h_ref[...], preferred_element_type=F32)

    lane = lax.broadcasted_iota(jnp.int32, (tm, LANES), 1)

    def first_argmax(vals, vmax):
        return jnp.min(jnp.where(vals == vmax, lane, LANES), axis=-1, keepdims=True)

    is_group = lane < N_GROUPS
    gl = jnp.where(is_group, logits, NEG_BIG)
    gmax = jnp.max(gl, axis=-1, keepdims=True)
    gidx = first_argmax(gl, gmax)
    g_p = 1.0 / jnp.sum(jnp.where(is_group, jnp.exp(gl - gmax), 0.0), axis=-1, keepdims=True)

    e_lane = lane - EXPERT_LANE0
    in_group = (e_lane >= 0) & (e_lane < N_EXPERTS) & ((e_lane // EXPERTS_PER_GROUP) == gidx)
    el = jnp.where(in_group, logits, NEG_BIG)
    m1 = jnp.max(el, axis=-1, keepdims=True)
    i1 = first_argmax(el, m1)
    el2 = jnp.where(lane == i1, NEG_BIG, el)
    m2 = jnp.max(el2, axis=-1, keepdims=True)
    i2 = first_argmax(el2, m2)
    r = jnp.exp(m2 - m1)
    w1 = g_p / (1.0 + r)
    w2 = g_p * r / (1.0 + r)

    oh1 = lane == i1
    oh2 = lane == i2
    onehot = (oh1 | oh2).astype(BF16)
    trow = lax.broadcasted_iota(jnp.int32, (tm, tm), 0)
    tcol = lax.broadcasted_iota(jnp.int32, (tm, tm), 1)
    before = jnp.dot((trow > tcol).astype(BF16), onehot, preferred_element_type=F32) + carry_ref[...]
    rank1 = jnp.sum(jnp.where(oh1, before, 0.0), axis=-1, keepdims=True)
    rank2 = jnp.sum(jnp.where(oh2, before, 0.0), axis=-1, keepdims=True)
    carry_ref[...] = carry_ref[...] + jnp.sum(onehot.astype(F32), axis=0, keepdims=True)
    cnt_ref[...] = carry_ref[...]

    table = jnp.zeros((tm, LANES), F32)
    for slot, val in ((R_E1, (i1 - EXPERT_LANE0).astype(F32)), (R_E2, (i2 - EXPERT_LANE0).astype(F32)),
                      (R_W1, w1), (R_W2, w2), (R_RANK1, rank1), (R_RANK2, rank2)):
        table = jnp.where(lane == slot, val, table)
    route_ref[...] = table


def _outproj(x2d, ohg, osc, wo1, wo2, g_ffn, wr_hl, wr_h, tm):
    n, d = x2d.shape
    const = lambda i: (0, 0)
    return pl.pallas_call(
        functools.partial(_outproj_kernel, tm=tm),
        out_shape=(
            jax.ShapeDtypeStruct((n, d), F32),
            jax.ShapeDtypeStruct((n * _token_rows(d), LANES), jnp.uint32),
            jax.ShapeDtypeStruct((n, LANES), F32),
            jax.ShapeDtypeStruct((1, LANES), F32),
        ),
        grid=(n // tm,),
        in_specs=[
            pl.BlockSpec((tm, d), lambda i: (i, 0)),
            pl.BlockSpec((tm, HG_WIDTH), lambda i: (i, 0)),
            pl.BlockSpec((tm, SC_WIDTH), lambda i: (i, 0)),
            pl.BlockSpec((HG_WIDTH, d), const),
            pl.BlockSpec((SC_WIDTH, d), const),
            pl.BlockSpec((1, d), const),
            pl.BlockSpec((d, 2 * LANES), const),
            pl.BlockSpec((d, LANES), const),
        ],
        out_specs=(
            pl.BlockSpec((tm, d), lambda i: (i, 0)),
            pl.BlockSpec((tm * _token_rows(d), LANES), lambda i: (i, 0)),
            pl.BlockSpec((tm, LANES), lambda i: (i, 0)),
            pl.BlockSpec((1, LANES), const),
        ),
        scratch_shapes=[pltpu.VMEM((1, LANES), F32)],
        compiler_params=_params(("arbitrary",)),
        name="outproj",
    )(x2d, ohg, osc, wo1, wo2, g_ffn, wr_hl, wr_h)


def _dispatch_kernel(cnt_ref, pad_ref, start_ref, dest_hbm, hn_ref, xs_hbm, idx_ref, zero_ref, idx_sem, sem,
                     *, td, tr, nsteps):
    i = pl.program_id(0)
    idx_copy = pltpu.make_async_copy(dest_hbm.at[pl.ds(i * 2 * td, 2 * td)], idx_ref, idx_sem)
    idx_copy.start()
    idx_copy.wait()

    def token_copy(t, slot):
        src = hn_ref.at[pl.ds(pl.multiple_of(t * tr, tr), tr)]
        return pltpu.make_async_copy(src, xs_hbm.at[pl.ds(pl.multiple_of(slot * tr, tr), tr)], sem)

    def issue(t, carry):
        token_copy(t, idx_ref[2 * t]).start(priority=0)
        token_copy(t, idx_ref[2 * t + 1]).start(priority=1)
        return carry

    lax.fori_loop(0, td, issue, 0, unroll=8)
    tile_copy = pltpu.make_async_copy(hn_ref, xs_hbm.at[pl.ds(0, td * tr)], sem)
    tile_copy.wait()
    tile_copy.wait()

    @pl.when(i == nsteps - 1)
    def _():
        zero_ref[...] = jnp.zeros_like(zero_ref)

        def pad_copy(slot):
            return pltpu.make_async_copy(zero_ref, xs_hbm.at[pl.ds(pl.multiple_of(slot * tr, tr), tr)], sem)

        def per_expert(e, carry):
            base = start_ref[e]

            def issue_pad(s, c2):
                pad_copy(base + s).start()
                return c2

            def drain_pad(s, c2):
                pad_copy(0).wait()
                return c2

            lax.fori_loop(cnt_ref[e], pad_ref[e], issue_pad, 0)
            lax.fori_loop(cnt_ref[e], pad_ref[e], drain_pad, 0)
            return carry

        lax.fori_loop(0, N_EXPERTS + 1, per_expert, 0)


def _dispatch(counts, padded, pstart, dest_flat, hn_packed, n, tr, p_rows, td):
    nsteps = n // td
    grid_spec = pltpu.PrefetchScalarGridSpec(
        num_scalar_prefetch=3,
        grid=(nsteps,),
        in_specs=[
            pl.BlockSpec(memory_space=pl.ANY),
            pl.BlockSpec((td * tr, LANES), lambda i, *_: (i, 0)),
        ],
        out_specs=pl.BlockSpec(memory_space=pl.ANY),
        scratch_shapes=[
            pltpu.SMEM((2 * td,), jnp.int32),
            pltpu.VMEM((tr, LANES), jnp.uint32),
            pltpu.SemaphoreType.DMA,
            pltpu.SemaphoreType.DMA,
        ],
    )
    return pl.pallas_call(
        functools.partial(_dispatch_kernel, td=td, tr=tr, nsteps=nsteps),
        out_shape=jax.ShapeDtypeStruct((p_rows * tr, LANES), jnp.uint32),
        grid_spec=grid_spec,
        compiler_params=_params(("arbitrary",)),
        name="dispatch",
    )(counts, padded, pstart, dest_flat, hn_packed)


def _experts_kernel(blk_e_ref, nused_ref, xs_ref, wg_ref, wu_ref, wd_ref, y_ref, *, tb, tr):
    @pl.when(pl.program_id(0) >= nused_ref[0])
    def _():
        y_ref[...] = jnp.zeros_like(y_ref)

    @pl.when(pl.program_id(0) < nused_ref[0])
    def _():
        x = _unpack_tokens(lambda s: xs_ref[pl.ds(s, tb, stride=tr), :], tr).astype(BF16)
        a = jnp.dot(x, wg_ref[...], preferred_element_type=F32)
        b = jnp.dot(x, wu_ref[...], preferred_element_type=F32)
        h = (a * jax.nn.sigmoid(a) * b).astype(BF16)
        _pack_tokens(jnp.dot(h, wd_ref[...], preferred_element_type=F32), y_ref)


def _experts(block_e, nused, xs_packed, wg, wu, wd, tb):
    d, ff = wg.shape[1], wg.shape[2]
    tr = _token_rows(d)
    nblocks = xs_packed.shape[0] // (tb * tr)

    def row_map(i, blk_e, nu):
        return (jnp.minimum(i, nu[0] - 1), 0)

    def w_map(i, blk_e, nu):
        return (blk_e[i], 0, 0)

    grid_spec = pltpu.PrefetchScalarGridSpec(
        num_scalar_prefetch=2,
        grid=(nblocks,),
        in_specs=[
            pl.BlockSpec((tb * tr, LANES), row_map),
            pl.BlockSpec((None, d, ff), w_map),
            pl.BlockSpec((None, d, ff), w_map),
            pl.BlockSpec((None, ff, d), w_map),
        ],
        out_specs=pl.BlockSpec((tb * tr, LANES), lambda i, blk_e, nu: (i, 0)),
    )
    return pl.pallas_call(
        functools.partial(_experts_kernel, tb=tb, tr=tr),
        out_shape=jax.ShapeDtypeStruct(xs_packed.shape, jnp.uint32),
        grid_spec=grid_spec,
        compiler_params=_params(("arbitrary",)),
        name="experts",
    )(block_e, nused, xs_packed, wg, wu, wd)


def _final_kernel(dest_hbm, route_ref, x1_ref, p_ref, wple_ref, gple_ref, wgate_ref, gfin_ref, y_hbm,
                  o_ref, idx_ref, ybuf_ref, idx_sem, sem, *, tf, tr, nsteps):
    i = pl.program_id(0)
    slot = i % 2

    def idx_copy(step, s):
        return pltpu.make_async_copy(dest_hbm.at[pl.ds(step * 2 * tf, 2 * tf)], idx_ref.at[s], idx_sem.at[s])

    def gather(s):
        def token_copy(t, k):
            src = y_hbm.at[pl.ds(pl.multiple_of(idx_ref[s, 2 * t + k] * tr, tr), tr)]
            dst = ybuf_ref.at[s, k, pl.ds(pl.multiple_of(t * tr, tr), tr)]
            return pltpu.make_async_copy(src, dst, sem.at[s])

        def issue(t, carry):
            token_copy(t, 0).start(priority=0)
            token_copy(t, 1).start(priority=1)
            return carry

        lax.fori_loop(0, tf, issue, 0, unroll=8)

    @pl.when(i == 0)
    def _():
        idx_copy(0, 0).start()
        idx_copy(0, 0).wait()
        gather(0)
        if nsteps > 1:
            idx_copy(1, 1).start()

    @pl.when(i + 1 < nsteps)
    def _():
        idx_copy(i + 1, 1 - slot).wait()
        gather(1 - slot)

    @pl.when(i + 2 < nsteps)
    def _():
        idx_copy(i + 2, slot).start()

    ple = _rms(jnp.dot(p_ref[...].astype(BF16), wple_ref[...], preferred_element_type=F32), gple_ref[...])

    for k in range(2):
        pltpu.make_async_copy(y_hbm.at[pl.ds(0, tf * tr)], ybuf_ref.at[slot, k], sem.at[slot]).wait()

    route = route_ref[...]
    w1 = route[:, R_W1:R_W1 + 1]
    w2 = route[:, R_W2:R_W2 + 1]
    y1 = _unpack_tokens(lambda s: ybuf_ref[slot, 0, pl.ds(s, tf, stride=tr), :], tr)
    y2 = _unpack_tokens(lambda s: ybuf_ref[slot, 1, pl.ds(s, tf, stride=tr), :], tr)
    x2 = x1_ref[...] + w1 * y1 + w2 * y2
    gate = jax.nn.sigmoid(jnp.dot(x2.astype(BF16), wgate_ref[...], preferred_element_type=F32))
    o_ref[...] = _rms(x2 + gate * ple, gfin_ref[...])


def _final(dest_flat, route, x1, p2d, wple, gple, wgate, gfin, y, tf):
    n, d = x1.shape
    pd = p2d.shape[1]
    tr = _token_rows(d)
    const = lambda i: (0, 0)
    return pl.pallas_call(
        functools.partial(_final_kernel, tf=tf, tr=tr, nsteps=n // tf),
        out_shape=jax.ShapeDtypeStruct((n, d), F32),
        grid=(n // tf,),
        in_specs=[
            pl.BlockSpec(memory_space=pl.ANY),
            pl.BlockSpec((tf, LANES), lambda i: (i, 0)),
            pl.BlockSpec((tf, d), lambda i: (i, 0)),
            pl.BlockSpec((tf, pd), lambda i: (i, 0)),
            pl.BlockSpec((pd, d), const),
            pl.BlockSpec((1, d), const),
            pl.BlockSpec((d, d), const),
            pl.BlockSpec((1, d), const),
            pl.BlockSpec(memory_space=pl.ANY),
        ],
        out_specs=pl.BlockSpec((tf, d), lambda i: (i, 0)),
        scratch_shapes=[
            pltpu.SMEM((2, 2 * tf), jnp.int32),
            pltpu.VMEM((2, 2, tf * tr, LANES), jnp.uint32),
            pltpu.SemaphoreType.DMA((2,)),
            pltpu.SemaphoreType.DMA((2,)),
        ],
        compiler_params=_params(("arbitrary",)),
        name="final",
    )(dest_flat, route, x1, p2d, wple, gple, wgate, gfin, y)


def _tile(n, want):
    t = min(n, want)
    assert n % t == 0, (n, want)
    return t


def kernel(x, p, g_mix, w_in, lb_logits, hg_norm, conv_w, sc_norm, w_out, g_ffn, w_router_group,
           w_router_expert, w_gate, w_up, w_down, w_ple, g_ple, w_ple_gate, g_final):
    b, t, d = x.shape
    n = b * t
    layer = 0
    x2d = x.reshape(n, d)

    lower_bounds = jnp.cumsum(jax.nn.softmax(lb_logits.astype(F32), axis=0), axis=0)
    lb = lower_bounds[layer].reshape(1, HG_WIDTH)

    z = _inproj(x2d, g_mix[layer].reshape(1, d), w_in[layer].astype(BF16), _tile(n, 1024), 1024)
    z3 = z.reshape(b, t, z.shape[1])
    ohg = _hgrn(z3, lb, hg_norm[layer].reshape(1, HEAD_DIM), _tile(t, 1024)).reshape(n, HG_WIDTH)
    osc = _sconv(z3, conv_w[layer], sc_norm[layer].reshape(1, SC_WIDTH), _tile(t, 512)).reshape(n, SC_WIDTH)

    wo = w_out[layer].astype(BF16)
    wr = jnp.concatenate([w_router_group[layer], w_router_expert[layer]], axis=1).astype(F32)
    wr = jnp.pad(wr, ((0, 0), (0, LANES - wr.shape[1])))
    wr_hi = wr.astype(BF16)
    wr_lo = (wr - wr_hi.astype(F32)).astype(BF16)
    x1, hn, route, cnt = _outproj(
        x2d, ohg, osc, wo[:HG_WIDTH], wo[HG_WIDTH:], g_ffn[layer].reshape(1, d),
        jnp.concatenate([wr_hi, wr_lo], axis=1), wr_hi, _tile(n, 512))

    tb = 256
    counts = cnt[0, EXPERT_LANE0:EXPERT_LANE0 + N_EXPERTS].astype(jnp.int32)
    padded = (counts + tb - 1) // tb * tb
    pend = jnp.cumsum(padded)
    pstart = pend - padded
    e_ids = route[:, R_E1:R_E2 + 1].astype(jnp.int32)
    ranks = route[:, R_RANK1:R_RANK2 + 1].astype(jnp.int32)
    expert_iota = jnp.arange(N_EXPERTS, dtype=jnp.int32)
    seg_start = jnp.sum(jnp.where(e_ids[..., None] == expert_iota, pstart.astype(jnp.int32), 0), axis=-1)
    dest_flat = (seg_start + ranks).reshape(2 * n)
    nblocks = -(-(2 * n + N_EXPERTS * (tb - 1)) // tb)
    nused = (pend[-1] // tb).astype(jnp.int32)
    blk = jnp.minimum(jnp.arange(nblocks, dtype=jnp.int32), nused - 1) * tb
    block_e = jnp.minimum(jnp.sum((blk[:, None] >= pend[None, :]).astype(jnp.int32), axis=1), N_EXPERTS - 1)

    p_rows = nblocks * tb
    zero = jnp.zeros((1,), jnp.int32)
    xs = _dispatch(jnp.concatenate([counts, zero]), jnp.concatenate([padded, p_rows - pend[-1:]]),
                   jnp.concatenate([pstart, pend[-1:]]).astype(jnp.int32), dest_flat, hn, n, _token_rows(d),
                   p_rows, _tile(n, 1024))
    y = _experts(block_e, nused.reshape(1), xs, w_gate[layer].astype(BF16), w_up[layer].astype(BF16),
                 w_down[layer].astype(BF16), tb)

    out = _final(dest_flat, route, x1, p[layer].reshape(n, -1), w_ple[layer].astype(BF16),
                 g_ple[layer].reshape(1, d), w_ple_gate[layer].astype(BF16), g_final.reshape(1, d), y,
                 _tile(n, 256))
    return out.reshape(b, t, d)
```

```python
import functools

import jax
import jax.numpy as jnp
from jax import lax
from jax.experimental import pallas as pl
from jax.experimental.pallas import tpu as pltpu

F32 = jnp.float32
BF16 = jnp.bfloat16
EPS = 1e-6

HEAD_DIM = 128
N_HEADS = 8
HG_WIDTH = N_HEADS * HEAD_DIM
SC_WIDTH = 1024
CHUNK = 128
N_GROUPS = 4
EXPERTS_PER_GROUP = 8
N_EXPERTS = N_GROUPS * EXPERTS_PER_GROUP
LANES = 128
EXPERT_LANE0 = N_GROUPS
NEG_BIG = -1e30
VMEM_LIMIT = 56 * 1024 * 1024


def _rms(v, gain):
    return v * lax.rsqrt(jnp.mean(v * v, axis=-1, keepdims=True) + EPS) * gain


def _token_rows(d):
    return d // (2 * LANES)


def _pack_tokens(v, out_ref):
    rows, d = v.shape
    tr = _token_rows(d)
    bits = lax.bitcast_convert_type(v.astype(BF16).astype(F32), jnp.uint32)
    words = bits[:, :d // 2] | (bits[:, d // 2:] >> 16)
    for s in range(tr):
        out_ref[pl.ds(s, rows, stride=tr), :] = words[:, s * LANES:(s + 1) * LANES]


def _unpack_tokens(load_rows, tr):
    words = [load_rows(s) for s in range(tr)]
    high = [lax.bitcast_convert_type(w & jnp.uint32(0xFFFF0000), F32) for w in words]
    low = [lax.bitcast_convert_type(w << 16, F32) for w in words]
    return jnp.concatenate(high + low, axis=1)


def _params(sem):
    return pltpu.CompilerParams(dimension_semantics=sem, vmem_limit_bytes=VMEM_LIMIT)


def _inproj_kernel(x_ref, g_ref, w_ref, z_ref, xn_ref):
    @pl.when(pl.program_id(1) == 0)
    def _():
        xn_ref[...] = _rms(x_ref[...], g_ref[...]).astype(BF16)

    z_ref[...] = jnp.dot(xn_ref[...], w_ref[...], preferred_element_type=F32).astype(z_ref.dtype)


def _inproj(x2d, g_mix, w_in_bf16, tm, tn):
    n, d = x2d.shape
    cols = w_in_bf16.shape[1]
    return pl.pallas_call(
        _inproj_kernel,
        out_shape=jax.ShapeDtypeStruct((n, cols), BF16),
        grid=(n // tm, cols // tn),
        in_specs=[
            pl.BlockSpec((tm, d), lambda i, j: (i, 0)),
            pl.BlockSpec((1, d), lambda i, j: (0, 0)),
            pl.BlockSpec((d, tn), lambda i, j: (0, j)),
        ],
        out_specs=pl.BlockSpec((tm, tn), lambda i, j: (i, j)),
        scratch_shapes=[pltpu.VMEM((tm, d), BF16)],
        compiler_params=_params(("arbitrary", "arbitrary")),
        name="inproj",
    )(x2d, g_mix, w_in_bf16)


def _hgrn_kernel(q_ref, f_ref, i_ref, g_ref, lb_ref, gain_ref, o_ref, st_ref, *, nchunks):
    @pl.when(pl.program_id(2) == 0)
    def _():
        st_ref[...] = jnp.zeros_like(st_ref)

    c_len = CHUNK
    n_levels = c_len.bit_length() - 1
    lb = lb_ref[...]
    one_minus_lb = 1.0 - lb
    gain = gain_ref[...]
    row = lax.broadcasted_iota(jnp.int32, (c_len, c_len), 0)
    col = lax.broadcasted_iota(jnp.int32, (c_len, c_len), 1)
    tri = (row >= col).astype(BF16)
    tri2 = jnp.concatenate([tri, tri], axis=1)
    differ = jnp.where(col < row, row ^ col, 0)
    sub8 = lax.broadcasted_iota(jnp.int32, (c_len // 8, 8, HEAD_DIM), 1)
    scale = HEAD_DIM ** -0.5
    nt = (((1,), (1,)), ((), ()))
    sign_bit = jnp.uint32(0x80000000)

    def neg_abs(d):
        return lax.bitcast_convert_type(lax.bitcast_convert_type(d, jnp.uint32) | sign_bit, F32)

    def boundary(a, half):
        if half >= 4:
            nb = c_len // (2 * half)
            a3 = a.reshape(nb, 2 * half, HEAD_DIM)
            return jnp.broadcast_to(a3[:, half - 1:half, :], a3.shape).reshape(c_len, HEAD_DIM)
        assert half == 2
        a3 = a.reshape(c_len // 8, 8, HEAD_DIM)
        return jnp.where(sub8 < 4, a3[:, 1:2, :], a3[:, 5:6, :]).reshape(c_len, HEAD_DIM)

    def body(c, st):
        sl = pl.ds(pl.multiple_of(c * c_len, c_len), c_len)
        fz = f_ref[sl, :].astype(F32)
        sg = jax.nn.sigmoid(fz)
        f = lb + one_minus_lb * sg
        log2_f = jnp.log2(f)
        k = (one_minus_lb * (1.0 - sg)).astype(BF16)
        h1 = log2_f.astype(BF16)
        h2 = (log2_f - h1.astype(F32)).astype(BF16)
        a = jnp.dot(tri2, jnp.concatenate([h1, h2], axis=0), preferred_element_type=F32)
        a_last = a[c_len - 1:c_len, :]
        qz = q_ref[sl, :]
        q = qz * jax.nn.sigmoid(qz) * scale
        qa = q * jnp.exp2(a).astype(BF16)
        kd = k * jnp.exp2(a_last - a).astype(BF16)
        v = i_ref[sl, :]
        diag = jnp.sum(q.astype(F32) * k.astype(F32), axis=-1, keepdims=True)
        scores = jnp.where(row == col, diag, 0.0)
        for level in range(n_levels):
            if level == 0:
                ql, kl = q * f.astype(BF16), k
            else:
                decay = jnp.exp2(neg_abs(a - boundary(a, 1 << level))).astype(BF16)
                ql, kl = q * decay, k * decay
            gram = lax.dot_general(ql, kl, nt, preferred_element_type=F32)
            scores = jnp.where(differ >= (1 << level), gram, scores)
        p = scores.astype(BF16)
        o = lax.dot_general(qa, st.astype(BF16), nt, preferred_element_type=F32)
        o = o + jnp.dot(p, v, preferred_element_type=F32)
        v_t = v.astype(F32).T.astype(BF16)
        st_new = st * jnp.exp2(a_last) + jnp.dot(v_t, kd, preferred_element_type=F32)
        gz = g_ref[sl, :]
        gate = (gz * jax.nn.sigmoid(gz)).astype(F32)
        o_ref[sl, :] = (_rms(o, gain) * gate).astype(o_ref.dtype)
        return st_new

    st_ref[...] = lax.fori_loop(0, nchunks, body, st_ref[...], unroll=True)


def _hgrn(z3, lb, gain, tc):
    b, t, _ = z3.shape
    hb = HG_WIDTH // HEAD_DIM

    def zspec(k):
        return pl.BlockSpec((None, tc, HEAD_DIM), lambda bi, h, ti, k=k: (bi, ti, h + k * hb))

    return pl.pallas_call(
        functools.partial(_hgrn_kernel, nchunks=tc // CHUNK),
        out_shape=jax.ShapeDtypeStruct((b, t, HG_WIDTH), BF16),
        grid=(b, N_HEADS, t // tc),
        in_specs=[
            zspec(0), zspec(1), zspec(2), zspec(3),
            pl.BlockSpec((1, HEAD_DIM), lambda bi, h, ti: (0, h)),
            pl.BlockSpec((1, HEAD_DIM), lambda bi, h, ti: (0, 0)),
        ],
        out_specs=pl.BlockSpec((None, tc, HEAD_DIM), lambda bi, h, ti: (bi, ti, h)),
        scratch_shapes=[pltpu.VMEM((HEAD_DIM, HEAD_DIM), F32)],
        compiler_params=_params(("arbitrary", "arbitrary", "arbitrary")),
        name="hgrn",
    )(z3, z3, z3, z3, lb, gain)


SC_HALO = 8


def _sconv_kernel(b_ref, c_ref, h_ref, w_ref, gain_ref, o_ref, u_ref, *, tc):
    @pl.when(pl.program_id(1) == 0)
    def _():
        u_ref[0:SC_HALO, :] = jnp.zeros((SC_HALO, SC_WIDTH), F32)

    u = c_ref[...].astype(F32) * h_ref[...].astype(F32)
    u_ref[SC_HALO:SC_HALO + tc, :] = u
    w = w_ref[...]
    y = (w[0:1, :] * u_ref[SC_HALO - 2:SC_HALO - 2 + tc, :]
         + w[1:2, :] * u_ref[SC_HALO - 1:SC_HALO - 1 + tc, :]
         + w[2:3, :] * u)
    o_ref[...] = _rms(b_ref[...].astype(F32) * y, gain_ref[...]).astype(o_ref.dtype)
    u_ref[0:SC_HALO, :] = u_ref[tc:tc + SC_HALO, :]


def _sconv(z3, conv_w, gain, tc):
    b, t, _ = z3.shape
    first = 4 * HG_WIDTH // SC_WIDTH

    def zspec(k):
        return pl.BlockSpec((None, tc, SC_WIDTH), lambda bi, ti, k=k: (bi, ti, first + k))

    return pl.pallas_call(
        functools.partial(_sconv_kernel, tc=tc),
        out_shape=jax.ShapeDtypeStruct((b, t, SC_WIDTH), BF16),
        grid=(b, t // tc),
        in_specs=[
            zspec(0), zspec(1), zspec(2),
            pl.BlockSpec((3, SC_WIDTH), lambda bi, ti: (0, 0)),
            pl.BlockSpec((1, SC_WIDTH), lambda bi, ti: (0, 0)),
        ],
        out_specs=pl.BlockSpec((None, tc, SC_WIDTH), lambda bi, ti: (bi, ti, 0)),
        scratch_shapes=[pltpu.VMEM((tc + SC_HALO, SC_WIDTH), F32)],
        compiler_params=_params(("arbitrary", "arbitrary")),
        name="sconv",
    )(z3, z3, z3, conv_w, gain)


R_E1, R_E2, R_W1, R_W2, R_RANK1, R_RANK2 = 0, 1, 2, 3, 4, 5


def _outproj_kernel(x_ref, ohg_ref, osc_ref, wo1_ref, wo2_ref, g_ref, wr_hl_ref, wr_h_ref,
                    x1_ref, hn_ref, route_ref, cnt_ref, carry_ref, *, tm):
    @pl.when(pl.program_id(0) == 0)
    def _():
        carry_ref[...] = jnp.zeros_like(carry_ref)

    acc = jnp.dot(ohg_ref[...], wo1_ref[...], preferred_element_type=F32)
    acc = acc + jnp.dot(osc_ref[...], wo2_ref[...], preferred_element_type=F32)
    x1 = x_ref[...] + acc
    x1_ref[...] = x1
    hn = _rms(x1, g_ref[...])
    _pack_tokens(hn, hn_ref)

    hi = hn.astype(BF16)
    lo = (hn - hi.astype(F32)).astype(BF16)
    both = jnp.dot(hi, wr_hl_ref[...], preferred_element_type=F32)
    logits = both[:, :LANES] + both[:, LANES:] + jnp.dot(lo, wr_h_ref[...], preferred_element_type=F32)

    lane = lax.broadcasted_iota(jnp.int32, (tm, LANES), 1)

    def first_argmax(vals, vmax):
        return jnp.min(jnp.where(vals == vmax, lane, LANES), axis=-1, keepdims=True)

    is_group = lane < N_GROUPS
    gl = jnp.where(is_group, logits, NEG_BIG)
    gmax = jnp.max(gl, axis=-1, keepdims=True)
    gidx = first_argmax(gl, gmax)
    g_p = 1.0 / jnp.sum(jnp.where(is_group, jnp.exp(gl - gmax), 0.0), axis=-1, keepdims=True)

    e_lane = lane - EXPERT_LANE0
    in_group = (e_lane >= 0) & (e_lane < N_EXPERTS) & ((e_lane // EXPERTS_PER_GROUP) == gidx)
    el = jnp.where(in_group, logits, NEG_BIG)
    m1 = jnp.max(el, axis=-1, keepdims=True)
    i1 = first_argmax(el, m1)
    el2 = jnp.where(lane == i1, NEG_BIG, el)
    m2 = jnp.max(el2, axis=-1, keepdims=True)
    i2 = first_argmax(el2, m2)
    r = jnp.exp(m2 - m1)
    w1 = g_p / (1.0 + r)
    w2 = g_p * r / (1.0 + r)

    oh1 = lane == i1
    oh2 = lane == i2
    onehot = (oh1 | oh2).astype(BF16)
    trow = lax.broadcasted_iota(jnp.int32, (tm, tm), 0)
    tcol = lax.broadcasted_iota(jnp.int32, (tm, tm), 1)
    before = jnp.dot((trow > tcol).astype(BF16), onehot, preferred_element_type=F32) + carry_ref[...]
    rank1 = jnp.sum(jnp.where(oh1, before, 0.0), axis=-1, keepdims=True)
    rank2 = jnp.sum(jnp.where(oh2, before, 0.0), axis=-1, keepdims=True)
    carry_ref[...] = carry_ref[...] + jnp.sum(onehot.astype(F32), axis=0, keepdims=True)
    cnt_ref[...] = carry_ref[...]

    table = jnp.zeros((tm, LANES), F32)
    for slot, val in ((R_E1, (i1 - EXPERT_LANE0).astype(F32)), (R_E2, (i2 - EXPERT_LANE0).astype(F32)),
                      (R_W1, w1), (R_W2, w2), (R_RANK1, rank1), (R_RANK2, rank2)):
        table = jnp.where(lane == slot, val, table)
    route_ref[...] = table


def _outproj(x2d, ohg, osc, wo1, wo2, g_ffn, wr_hl, wr_h, tm):
    n, d = x2d.shape
    const = lambda i: (0, 0)
    return pl.pallas_call(
        functools.partial(_outproj_kernel, tm=tm),
        out_shape=(
            jax.ShapeDtypeStruct((n, d), F32),
            jax.ShapeDtypeStruct((n * _token_rows(d), LANES), jnp.uint32),
            jax.ShapeDtypeStruct((n, LANES), F32),
            jax.ShapeDtypeStruct((1, LANES), F32),
        ),
        grid=(n // tm,),
        in_specs=[
            pl.BlockSpec((tm, d), lambda i: (i, 0)),
            pl.BlockSpec((tm, HG_WIDTH), lambda i: (i, 0)),
            pl.BlockSpec((tm, SC_WIDTH), lambda i: (i, 0)),
            pl.BlockSpec((HG_WIDTH, d), const),
            pl.BlockSpec((SC_WIDTH, d), const),
            pl.BlockSpec((1, d), const),
            pl.BlockSpec((d, 2 * LANES), const),
            pl.BlockSpec((d, LANES), const),
        ],
        out_specs=(
            pl.BlockSpec((tm, d), lambda i: (i, 0)),
            pl.BlockSpec((tm * _token_rows(d), LANES), lambda i: (i, 0)),
            pl.BlockSpec((tm, LANES), lambda i: (i, 0)),
            pl.BlockSpec((1, LANES), const),
        ),
        scratch_shapes=[pltpu.VMEM((1, LANES), F32)],
        compiler_params=_params(("arbitrary",)),
        name="outproj",
    )(x2d, ohg, osc, wo1, wo2, g_ffn, wr_hl, wr_h)


def _dispatch_kernel(cnt_ref, pad_ref, start_ref, dest_hbm, hn_ref, xs_hbm, idx_ref, zero_ref, idx_sem, sem,
                     *, td, tr, nsteps):
    i = pl.program_id(0)
    idx_copy = pltpu.make_async_copy(dest_hbm.at[pl.ds(i * 2 * td, 2 * td)], idx_ref, idx_sem)
    idx_copy.start()
    idx_copy.wait()

    def token_copy(t, slot):
        src = hn_ref.at[pl.ds(pl.multiple_of(t * tr, tr), tr)]
        return pltpu.make_async_copy(src, xs_hbm.at[pl.ds(pl.multiple_of(slot * tr, tr), tr)], sem)

    def issue(t, carry):
        token_copy(t, idx_ref[2 * t]).start(priority=0)
        token_copy(t, idx_ref[2 * t + 1]).start(priority=1)
        return carry

    lax.fori_loop(0, td, issue, 0, unroll=8)
    tile_copy = pltpu.make_async_copy(hn_ref, xs_hbm.at[pl.ds(0, td * tr)], sem)
    tile_copy.wait()
    tile_copy.wait()

    @pl.when(i == nsteps - 1)
    def _():
        zero_ref[...] = jnp.zeros_like(zero_ref)

        def pad_copy(slot):
            return pltpu.make_async_copy(zero_ref, xs_hbm.at[pl.ds(pl.multiple_of(slot * tr, tr), tr)], sem)

        def per_expert(e, carry):
            base = start_ref[e]

            def issue_pad(s, c2):
                pad_copy(base + s).start()
                return c2

            def drain_pad(s, c2):
                pad_copy(0).wait()
                return c2

            lax.fori_loop(cnt_ref[e], pad_ref[e], issue_pad, 0)
            lax.fori_loop(cnt_ref[e], pad_ref[e], drain_pad, 0)
            return carry

        lax.fori_loop(0, N_EXPERTS + 1, per_expert, 0)


def _dispatch(counts, padded, pstart, dest_flat, hn_packed, n, tr, p_rows, td):
    nsteps = n // td
    grid_spec = pltpu.PrefetchScalarGridSpec(
        num_scalar_prefetch=3,
        grid=(nsteps,),
        in_specs=[
            pl.BlockSpec(memory_space=pl.ANY),
            pl.BlockSpec((td * tr, LANES), lambda i, *_: (i, 0)),
        ],
        out_specs=pl.BlockSpec(memory_space=pl.ANY),
        scratch_shapes=[
            pltpu.SMEM((2 * td,), jnp.int32),
            pltpu.VMEM((tr, LANES), jnp.uint32),
            pltpu.SemaphoreType.DMA,
            pltpu.SemaphoreType.DMA,
        ],
    )
    return pl.pallas_call(
        functools.partial(_dispatch_kernel, td=td, tr=tr, nsteps=nsteps),
        out_shape=jax.ShapeDtypeStruct((p_rows * tr, LANES), jnp.uint32),
        grid_spec=grid_spec,
        compiler_params=_params(("arbitrary",)),
        name="dispatch",
    )(counts, padded, pstart, dest_flat, hn_packed)


def _experts_kernel(blk_e_ref, nused_ref, xs_ref, wg_ref, wu_ref, wd_ref, y_ref, *, tb, tr):
    @pl.when(pl.program_id(0) >= nused_ref[0])
    def _():
        y_ref[...] = jnp.zeros_like(y_ref)

    @pl.when(pl.program_id(0) < nused_ref[0])
    def _():
        x = _unpack_tokens(lambda s: xs_ref[pl.ds(s, tb, stride=tr), :], tr)
        a = jnp.dot(x, wg_ref[...], preferred_element_type=F32)
        b = jnp.dot(x, wu_ref[...], preferred_element_type=F32)
        h = a * jax.nn.sigmoid(a) * b
        _pack_tokens(jnp.dot(h, wd_ref[...], preferred_element_type=F32), y_ref)


def _experts(block_e, nused, xs_packed, wg, wu, wd, tb):
    d, ff = wg.shape[1], wg.shape[2]
    tr = _token_rows(d)
    nblocks = xs_packed.shape[0] // (tb * tr)

    def row_map(i, blk_e, nu):
        return (jnp.minimum(i, nu[0] - 1), 0)

    def w_map(i, blk_e, nu):
        return (blk_e[i], 0, 0)

    grid_spec = pltpu.PrefetchScalarGridSpec(
        num_scalar_prefetch=2,
        grid=(nblocks,),
        in_specs=[
            pl.BlockSpec((tb * tr, LANES), row_map),
            pl.BlockSpec((None, d, ff), w_map),
            pl.BlockSpec((None, d, ff), w_map),
            pl.BlockSpec((None, ff, d), w_map),
        ],
        out_specs=pl.BlockSpec((tb * tr, LANES), lambda i, blk_e, nu: (i, 0)),
    )
    return pl.pallas_call(
        functools.partial(_experts_kernel, tb=tb, tr=tr),
        out_shape=jax.ShapeDtypeStruct(xs_packed.shape, jnp.uint32),
        grid_spec=grid_spec,
        compiler_params=_params(("arbitrary",)),
        name="experts",
    )(block_e, nused, xs_packed, wg, wu, wd)


def _final_kernel(dest_hbm, route_ref, x1_ref, p_ref, wple_ref, gple_ref, wgate_ref, gfin_ref, y_hbm,
                  o_ref, idx_a, idx_b, ybuf_a, ybuf_b, idx_sem, sem, *, tf, tr, nsteps):
    i = pl.program_id(0)
    halves = ((idx_a, ybuf_a), (idx_b, ybuf_b))

    def idx_copy(tile, h):
        return pltpu.make_async_copy(dest_hbm.at[pl.ds(tile * 2 * tf, 2 * tf)], halves[h][0], idx_sem.at[h])

    def gather(h):
        idx_ref, ybuf_ref = halves[h]

        def issue(t, carry):
            for k in range(2):
                src = y_hbm.at[pl.ds(pl.multiple_of(idx_ref[2 * t + k] * tr, tr), tr)]
                dst = ybuf_ref.at[k, pl.ds(pl.multiple_of(t * tr, tr), tr)]
                pltpu.make_async_copy(src, dst, sem.at[h]).start(priority=k)
            return carry

        lax.fori_loop(0, tf, issue, 0, unroll=True)

    def wait_rows(h):
        for k in range(2):
            pltpu.make_async_copy(y_hbm.at[pl.ds(0, tf * tr)], halves[h][1].at[k], sem.at[h]).wait()

    def compute(h):
        ybuf_ref = halves[h][1]
        rows = pl.ds(h * tf, tf)
        ple = _rms(jnp.dot(p_ref[rows, :], wple_ref[...], preferred_element_type=F32), gple_ref[...])
        wait_rows(h)
        route = route_ref[rows, :]
        w1 = route[:, R_W1:R_W1 + 1]
        w2 = route[:, R_W2:R_W2 + 1]
        y1 = _unpack_tokens(lambda s: ybuf_ref[0, pl.ds(s, tf, stride=tr), :], tr)
        y2 = _unpack_tokens(lambda s: ybuf_ref[1, pl.ds(s, tf, stride=tr), :], tr)
        x2 = x1_ref[rows, :] + w1 * y1 + w2 * y2
        gate = jax.nn.sigmoid(jnp.dot(x2, wgate_ref[...], preferred_element_type=F32))
        o_ref[rows, :] = _rms(x2 + gate * ple, gfin_ref[...])

    @pl.when(i == 0)
    def _():
        idx_copy(0, 0).start()
        idx_copy(0, 0).wait()
        gather(0)
        idx_copy(1, 1).start()

    idx_copy(2 * i + 1, 1).wait()
    gather(1)
    idx_copy(2 * i + 2, 0).start()
    compute(0)
    idx_copy(2 * i + 2, 0).wait()
    gather(0)
    idx_copy(2 * i + 3, 1).start()
    compute(1)

    @pl.when(i == nsteps - 1)
    def _():
        wait_rows(0)
        idx_copy(0, 1).wait()


def _final(dest_flat, route, x1, p2d, wple, gple, wgate, gfin, y, tf):
    n, d = x1.shape
    pd = p2d.shape[1]
    tr = _token_rows(d)
    nsteps = n // (2 * tf)
    dest_padded = jnp.concatenate([dest_flat, jnp.zeros((4 * tf,), jnp.int32)])
    const = lambda i: (0, 0)
    return pl.pallas_call(
        functools.partial(_final_kernel, tf=tf, tr=tr, nsteps=nsteps),
        out_shape=jax.ShapeDtypeStruct((n, d), F32),
        grid=(nsteps,),
        in_specs=[
            pl.BlockSpec(memory_space=pl.ANY),
            pl.BlockSpec((2 * tf, LANES), lambda i: (i, 0)),
            pl.BlockSpec((2 * tf, d), lambda i: (i, 0)),
            pl.BlockSpec((2 * tf, pd), lambda i: (i, 0)),
            pl.BlockSpec((pd, d), const, pipeline_mode=pl.Buffered(1)),
            pl.BlockSpec((1, d), const),
            pl.BlockSpec((d, d), const, pipeline_mode=pl.Buffered(1)),
            pl.BlockSpec((1, d), const),
            pl.BlockSpec(memory_space=pl.ANY),
        ],
        out_specs=pl.BlockSpec((2 * tf, d), lambda i: (i, 0)),
        scratch_shapes=[
            pltpu.SMEM((2 * tf,), jnp.int32),
            pltpu.SMEM((2 * tf,), jnp.int32),
            pltpu.VMEM((2, tf * tr, LANES), jnp.uint32),
            pltpu.VMEM((2, tf * tr, LANES), jnp.uint32),
            pltpu.SemaphoreType.DMA((2,)),
            pltpu.SemaphoreType.DMA((2,)),
        ],
        compiler_params=_params(("arbitrary",)),
        name="final",
    )(dest_padded, route, x1, p2d, wple, gple, wgate, gfin, y)


def _tile(n, want):
    t = min(n, want)
    assert n % t == 0, (n, want)
    return t


def kernel(x, p, g_mix, w_in, lb_logits, hg_norm, conv_w, sc_norm, w_out, g_ffn, w_router_group,
           w_router_expert, w_gate, w_up, w_down, w_ple, g_ple, w_ple_gate, g_final):
    b, t, d = x.shape
    n = b * t
    layer = 0
    x2d = x.reshape(n, d)

    lower_bounds = jnp.cumsum(jax.nn.softmax(lb_logits.astype(F32), axis=0), axis=0)
    lb = lower_bounds[layer].reshape(1, HG_WIDTH)

    z = _inproj(x2d, g_mix[layer].reshape(1, d), w_in[layer].astype(BF16), _tile(n, 1024), 1024)
    z3 = z.reshape(b, t, z.shape[1])
    ohg = _hgrn(z3, lb, hg_norm[layer].reshape(1, HEAD_DIM), _tile(t, 1024)).reshape(n, HG_WIDTH)
    osc = _sconv(z3, conv_w[layer], sc_norm[layer].reshape(1, SC_WIDTH), _tile(t, 512)).reshape(n, SC_WIDTH)

    wo = w_out[layer].astype(BF16)
    wr = jnp.concatenate([w_router_group[layer], w_router_expert[layer]], axis=1).astype(F32)
    wr = jnp.pad(wr, ((0, 0), (0, LANES - wr.shape[1])))
    wr_hi = wr.astype(BF16)
    wr_lo = (wr - wr_hi.astype(F32)).astype(BF16)
    x1, hn, route, cnt = _outproj(
        x2d, ohg, osc, wo[:HG_WIDTH], wo[HG_WIDTH:], g_ffn[layer].reshape(1, d),
        jnp.concatenate([wr_hi, wr_lo], axis=1), wr_hi, _tile(n, 512))

    tb = 256
    counts = cnt[0, EXPERT_LANE0:EXPERT_LANE0 + N_EXPERTS].astype(jnp.int32)
    padded = (counts + tb - 1) // tb * tb
    pend = jnp.cumsum(padded)
    pstart = pend - padded
    e_ids = route[:, R_E1:R_E2 + 1].astype(jnp.int32)
    ranks = route[:, R_RANK1:R_RANK2 + 1].astype(jnp.int32)
    expert_iota = jnp.arange(N_EXPERTS, dtype=jnp.int32)
    seg_start = jnp.sum(jnp.where(e_ids[..., None] == expert_iota, pstart.astype(jnp.int32), 0), axis=-1)
    dest_flat = (seg_start + ranks).reshape(2 * n)
    nblocks = -(-(2 * n + N_EXPERTS * (tb - 1)) // tb)
    nused = (pend[-1] // tb).astype(jnp.int32)
    blk = jnp.minimum(jnp.arange(nblocks, dtype=jnp.int32), nused - 1) * tb
    block_e = jnp.minimum(jnp.sum((blk[:, None] >= pend[None, :]).astype(jnp.int32), axis=1), N_EXPERTS - 1)

    p_rows = nblocks * tb
    zero = jnp.zeros((1,), jnp.int32)
    xs = _dispatch(jnp.concatenate([counts, zero]), jnp.concatenate([padded, p_rows - pend[-1:]]),
                   jnp.concatenate([pstart, pend[-1:]]).astype(jnp.int32), dest_flat, hn, n, _token_rows(d),
                   p_rows, _tile(n, 1024))
    y = _experts(block_e, nused.reshape(1), xs, w_gate[layer], w_up[layer], w_down[layer], tb)

    out = _final(dest_flat, route, x1, p[layer].reshape(n, -1), w_ple[layer],
                 g_ple[layer].reshape(1, d), w_ple_gate[layer], g_final.reshape(1, d), y,
                 _tile(n, 256))
    return out.reshape(b, t, d)
```

```python
import functools

import jax
import jax.numpy as jnp
from jax import lax
from jax.experimental import pallas as pl
from jax.experimental.pallas import tpu as pltpu

F32 = jnp.float32
BF16 = jnp.bfloat16
EPS = 1e-6

HEAD_DIM = 128
N_HEADS = 8
HG_WIDTH = N_HEADS * HEAD_DIM
SC_WIDTH = 1024
CHUNK = 128
N_GROUPS = 4
EXPERTS_PER_GROUP = 8
N_EXPERTS = N_GROUPS * EXPERTS_PER_GROUP
LANES = 128
EXPERT_LANE0 = N_GROUPS
NEG_BIG = -1e30
VMEM_LIMIT = 56 * 1024 * 1024


def _rms(v, gain):
    return v * lax.rsqrt(jnp.mean(v * v, axis=-1, keepdims=True) + EPS) * gain


def _token_rows(d):
    return d // (2 * LANES)


def _pack_tokens(v, out_ref):
    rows, d = v.shape
    tr = _token_rows(d)
    bits = lax.bitcast_convert_type(v.astype(BF16).astype(F32), jnp.uint32)
    words = bits[:, :d // 2] | (bits[:, d // 2:] >> 16)
    for s in range(tr):
        out_ref[pl.ds(s, rows, stride=tr), :] = words[:, s * LANES:(s + 1) * LANES]


def _unpack_tokens(load_rows, tr):
    words = [load_rows(s) for s in range(tr)]
    high = [lax.bitcast_convert_type(w & jnp.uint32(0xFFFF0000), F32) for w in words]
    low = [lax.bitcast_convert_type(w << 16, F32) for w in words]
    return jnp.concatenate(high + low, axis=1)


def _params(sem):
    return pltpu.CompilerParams(dimension_semantics=sem, vmem_limit_bytes=VMEM_LIMIT)


def _inproj_kernel(x_ref, g_ref, w_ref, z_ref, xn_ref):
    @pl.when(pl.program_id(1) == 0)
    def _():
        xn_ref[...] = _rms(x_ref[...], g_ref[...]).astype(BF16)

    z_ref[...] = jnp.dot(xn_ref[...], w_ref[...], preferred_element_type=F32).astype(z_ref.dtype)


def _inproj(x2d, g_mix, w_in_bf16, tm, tn):
    n, d = x2d.shape
    cols = w_in_bf16.shape[1]
    return pl.pallas_call(
        _inproj_kernel,
        out_shape=jax.ShapeDtypeStruct((n, cols), BF16),
        grid=(n // tm, cols // tn),
        in_specs=[
            pl.BlockSpec((tm, d), lambda i, j: (i, 0)),
            pl.BlockSpec((1, d), lambda i, j: (0, 0)),
            pl.BlockSpec((d, tn), lambda i, j: (0, j)),
        ],
        out_specs=pl.BlockSpec((tm, tn), lambda i, j: (i, j)),
        scratch_shapes=[pltpu.VMEM((tm, d), BF16)],
        compiler_params=_params(("arbitrary", "arbitrary")),
        name="inproj",
    )(x2d, g_mix, w_in_bf16)


def _hgrn_kernel(q_ref, f_ref, i_ref, g_ref, lb_ref, gain_ref, o_ref, st_ref, *, nchunks):
    @pl.when(pl.program_id(2) == 0)
    def _():
        st_ref[...] = jnp.zeros_like(st_ref)

    c_len = CHUNK
    n_levels = c_len.bit_length() - 1
    lb = lb_ref[...]
    one_minus_lb = 1.0 - lb
    gain = gain_ref[...]
    row = lax.broadcasted_iota(jnp.int32, (c_len, c_len), 0)
    col = lax.broadcasted_iota(jnp.int32, (c_len, c_len), 1)
    tri = (row >= col).astype(BF16)
    tri2 = jnp.concatenate([tri, tri], axis=1)
    differ = jnp.where(col < row, row ^ col, 0)
    sub8 = lax.broadcasted_iota(jnp.int32, (c_len // 8, 8, HEAD_DIM), 1)
    scale = HEAD_DIM ** -0.5
    nt = (((1,), (1,)), ((), ()))
    sign_bit = jnp.uint32(0x80000000)

    def neg_abs(d):
        return lax.bitcast_convert_type(lax.bitcast_convert_type(d, jnp.uint32) | sign_bit, F32)

    def boundary(a, half):
        if half >= 4:
            nb = c_len // (2 * half)
            a3 = a.reshape(nb, 2 * half, HEAD_DIM)
            return jnp.broadcast_to(a3[:, half - 1:half, :], a3.shape).reshape(c_len, HEAD_DIM)
        assert half == 2
        a3 = a.reshape(c_len // 8, 8, HEAD_DIM)
        return jnp.where(sub8 < 4, a3[:, 1:2, :], a3[:, 5:6, :]).reshape(c_len, HEAD_DIM)

    def body(c, st):
        sl = pl.ds(pl.multiple_of(c * c_len, c_len), c_len)
        fz = f_ref[sl, :].astype(F32)
        sg = jax.nn.sigmoid(fz)
        f = lb + one_minus_lb * sg
        log2_f = jnp.log2(f)
        k = (one_minus_lb * (1.0 - sg)).astype(BF16)
        h1 = log2_f.astype(BF16)
        h2 = (log2_f - h1.astype(F32)).astype(BF16)
        a = jnp.dot(tri2, jnp.concatenate([h1, h2], axis=0), preferred_element_type=F32)
        a_last = a[c_len - 1:c_len, :]
        qz = q_ref[sl, :]
        q = qz * jax.nn.sigmoid(qz) * scale
        qa = q * jnp.exp2(a).astype(BF16)
        kd = k * jnp.exp2(a_last - a).astype(BF16)
        v = i_ref[sl, :]
        diag = jnp.sum(q.astype(F32) * k.astype(F32), axis=-1, keepdims=True)
        scores = jnp.where(row == col, diag, 0.0)
        for level in range(n_levels):
            if level == 0:
                ql, kl = q * f.astype(BF16), k
            else:
                decay = jnp.exp2(neg_abs(a - boundary(a, 1 << level))).astype(BF16)
                ql, kl = q * decay, k * decay
            gram = lax.dot_general(ql, kl, nt, preferred_element_type=F32)
            scores = jnp.where(differ >= (1 << level), gram, scores)
        p = scores.astype(BF16)
        o = lax.dot_general(qa, st.astype(BF16), nt, preferred_element_type=F32)
        o = o + jnp.dot(p, v, preferred_element_type=F32)
        v_t = v.astype(F32).T.astype(BF16)
        st_new = st * jnp.exp2(a_last) + jnp.dot(v_t, kd, preferred_element_type=F32)
        gz = g_ref[sl, :]
        gate = (gz * jax.nn.sigmoid(gz)).astype(F32)
        o_ref[sl, :] = (_rms(o, gain) * gate).astype(o_ref.dtype)
        return st_new

    st_ref[...] = lax.fori_loop(0, nchunks, body, st_ref[...], unroll=True)


def _hgrn(z3, lb, gain, tc):
    b, t, _ = z3.shape
    hb = HG_WIDTH // HEAD_DIM

    def zspec(k):
        return pl.BlockSpec((None, tc, HEAD_DIM), lambda bi, h, ti, k=k: (bi, ti, h + k * hb))

    return pl.pallas_call(
        functools.partial(_hgrn_kernel, nchunks=tc // CHUNK),
        out_shape=jax.ShapeDtypeStruct((b, t, HG_WIDTH), BF16),
        grid=(b, N_HEADS, t // tc),
        in_specs=[
            zspec(0), zspec(1), zspec(2), zspec(3),
            pl.BlockSpec((1, HEAD_DIM), lambda bi, h, ti: (0, h)),
            pl.BlockSpec((1, HEAD_DIM), lambda bi, h, ti: (0, 0)),
        ],
        out_specs=pl.BlockSpec((None, tc, HEAD_DIM), lambda bi, h, ti: (bi, ti, h)),
        scratch_shapes=[pltpu.VMEM((HEAD_DIM, HEAD_DIM), F32)],
        compiler_params=_params(("arbitrary", "arbitrary", "arbitrary")),
        name="hgrn",
    )(z3, z3, z3, z3, lb, gain)


SC_HALO = 8


def _sconv_kernel(b_ref, c_ref, h_ref, w_ref, gain_ref, o_ref, u_ref, *, tc):
    @pl.when(pl.program_id(1) == 0)
    def _():
        u_ref[0:SC_HALO, :] = jnp.zeros((SC_HALO, SC_WIDTH), F32)

    u = c_ref[...].astype(F32) * h_ref[...].astype(F32)
    u_ref[SC_HALO:SC_HALO + tc, :] = u
    w = w_ref[...]
    y = (w[0:1, :] * u_ref[SC_HALO - 2:SC_HALO - 2 + tc, :]
         + w[1:2, :] * u_ref[SC_HALO - 1:SC_HALO - 1 + tc, :]
         + w[2:3, :] * u)
    o_ref[...] = _rms(b_ref[...].astype(F32) * y, gain_ref[...]).astype(o_ref.dtype)
    u_ref[0:SC_HALO, :] = u_ref[tc:tc + SC_HALO, :]


def _sconv(z3, conv_w, gain, tc):
    b, t, _ = z3.shape
    first = 4 * HG_WIDTH // SC_WIDTH

    def zspec(k):
        return pl.BlockSpec((None, tc, SC_WIDTH), lambda bi, ti, k=k: (bi, ti, first + k))

    return pl.pallas_call(
        functools.partial(_sconv_kernel, tc=tc),
        out_shape=jax.ShapeDtypeStruct((b, t, SC_WIDTH), BF16),
        grid=(b, t // tc),
        in_specs=[
            zspec(0), zspec(1), zspec(2),
            pl.BlockSpec((3, SC_WIDTH), lambda bi, ti: (0, 0)),
            pl.BlockSpec((1, SC_WIDTH), lambda bi, ti: (0, 0)),
        ],
        out_specs=pl.BlockSpec((None, tc, SC_WIDTH), lambda bi, ti: (bi, ti, 0)),
        scratch_shapes=[pltpu.VMEM((tc + SC_HALO, SC_WIDTH), F32)],
        compiler_params=_params(("arbitrary", "arbitrary")),
        name="sconv",
    )(z3, z3, z3, conv_w, gain)


R_E1, R_E2, R_W1, R_W2, R_RANK1, R_RANK2 = 0, 1, 2, 3, 4, 5


def _outproj_kernel(x_ref, ohg_ref, osc_ref, wo1_ref, wo2_ref, g_ref, wr_hl_ref, wr_h_ref,
                    x1_ref, hn_ref, route_ref, cnt_ref, carry_ref, *, tm):
    @pl.when(pl.program_id(0) == 0)
    def _():
        carry_ref[...] = jnp.zeros_like(carry_ref)

    acc = jnp.dot(ohg_ref[...], wo1_ref[...], preferred_element_type=F32)
    acc = acc + jnp.dot(osc_ref[...], wo2_ref[...], preferred_element_type=F32)
    x1 = x_ref[...] + acc
    x1_ref[...] = x1
    hn = _rms(x1, g_ref[...])
    _pack_tokens(hn, hn_ref)

    hi = hn.astype(BF16)
    lo = (hn - hi.astype(F32)).astype(BF16)
    both = jnp.dot(hi, wr_hl_ref[...], preferred_element_type=F32)
    logits = both[:, :LANES] + both[:, LANES:] + jnp.dot(lo, wr_h_ref[...], preferred_element_type=F32)

    lane = lax.broadcasted_iota(jnp.int32, (tm, LANES), 1)

    def first_argmax(vals, vmax):
        return jnp.min(jnp.where(vals == vmax, lane, LANES), axis=-1, keepdims=True)

    is_group = lane < N_GROUPS
    gl = jnp.where(is_group, logits, NEG_BIG)
    gmax = jnp.max(gl, axis=-1, keepdims=True)
    gidx = first_argmax(gl, gmax)
    g_p = 1.0 / jnp.sum(jnp.where(is_group, jnp.exp(gl - gmax), 0.0), axis=-1, keepdims=True)

    e_lane = lane - EXPERT_LANE0
    in_group = (e_lane >= 0) & (e_lane < N_EXPERTS) & ((e_lane // EXPERTS_PER_GROUP) == gidx)
    el = jnp.where(in_group, logits, NEG_BIG)
    m1 = jnp.max(el, axis=-1, keepdims=True)
    i1 = first_argmax(el, m1)
    el2 = jnp.where(lane == i1, NEG_BIG, el)
    m2 = jnp.max(el2, axis=-1, keepdims=True)
    i2 = first_argmax(el2, m2)
    r = jnp.exp(m2 - m1)
    w1 = g_p / (1.0 + r)
    w2 = g_p * r / (1.0 + r)

    oh1 = lane == i1
    oh2 = lane == i2
    onehot = (oh1 | oh2).astype(BF16)
    trow = lax.broadcasted_iota(jnp.int32, (tm, tm), 0)
    tcol = lax.broadcasted_iota(jnp.int32, (tm, tm), 1)
    before = jnp.dot((trow > tcol).astype(BF16), onehot, preferred_element_type=F32) + carry_ref[...]
    rank1 = jnp.sum(jnp.where(oh1, before, 0.0), axis=-1, keepdims=True)
    rank2 = jnp.sum(jnp.where(oh2, before, 0.0), axis=-1, keepdims=True)
    carry_ref[...] = carry_ref[...] + jnp.sum(onehot.astype(F32), axis=0, keepdims=True)
    cnt_ref[...] = carry_ref[...]

    table = jnp.zeros((tm, LANES), F32)
    for slot, val in ((R_E1, (i1 - EXPERT_LANE0).astype(F32)), (R_E2, (i2 - EXPERT_LANE0).astype(F32)),
                      (R_W1, w1), (R_W2, w2), (R_RANK1, rank1), (R_RANK2, rank2)):
        table = jnp.where(lane == slot, val, table)
    route_ref[...] = table


def _outproj(x2d, ohg, osc, wo1, wo2, g_ffn, wr_hl, wr_h, tm):
    n, d = x2d.shape
    const = lambda i: (0, 0)
    return pl.pallas_call(
        functools.partial(_outproj_kernel, tm=tm),
        out_shape=(
            jax.ShapeDtypeStruct((n, d), F32),
            jax.ShapeDtypeStruct((n * _token_rows(d), LANES), jnp.uint32),
            jax.ShapeDtypeStruct((n, LANES), F32),
            jax.ShapeDtypeStruct((1, LANES), F32),
        ),
        grid=(n // tm,),
        in_specs=[
            pl.BlockSpec((tm, d), lambda i: (i, 0)),
            pl.BlockSpec((tm, HG_WIDTH), lambda i: (i, 0)),
            pl.BlockSpec((tm, SC_WIDTH), lambda i: (i, 0)),
            pl.BlockSpec((HG_WIDTH, d), const),
            pl.BlockSpec((SC_WIDTH, d), const),
            pl.BlockSpec((1, d), const),
            pl.BlockSpec((d, 2 * LANES), const),
            pl.BlockSpec((d, LANES), const),
        ],
        out_specs=(
            pl.BlockSpec((tm, d), lambda i: (i, 0)),
            pl.BlockSpec((tm * _token_rows(d), LANES), lambda i: (i, 0)),
            pl.BlockSpec((tm, LANES), lambda i: (i, 0)),
            pl.BlockSpec((1, LANES), const),
        ),
        scratch_shapes=[pltpu.VMEM((1, LANES), F32)],
        compiler_params=_params(("arbitrary",)),
        name="outproj",
    )(x2d, ohg, osc, wo1, wo2, g_ffn, wr_hl, wr_h)


def _dispatch_kernel(cnt_ref, pad_ref, start_ref, dest_hbm, hn_ref, xs_hbm, idx_ref, zero_ref, idx_sem, sem,
                     *, td, tr, nsteps):
    i = pl.program_id(0)
    idx_copy = pltpu.make_async_copy(dest_hbm.at[pl.ds(i * 2 * td, 2 * td)], idx_ref, idx_sem)
    idx_copy.start()
    idx_copy.wait()

    def token_copy(t, slot):
        src = hn_ref.at[pl.ds(pl.multiple_of(t * tr, tr), tr)]
        return pltpu.make_async_copy(src, xs_hbm.at[pl.ds(pl.multiple_of(slot * tr, tr), tr)], sem)

    def issue(t, carry):
        token_copy(t, idx_ref[2 * t]).start(priority=0)
        token_copy(t, idx_ref[2 * t + 1]).start(priority=1)
        return carry

    lax.fori_loop(0, td, issue, 0, unroll=8)
    tile_copy = pltpu.make_async_copy(hn_ref, xs_hbm.at[pl.ds(0, td * tr)], sem)
    tile_copy.wait()
    tile_copy.wait()

    @pl.when(i == nsteps - 1)
    def _():
        zero_ref[...] = jnp.zeros_like(zero_ref)

        def pad_copy(slot):
            return pltpu.make_async_copy(zero_ref, xs_hbm.at[pl.ds(pl.multiple_of(slot * tr, tr), tr)], sem)

        def per_expert(e, carry):
            base = start_ref[e]

            def issue_pad(s, c2):
                pad_copy(base + s).start()
                return c2

            def drain_pad(s, c2):
                pad_copy(0).wait()
                return c2

            lax.fori_loop(cnt_ref[e], pad_ref[e], issue_pad, 0)
            lax.fori_loop(cnt_ref[e], pad_ref[e], drain_pad, 0)
            return carry

        lax.fori_loop(0, N_EXPERTS + 1, per_expert, 0)


def _dispatch(counts, padded, pstart, dest_flat, hn_packed, n, tr, p_rows, td):
    nsteps = n // td
    grid_spec = pltpu.PrefetchScalarGridSpec(
        num_scalar_prefetch=3,
        grid=(nsteps,),
        in_specs=[
            pl.BlockSpec(memory_space=pl.ANY),
            pl.BlockSpec((td * tr, LANES), lambda i, *_: (i, 0)),
        ],
        out_specs=pl.BlockSpec(memory_space=pl.ANY),
        scratch_shapes=[
            pltpu.SMEM((2 * td,), jnp.int32),
            pltpu.VMEM((tr, LANES), jnp.uint32),
            pltpu.SemaphoreType.DMA,
            pltpu.SemaphoreType.DMA,
        ],
    )
    return pl.pallas_call(
        functools.partial(_dispatch_kernel, td=td, tr=tr, nsteps=nsteps),
        out_shape=jax.ShapeDtypeStruct((p_rows * tr, LANES), jnp.uint32),
        grid_spec=grid_spec,
        compiler_params=_params(("arbitrary",)),
        name="dispatch",
    )(counts, padded, pstart, dest_flat, hn_packed)


def _experts_kernel(blk_e_ref, first_ref, slot_ref, next_ref, nused_ref, xs_ref, wg_hbm, wu_hbm, wd_hbm, y_ref,
                    wg_buf, wu_buf, wd_buf, sem, *, tb, tr):
    i = pl.program_id(0)

    def fetch(e, slot):
        pairs = ((wg_hbm, wg_buf), (wu_hbm, wu_buf), (wd_hbm, wd_buf))
        return [pltpu.make_async_copy(src.at[e], buf.at[slot], sem.at[slot, j]) for j, (src, buf) in enumerate(pairs)]

    @pl.when(i >= nused_ref[0])
    def _():
        y_ref[...] = jnp.zeros_like(y_ref)

    @pl.when(i < nused_ref[0])
    def _():
        slot = slot_ref[i]

        @pl.when(first_ref[i] == 1)
        def _():
            @pl.when(i == 0)
            def _():
                for copy in fetch(blk_e_ref[0], slot):
                    copy.start()

            for copy in fetch(blk_e_ref[i], slot):
                copy.wait()

            @pl.when(next_ref[i] >= 0)
            def _():
                for copy in fetch(next_ref[i], 1 - slot):
                    copy.start()

        x = _unpack_tokens(lambda s: xs_ref[pl.ds(s, tb, stride=tr), :], tr)
        a = jnp.dot(x, wg_buf[slot], preferred_element_type=F32)
        b = jnp.dot(x, wu_buf[slot], preferred_element_type=F32)
        h = a * jax.nn.sigmoid(a) * b
        _pack_tokens(jnp.dot(h, wd_buf[slot], preferred_element_type=F32), y_ref)


def _experts(block_e, counts, nused, xs_packed, wg, wu, wd, tb):
    d, ff = wg.shape[1], wg.shape[2]
    tr = _token_rows(d)
    nblocks = xs_packed.shape[0] // (tb * tr)

    blk = jnp.arange(nblocks, dtype=jnp.int32)
    first = ((blk == 0) | (block_e != jnp.roll(block_e, 1))) & (blk < nused[0])
    slot = (jnp.cumsum(first.astype(jnp.int32)) - 1) % 2
    e_iota = jnp.arange(N_EXPERTS, dtype=jnp.int32)
    later_used = (e_iota[None, :] > e_iota[:, None]) & (counts > 0)[None, :]
    next_used = jnp.min(jnp.where(later_used, e_iota[None, :], N_EXPERTS), axis=1)
    next_used = jnp.where(next_used == N_EXPERTS, -1, next_used)
    next_e = jnp.sum(jnp.where(block_e[:, None] == e_iota[None, :], next_used[None, :], 0), axis=1)

    def row_map(i, *prefetch):
        return (jnp.minimum(i, prefetch[-1][0] - 1), 0)

    grid_spec = pltpu.PrefetchScalarGridSpec(
        num_scalar_prefetch=5,
        grid=(nblocks,),
        in_specs=[
            pl.BlockSpec((tb * tr, LANES), row_map),
            pl.BlockSpec(memory_space=pl.ANY),
            pl.BlockSpec(memory_space=pl.ANY),
            pl.BlockSpec(memory_space=pl.ANY),
        ],
        out_specs=pl.BlockSpec((tb * tr, LANES), lambda i, *prefetch: (i, 0)),
        scratch_shapes=[
            pltpu.VMEM((2, d, ff), F32),
            pltpu.VMEM((2, d, ff), F32),
            pltpu.VMEM((2, ff, d), F32),
            pltpu.SemaphoreType.DMA((2, 3)),
        ],
    )
    return pl.pallas_call(
        functools.partial(_experts_kernel, tb=tb, tr=tr),
        out_shape=jax.ShapeDtypeStruct(xs_packed.shape, jnp.uint32),
        grid_spec=grid_spec,
        compiler_params=_params(("arbitrary",)),
        name="experts",
    )(block_e, first.astype(jnp.int32), slot.astype(jnp.int32), next_e.astype(jnp.int32), nused,
      xs_packed, wg, wu, wd)


def _final_kernel(dest_hbm, route_ref, x1_ref, p_ref, wple_ref, gple_ref, wgate_ref, gfin_ref, y_hbm,
                  o_ref, idx_a, idx_b, ybuf_a, ybuf_b, idx_sem, sem, *, tf, tr, nsteps):
    i = pl.program_id(0)
    halves = ((idx_a, ybuf_a), (idx_b, ybuf_b))

    def idx_copy(tile, h):
        return pltpu.make_async_copy(dest_hbm.at[pl.ds(tile * 2 * tf, 2 * tf)], halves[h][0], idx_sem.at[h])

    def gather(h):
        idx_ref, ybuf_ref = halves[h]

        def issue(t, carry):
            for k in range(2):
                src = y_hbm.at[pl.ds(pl.multiple_of(idx_ref[2 * t + k] * tr, tr), tr)]
                dst = ybuf_ref.at[k, pl.ds(pl.multiple_of(t * tr, tr), tr)]
                pltpu.make_async_copy(src, dst, sem.at[h]).start(priority=k)
            return carry

        lax.fori_loop(0, tf, issue, 0, unroll=True)

    def wait_rows(h):
        for k in range(2):
            pltpu.make_async_copy(y_hbm.at[pl.ds(0, tf * tr)], halves[h][1].at[k], sem.at[h]).wait()

    def compute(h):
        ybuf_ref = halves[h][1]
        rows = pl.ds(h * tf, tf)
        ple = _rms(jnp.dot(p_ref[rows, :], wple_ref[...], preferred_element_type=F32), gple_ref[...])
        wait_rows(h)
        route = route_ref[rows, :]
        w1 = route[:, R_W1:R_W1 + 1]
        w2 = route[:, R_W2:R_W2 + 1]
        y1 = _unpack_tokens(lambda s: ybuf_ref[0, pl.ds(s, tf, stride=tr), :], tr)
        y2 = _unpack_tokens(lambda s: ybuf_ref[1, pl.ds(s, tf, stride=tr), :], tr)
        x2 = x1_ref[rows, :] + w1 * y1 + w2 * y2
        gate = jax.nn.sigmoid(jnp.dot(x2, wgate_ref[...], preferred_element_type=F32))
        o_ref[rows, :] = _rms(x2 + gate * ple, gfin_ref[...])

    @pl.when(i == 0)
    def _():
        idx_copy(0, 0).start()
        idx_copy(0, 0).wait()
        gather(0)
        idx_copy(1, 1).start()

    idx_copy(2 * i + 1, 1).wait()
    gather(1)
    idx_copy(2 * i + 2, 0).start()
    compute(0)
    idx_copy(2 * i + 2, 0).wait()
    gather(0)
    idx_copy(2 * i + 3, 1).start()
    compute(1)

    @pl.when(i == nsteps - 1)
    def _():
        wait_rows(0)
        idx_copy(0, 1).wait()


def _final(dest_flat, route, x1, p2d, wple, gple, wgate, gfin, y, tf):
    n, d = x1.shape
    pd = p2d.shape[1]
    tr = _token_rows(d)
    nsteps = n // (2 * tf)
    dest_padded = jnp.concatenate([dest_flat, jnp.zeros((4 * tf,), jnp.int32)])
    const = lambda i: (0, 0)
    return pl.pallas_call(
        functools.partial(_final_kernel, tf=tf, tr=tr, nsteps=nsteps),
        out_shape=jax.ShapeDtypeStruct((n, d), F32),
        grid=(nsteps,),
        in_specs=[
            pl.BlockSpec(memory_space=pl.ANY),
            pl.BlockSpec((2 * tf, LANES), lambda i: (i, 0)),
            pl.BlockSpec((2 * tf, d), lambda i: (i, 0)),
            pl.BlockSpec((2 * tf, pd), lambda i: (i, 0)),
            pl.BlockSpec((pd, d), const, pipeline_mode=pl.Buffered(1)),
            pl.BlockSpec((1, d), const),
            pl.BlockSpec((d, d), const, pipeline_mode=pl.Buffered(1)),
            pl.BlockSpec((1, d), const),
            pl.BlockSpec(memory_space=pl.ANY),
        ],
        out_specs=pl.BlockSpec((2 * tf, d), lambda i: (i, 0)),
        scratch_shapes=[
            pltpu.SMEM((2 * tf,), jnp.int32),
            pltpu.SMEM((2 * tf,), jnp.int32),
            pltpu.VMEM((2, tf * tr, LANES), jnp.uint32),
            pltpu.VMEM((2, tf * tr, LANES), jnp.uint32),
            pltpu.SemaphoreType.DMA((2,)),
            pltpu.SemaphoreType.DMA((2,)),
        ],
        compiler_params=_params(("arbitrary",)),
        name="final",
    )(dest_padded, route, x1, p2d, wple, gple, wgate, gfin, y)


def _tile(n, want):
    t = min(n, want)
    assert n % t == 0, (n, want)
    return t


def kernel(x, p, g_mix, w_in, lb_logits, hg_norm, conv_w, sc_norm, w_out, g_ffn, w_router_group,
           w_router_expert, w_gate, w_up, w_down, w_ple, g_ple, w_ple_gate, g_final):
    b, t, d = x.shape
    n = b * t
    layer = 0
    x2d = x.reshape(n, d)

    lower_bounds = jnp.cumsum(jax.nn.softmax(lb_logits.astype(F32), axis=0), axis=0)
    lb = lower_bounds[layer].reshape(1, HG_WIDTH)

    z = _inproj(x2d, g_mix[layer].reshape(1, d), w_in[layer].astype(BF16), _tile(n, 1024), 1792)
    z3 = z.reshape(b, t, z.shape[1])
    ohg = _hgrn(z3, lb, hg_norm[layer].reshape(1, HEAD_DIM), _tile(t, 1024)).reshape(n, HG_WIDTH)
    osc = _sconv(z3, conv_w[layer], sc_norm[layer].reshape(1, SC_WIDTH), _tile(t, 512)).reshape(n, SC_WIDTH)

    wo = w_out[layer].astype(BF16)
    wr = jnp.concatenate([w_router_group[layer], w_router_expert[layer]], axis=1).astype(F32)
    wr = jnp.pad(wr, ((0, 0), (0, LANES - wr.shape[1])))
    wr_hi = wr.astype(BF16)
    wr_lo = (wr - wr_hi.astype(F32)).astype(BF16)
    x1, hn, route, cnt = _outproj(
        x2d, ohg, osc, wo[:HG_WIDTH], wo[HG_WIDTH:], g_ffn[layer].reshape(1, d),
        jnp.concatenate([wr_hi, wr_lo], axis=1), wr_hi, _tile(n, 512))

    tb = 256
    counts = cnt[0, EXPERT_LANE0:EXPERT_LANE0 + N_EXPERTS].astype(jnp.int32)
    padded = (counts + tb - 1) // tb * tb
    pend = jnp.cumsum(padded)
    pstart = pend - padded
    e_ids = route[:, R_E1:R_E2 + 1].astype(jnp.int32)
    ranks = route[:, R_RANK1:R_RANK2 + 1].astype(jnp.int32)
    expert_iota = jnp.arange(N_EXPERTS, dtype=jnp.int32)
    seg_start = jnp.sum(jnp.where(e_ids[..., None] == expert_iota, pstart.astype(jnp.int32), 0), axis=-1)
    dest_flat = (seg_start + ranks).reshape(2 * n)
    nblocks = -(-(2 * n + N_EXPERTS * (tb - 1)) // tb)
    nused = (pend[-1] // tb).astype(jnp.int32)
    blk = jnp.minimum(jnp.arange(nblocks, dtype=jnp.int32), nused - 1) * tb
    block_e = jnp.minimum(jnp.sum((blk[:, None] >= pend[None, :]).astype(jnp.int32), axis=1), N_EXPERTS - 1)

    p_rows = nblocks * tb
    zero = jnp.zeros((1,), jnp.int32)
    xs = _dispatch(jnp.concatenate([counts, zero]), jnp.concatenate([padded, p_rows - pend[-1:]]),
                   jnp.concatenate([pstart, pend[-1:]]).astype(jnp.int32), dest_flat, hn, n, _token_rows(d),
                   p_rows, _tile(n, 4096))
    y = _experts(block_e, counts, nused.reshape(1), xs, w_gate[layer], w_up[layer], w_down[layer], tb)

    out = _final(dest_flat, route, x1, p[layer].reshape(n, -1), w_ple[layer],
                 g_ple[layer].reshape(1, d), w_ple_gate[layer], g_final.reshape(1, d), y,
                 _tile(n, 256))
    return out.reshape(b, t, d)
```

```python
import functools

import jax
import jax.numpy as jnp
from jax import lax
from jax.experimental import pallas as pl
from jax.experimental.pallas import tpu as pltpu

F32 = jnp.float32
BF16 = jnp.bfloat16
EPS = 1e-6

HEAD_DIM = 128
N_HEADS = 8
HG_WIDTH = N_HEADS * HEAD_DIM
SC_WIDTH = 1024
CHUNK = 128
N_GROUPS = 4
EXPERTS_PER_GROUP = 8
N_EXPERTS = N_GROUPS * EXPERTS_PER_GROUP
LANES = 128
EXPERT_LANE0 = N_GROUPS
NEG_BIG = -1e30
VMEM_LIMIT = 56 * 1024 * 1024


def _rms(v, gain):
    return v * lax.rsqrt(jnp.mean(v * v, axis=-1, keepdims=True) + EPS) * gain


def _token_rows(d):
    return d // (2 * LANES)


def _pack_tokens(v, out_ref):
    rows, d = v.shape
    tr = _token_rows(d)
    bits = lax.bitcast_convert_type(v.astype(BF16).astype(F32), jnp.uint32)
    words = bits[:, :d // 2] | (bits[:, d // 2:] >> 16)
    for s in range(tr):
        out_ref[pl.ds(s, rows, stride=tr), :] = words[:, s * LANES:(s + 1) * LANES]


def _unpack_tokens(load_rows, tr):
    words = [load_rows(s) for s in range(tr)]
    high = [lax.bitcast_convert_type(w & jnp.uint32(0xFFFF0000), F32) for w in words]
    low = [lax.bitcast_convert_type(w << 16, F32) for w in words]
    return jnp.concatenate(high + low, axis=1)


def _params(sem):
    return pltpu.CompilerParams(dimension_semantics=sem, vmem_limit_bytes=VMEM_LIMIT)


def _inproj_kernel(x_ref, g_ref, w_ref, z_ref, xn_ref):
    @pl.when(pl.program_id(1) == 0)
    def _():
        xn_ref[...] = _rms(x_ref[...], g_ref[...]).astype(BF16)

    z_ref[...] = jnp.dot(xn_ref[...], w_ref[...], preferred_element_type=F32).astype(z_ref.dtype)


def _inproj(x2d, g_mix, w_in_bf16, tm, tn):
    n, d = x2d.shape
    cols = w_in_bf16.shape[1]
    return pl.pallas_call(
        _inproj_kernel,
        out_shape=jax.ShapeDtypeStruct((n, cols), BF16),
        grid=(n // tm, cols // tn),
        in_specs=[
            pl.BlockSpec((tm, d), lambda i, j: (i, 0)),
            pl.BlockSpec((1, d), lambda i, j: (0, 0)),
            pl.BlockSpec((d, tn), lambda i, j: (0, j)),
        ],
        out_specs=pl.BlockSpec((tm, tn), lambda i, j: (i, j)),
        scratch_shapes=[pltpu.VMEM((tm, d), BF16)],
        compiler_params=_params(("arbitrary", "arbitrary")),
        name="inproj",
    )(x2d, g_mix, w_in_bf16)


def _hgrn_kernel(q_ref, f_ref, i_ref, g_ref, lb_ref, gain_ref, o_ref, st_ref, *, nchunks):
    @pl.when(pl.program_id(2) == 0)
    def _():
        st_ref[...] = jnp.zeros_like(st_ref)

    c_len = CHUNK
    n_levels = c_len.bit_length() - 1
    lb = lb_ref[...]
    one_minus_lb = 1.0 - lb
    gain = gain_ref[...]
    row = lax.broadcasted_iota(jnp.int32, (c_len, c_len), 0)
    col = lax.broadcasted_iota(jnp.int32, (c_len, c_len), 1)
    tri = (row >= col).astype(BF16)
    tri2 = jnp.concatenate([tri, tri], axis=1)
    differ = jnp.where(col < row, row ^ col, 0)
    sub8 = lax.broadcasted_iota(jnp.int32, (c_len // 8, 8, HEAD_DIM), 1)
    scale = HEAD_DIM ** -0.5
    nt = (((1,), (1,)), ((), ()))
    sign_bit = jnp.uint32(0x80000000)

    def neg_abs(d):
        return lax.bitcast_convert_type(lax.bitcast_convert_type(d, jnp.uint32) | sign_bit, F32)

    def boundary(a, half):
        if half >= 4:
            nb = c_len // (2 * half)
            a3 = a.reshape(nb, 2 * half, HEAD_DIM)
            return jnp.broadcast_to(a3[:, half - 1:half, :], a3.shape).reshape(c_len, HEAD_DIM)
        assert half == 2
        a3 = a.reshape(c_len // 8, 8, HEAD_DIM)
        return jnp.where(sub8 < 4, a3[:, 1:2, :], a3[:, 5:6, :]).reshape(c_len, HEAD_DIM)

    def body(c, st):
        sl = pl.ds(pl.multiple_of(c * c_len, c_len), c_len)
        fz = f_ref[sl, :].astype(F32)
        sg = jax.nn.sigmoid(fz)
        f = lb + one_minus_lb * sg
        log2_f = jnp.log2(f)
        k = (one_minus_lb * (1.0 - sg)).astype(BF16)
        h1 = log2_f.astype(BF16)
        h2 = (log2_f - h1.astype(F32)).astype(BF16)
        a = jnp.dot(tri2, jnp.concatenate([h1, h2], axis=0), preferred_element_type=F32)
        a_last = a[c_len - 1:c_len, :]
        qz = q_ref[sl, :]
        q = qz * jax.nn.sigmoid(qz) * scale
        qa = q * jnp.exp2(a).astype(BF16)
        kd = k * jnp.exp2(a_last - a).astype(BF16)
        v = i_ref[sl, :]
        diag = jnp.sum(q.astype(F32) * k.astype(F32), axis=-1, keepdims=True)
        scores = jnp.where(row == col, diag, 0.0)
        for level in range(n_levels):
            if level == 0:
                ql, kl = q * f.astype(BF16), k
            else:
                decay = jnp.exp2(neg_abs(a - boundary(a, 1 << level))).astype(BF16)
                ql, kl = q * decay, k * decay
            gram = lax.dot_general(ql, kl, nt, preferred_element_type=F32)
            scores = jnp.where(differ >= (1 << level), gram, scores)
        p = scores.astype(BF16)
        o = lax.dot_general(qa, st.astype(BF16), nt, preferred_element_type=F32)
        o = o + jnp.dot(p, v, preferred_element_type=F32)
        v_t = v.astype(F32).T.astype(BF16)
        st_new = st * jnp.exp2(a_last) + jnp.dot(v_t, kd, preferred_element_type=F32)
        gz = g_ref[sl, :]
        gate = (gz * jax.nn.sigmoid(gz)).astype(F32)
        o_ref[sl, :] = (_rms(o, gain) * gate).astype(o_ref.dtype)
        return st_new

    st_ref[...] = lax.fori_loop(0, nchunks, body, st_ref[...], unroll=True)


def _hgrn(z3, lb, gain, tc):
    b, t, _ = z3.shape
    hb = HG_WIDTH // HEAD_DIM

    def zspec(k):
        return pl.BlockSpec((None, tc, HEAD_DIM), lambda bi, h, ti, k=k: (bi, ti, h + k * hb))

    return pl.pallas_call(
        functools.partial(_hgrn_kernel, nchunks=tc // CHUNK),
        out_shape=jax.ShapeDtypeStruct((b, t, HG_WIDTH), BF16),
        grid=(b, N_HEADS, t // tc),
        in_specs=[
            zspec(0), zspec(1), zspec(2), zspec(3),
            pl.BlockSpec((1, HEAD_DIM), lambda bi, h, ti: (0, h)),
            pl.BlockSpec((1, HEAD_DIM), lambda bi, h, ti: (0, 0)),
        ],
        out_specs=pl.BlockSpec((None, tc, HEAD_DIM), lambda bi, h, ti: (bi, ti, h)),
        scratch_shapes=[pltpu.VMEM((HEAD_DIM, HEAD_DIM), F32)],
        compiler_params=_params(("arbitrary", "arbitrary", "arbitrary")),
        name="hgrn",
    )(z3, z3, z3, z3, lb, gain)


SC_HALO = 8


def _short_conv(b_ref, c_ref, h_ref, w_ref, gain_ref, u_ref, new_sequence, tc):
    @pl.when(new_sequence)
    def _():
        u_ref[0:SC_HALO, :] = jnp.zeros((SC_HALO, SC_WIDTH), F32)

    u = c_ref[...].astype(F32) * h_ref[...].astype(F32)
    u_ref[SC_HALO:SC_HALO + tc, :] = u
    w = w_ref[...]
    y = (w[0:1, :] * u_ref[SC_HALO - 2:SC_HALO - 2 + tc, :]
         + w[1:2, :] * u_ref[SC_HALO - 1:SC_HALO - 1 + tc, :]
         + w[2:3, :] * u)
    out = _rms(b_ref[...].astype(F32) * y, gain_ref[...]).astype(BF16)
    u_ref[0:SC_HALO, :] = u_ref[tc:tc + SC_HALO, :]
    return out


R_E1, R_E2, R_W1, R_W2, R_RANK1, R_RANK2 = 0, 1, 2, 3, 4, 5


def _outproj_kernel(x_ref, ohg_ref, zb_ref, zc_ref, zh_ref, convw_ref, scgain_ref, wo1_ref, wo2_ref, g_ref,
                    wr_hl_ref, wr_h_ref, x1_ref, hn_ref, route_ref, cnt_ref, carry_ref, u_ref,
                    *, tm, tiles_per_seq):
    @pl.when(pl.program_id(0) == 0)
    def _():
        carry_ref[...] = jnp.zeros_like(carry_ref)

    osc = _short_conv(zb_ref, zc_ref, zh_ref, convw_ref, scgain_ref, u_ref,
                      pl.program_id(0) % tiles_per_seq == 0, tm)
    acc = jnp.dot(ohg_ref[...], wo1_ref[...], preferred_element_type=F32)
    acc = acc + jnp.dot(osc, wo2_ref[...], preferred_element_type=F32)
    x1 = x_ref[...] + acc
    x1_ref[...] = x1
    hn = _rms(x1, g_ref[...])
    _pack_tokens(hn, hn_ref)

    hi = hn.astype(BF16)
    lo = (hn - hi.astype(F32)).astype(BF16)
    both = jnp.dot(hi, wr_hl_ref[...], preferred_element_type=F32)
    logits = both[:, :LANES] + both[:, LANES:] + jnp.dot(lo, wr_h_ref[...], preferred_element_type=F32)

    lane = lax.broadcasted_iota(jnp.int32, (tm, LANES), 1)

    def first_argmax(vals, vmax):
        return jnp.min(jnp.where(vals == vmax, lane, LANES), axis=-1, keepdims=True)

    is_group = lane < N_GROUPS
    gl = jnp.where(is_group, logits, NEG_BIG)
    gmax = jnp.max(gl, axis=-1, keepdims=True)
    gidx = first_argmax(gl, gmax)
    g_p = 1.0 / jnp.sum(jnp.where(is_group, jnp.exp(gl - gmax), 0.0), axis=-1, keepdims=True)

    e_lane = lane - EXPERT_LANE0
    in_group = (e_lane >= 0) & (e_lane < N_EXPERTS) & ((e_lane // EXPERTS_PER_GROUP) == gidx)
    el = jnp.where(in_group, logits, NEG_BIG)
    m1 = jnp.max(el, axis=-1, keepdims=True)
    i1 = first_argmax(el, m1)
    el2 = jnp.where(lane == i1, NEG_BIG, el)
    m2 = jnp.max(el2, axis=-1, keepdims=True)
    i2 = first_argmax(el2, m2)
    r = jnp.exp(m2 - m1)
    w1 = g_p / (1.0 + r)
    w2 = g_p * r / (1.0 + r)

    oh1 = lane == i1
    oh2 = lane == i2
    onehot = (oh1 | oh2).astype(BF16)
    trow = lax.broadcasted_iota(jnp.int32, (tm, tm), 0)
    tcol = lax.broadcasted_iota(jnp.int32, (tm, tm), 1)
    before = jnp.dot((trow > tcol).astype(BF16), onehot, preferred_element_type=F32) + carry_ref[...]
    rank1 = jnp.sum(jnp.where(oh1, before, 0.0), axis=-1, keepdims=True)
    rank2 = jnp.sum(jnp.where(oh2, before, 0.0), axis=-1, keepdims=True)
    carry_ref[...] = carry_ref[...] + jnp.sum(onehot.astype(F32), axis=0, keepdims=True)
    cnt_ref[...] = carry_ref[...]

    table = jnp.zeros((tm, LANES), F32)
    for slot, val in ((R_E1, (i1 - EXPERT_LANE0).astype(F32)), (R_E2, (i2 - EXPERT_LANE0).astype(F32)),
                      (R_W1, w1), (R_W2, w2), (R_RANK1, rank1), (R_RANK2, rank2)):
        table = jnp.where(lane == slot, val, table)
    route_ref[...] = table


def _outproj(x2d, ohg, z2d, conv_w, sc_gain, wo1, wo2, g_ffn, wr_hl, wr_h, tm, seq_len):
    n, d = x2d.shape
    const = lambda i: (0, 0)
    sc_first = 4 * HG_WIDTH // SC_WIDTH

    def zspec(k):
        return pl.BlockSpec((tm, SC_WIDTH), lambda i, k=k: (i, sc_first + k))

    return pl.pallas_call(
        functools.partial(_outproj_kernel, tm=tm, tiles_per_seq=seq_len // tm),
        out_shape=(
            jax.ShapeDtypeStruct((n, d), F32),
            jax.ShapeDtypeStruct((n * _token_rows(d), LANES), jnp.uint32),
            jax.ShapeDtypeStruct((n, LANES), F32),
            jax.ShapeDtypeStruct((1, LANES), F32),
        ),
        grid=(n // tm,),
        in_specs=[
            pl.BlockSpec((tm, d), lambda i: (i, 0)),
            pl.BlockSpec((tm, HG_WIDTH), lambda i: (i, 0)),
            zspec(0), zspec(1), zspec(2),
            pl.BlockSpec((3, SC_WIDTH), const),
            pl.BlockSpec((1, SC_WIDTH), const),
            pl.BlockSpec((HG_WIDTH, d), const),
            pl.BlockSpec((SC_WIDTH, d), const),
            pl.BlockSpec((1, d), const),
            pl.BlockSpec((d, 2 * LANES), const),
            pl.BlockSpec((d, LANES), const),
        ],
        out_specs=(
            pl.BlockSpec((tm, d), lambda i: (i, 0)),
            pl.BlockSpec((tm * _token_rows(d), LANES), lambda i: (i, 0)),
            pl.BlockSpec((tm, LANES), lambda i: (i, 0)),
            pl.BlockSpec((1, LANES), const),
        ),
        scratch_shapes=[pltpu.VMEM((1, LANES), F32), pltpu.VMEM((tm + SC_HALO, SC_WIDTH), F32)],
        compiler_params=_params(("arbitrary",)),
        name="outproj",
    )(x2d, ohg, z2d, z2d, z2d, conv_w, sc_gain, wo1, wo2, g_ffn, wr_hl, wr_h)


def _dispatch_kernel(cnt_ref, pad_ref, start_ref, dest_hbm, hn_ref, xs_hbm, idx_ref, zero_ref, idx_sem, sem,
                     *, td, tr, nsteps):
    i = pl.program_id(0)
    idx_copy = pltpu.make_async_copy(dest_hbm.at[pl.ds(i * 2 * td, 2 * td)], idx_ref, idx_sem)
    idx_copy.start()
    idx_copy.wait()

    def token_copy(t, slot):
        src = hn_ref.at[pl.ds(pl.multiple_of(t * tr, tr), tr)]
        return pltpu.make_async_copy(src, xs_hbm.at[pl.ds(pl.multiple_of(slot * tr, tr), tr)], sem)

    def issue(t, carry):
        token_copy(t, idx_ref[2 * t]).start(priority=0)
        token_copy(t, idx_ref[2 * t + 1]).start(priority=1)
        return carry

    lax.fori_loop(0, td, issue, 0, unroll=8)
    tile_copy = pltpu.make_async_copy(hn_ref, xs_hbm.at[pl.ds(0, td * tr)], sem)
    tile_copy.wait()
    tile_copy.wait()

    @pl.when(i == nsteps - 1)
    def _():
        zero_ref[...] = jnp.zeros_like(zero_ref)

        def pad_copy(slot):
            return pltpu.make_async_copy(zero_ref, xs_hbm.at[pl.ds(pl.multiple_of(slot * tr, tr), tr)], sem)

        def per_expert(e, carry):
            base = start_ref[e]

            def issue_pad(s, c2):
                pad_copy(base + s).start()
                return c2

            def drain_pad(s, c2):
                pad_copy(0).wait()
                return c2

            lax.fori_loop(cnt_ref[e], pad_ref[e], issue_pad, 0)
            lax.fori_loop(cnt_ref[e], pad_ref[e], drain_pad, 0)
            return carry

        lax.fori_loop(0, N_EXPERTS + 1, per_expert, 0)


def _dispatch(counts, padded, pstart, dest_flat, hn_packed, n, tr, p_rows, td):
    nsteps = n // td
    grid_spec = pltpu.PrefetchScalarGridSpec(
        num_scalar_prefetch=3,
        grid=(nsteps,),
        in_specs=[
            pl.BlockSpec(memory_space=pl.ANY),
            pl.BlockSpec((td * tr, LANES), lambda i, *_: (i, 0)),
        ],
        out_specs=pl.BlockSpec(memory_space=pl.ANY),
        scratch_shapes=[
            pltpu.SMEM((2 * td,), jnp.int32),
            pltpu.VMEM((tr, LANES), jnp.uint32),
            pltpu.SemaphoreType.DMA,
            pltpu.SemaphoreType.DMA,
        ],
    )
    return pl.pallas_call(
        functools.partial(_dispatch_kernel, td=td, tr=tr, nsteps=nsteps),
        out_shape=jax.ShapeDtypeStruct((p_rows * tr, LANES), jnp.uint32),
        grid_spec=grid_spec,
        compiler_params=_params(("arbitrary",)),
        name="dispatch",
    )(counts, padded, pstart, dest_flat, hn_packed)


def _experts_kernel(blk_e_ref, first_ref, slot_ref, next_ref, nused_ref, xs_ref, wg_hbm, wu_hbm, wd_hbm, y_ref,
                    wg_buf, wu_buf, wd_buf, sem, *, tb, tr):
    i = pl.program_id(0)

    def fetch(e, slot):
        pairs = ((wg_hbm, wg_buf), (wu_hbm, wu_buf), (wd_hbm, wd_buf))
        return [pltpu.make_async_copy(src.at[e], buf.at[slot], sem.at[slot, j]) for j, (src, buf) in enumerate(pairs)]

    @pl.when(i >= nused_ref[0])
    def _():
        y_ref[...] = jnp.zeros_like(y_ref)

    @pl.when(i < nused_ref[0])
    def _():
        slot = slot_ref[i]

        @pl.when(first_ref[i] == 1)
        def _():
            @pl.when(i == 0)
            def _():
                for copy in fetch(blk_e_ref[0], slot):
                    copy.start()

            for copy in fetch(blk_e_ref[i], slot):
                copy.wait()

            @pl.when(next_ref[i] >= 0)
            def _():
                for copy in fetch(next_ref[i], 1 - slot):
                    copy.start()

        x = _unpack_tokens(lambda s: xs_ref[pl.ds(s, tb, stride=tr), :], tr)
        a = jnp.dot(x, wg_buf[slot], preferred_element_type=F32)
        b = jnp.dot(x, wu_buf[slot], preferred_element_type=F32)
        h = a * jax.nn.sigmoid(a) * b
        _pack_tokens(jnp.dot(h, wd_buf[slot], preferred_element_type=F32), y_ref)


def _experts(block_e, counts, nused, xs_packed, wg, wu, wd, tb):
    d, ff = wg.shape[1], wg.shape[2]
    tr = _token_rows(d)
    nblocks = xs_packed.shape[0] // (tb * tr)

    blk = jnp.arange(nblocks, dtype=jnp.int32)
    first = ((blk == 0) | (block_e != jnp.roll(block_e, 1))) & (blk < nused[0])
    slot = (jnp.cumsum(first.astype(jnp.int32)) - 1) % 2
    e_iota = jnp.arange(N_EXPERTS, dtype=jnp.int32)
    later_used = (e_iota[None, :] > e_iota[:, None]) & (counts > 0)[None, :]
    next_used = jnp.min(jnp.where(later_used, e_iota[None, :], N_EXPERTS), axis=1)
    next_used = jnp.where(next_used == N_EXPERTS, -1, next_used)
    next_e = jnp.sum(jnp.where(block_e[:, None] == e_iota[None, :], next_used[None, :], 0), axis=1)

    def row_map(i, *prefetch):
        return (jnp.minimum(i, prefetch[-1][0] - 1), 0)

    grid_spec = pltpu.PrefetchScalarGridSpec(
        num_scalar_prefetch=5,
        grid=(nblocks,),
        in_specs=[
            pl.BlockSpec((tb * tr, LANES), row_map),
            pl.BlockSpec(memory_space=pl.ANY),
            pl.BlockSpec(memory_space=pl.ANY),
            pl.BlockSpec(memory_space=pl.ANY),
        ],
        out_specs=pl.BlockSpec((tb * tr, LANES), lambda i, *prefetch: (i, 0)),
        scratch_shapes=[
            pltpu.VMEM((2, d, ff), F32),
            pltpu.VMEM((2, d, ff), F32),
            pltpu.VMEM((2, ff, d), F32),
            pltpu.SemaphoreType.DMA((2, 3)),
        ],
    )
    return pl.pallas_call(
        functools.partial(_experts_kernel, tb=tb, tr=tr),
        out_shape=jax.ShapeDtypeStruct(xs_packed.shape, jnp.uint32),
        grid_spec=grid_spec,
        compiler_params=_params(("arbitrary",)),
        name="experts",
    )(block_e, first.astype(jnp.int32), slot.astype(jnp.int32), next_e.astype(jnp.int32), nused,
      xs_packed, wg, wu, wd)


def _final_kernel(dest_hbm, route_ref, x1_ref, p_ref, wple_ref, gple_ref, wgate_ref, gfin_ref, y_hbm,
                  o_ref, idx_a, idx_b, ybuf_a, ybuf_b, idx_sem, sem, *, tf, tr, nsteps):
    i = pl.program_id(0)
    halves = ((idx_a, ybuf_a), (idx_b, ybuf_b))

    def idx_copy(tile, h):
        return pltpu.make_async_copy(dest_hbm.at[pl.ds(tile * 2 * tf, 2 * tf)], halves[h][0], idx_sem.at[h])

    def gather(h):
        idx_ref, ybuf_ref = halves[h]

        def issue(t, carry):
            for k in range(2):
                src = y_hbm.at[pl.ds(pl.multiple_of(idx_ref[2 * t + k] * tr, tr), tr)]
                dst = ybuf_ref.at[k, pl.ds(pl.multiple_of(t * tr, tr), tr)]
                pltpu.make_async_copy(src, dst, sem.at[h]).start(priority=k)
            return carry

        lax.fori_loop(0, tf, issue, 0, unroll=True)

    def wait_rows(h):
        for k in range(2):
            pltpu.make_async_copy(y_hbm.at[pl.ds(0, tf * tr)], halves[h][1].at[k], sem.at[h]).wait()

    def combine(h):
        ybuf_ref = halves[h][1]
        rows = pl.ds(h * tf, tf)
        wait_rows(h)
        route = route_ref[rows, :]
        w1 = route[:, R_W1:R_W1 + 1]
        w2 = route[:, R_W2:R_W2 + 1]
        y1 = _unpack_tokens(lambda s: ybuf_ref[0, pl.ds(s, tf, stride=tr), :], tr)
        y2 = _unpack_tokens(lambda s: ybuf_ref[1, pl.ds(s, tf, stride=tr), :], tr)
        return x1_ref[rows, :] + w1 * y1 + w2 * y2

    def finish(h, x2):
        rows = pl.ds(h * tf, tf)
        ple = _rms(jnp.dot(p_ref[rows, :], wple_ref[...], preferred_element_type=F32), gple_ref[...])
        gate = jax.nn.sigmoid(jnp.dot(x2, wgate_ref[...], preferred_element_type=F32))
        o_ref[rows, :] = _rms(x2 + gate * ple, gfin_ref[...])

    @pl.when(i == 0)
    def _():
        for h in range(2):
            idx_copy(h, h).start()
            idx_copy(h, h).wait()
            gather(h)

    for h in range(2):
        idx_copy(2 * i + 2 + h, h).start()
    for h in range(2):
        x2 = combine(h)
        idx_copy(2 * i + 2 + h, h).wait()
        gather(h)
        finish(h, x2)

    @pl.when(i == nsteps - 1)
    def _():
        for h in range(2):
            wait_rows(h)


def _final(dest_flat, route, x1, p2d, wple, gple, wgate, gfin, y, tf):
    n, d = x1.shape
    pd = p2d.shape[1]
    tr = _token_rows(d)
    nsteps = n // (2 * tf)
    dest_padded = jnp.concatenate([dest_flat, jnp.zeros((4 * tf,), jnp.int32)])
    const = lambda i: (0, 0)
    return pl.pallas_call(
        functools.partial(_final_kernel, tf=tf, tr=tr, nsteps=nsteps),
        out_shape=jax.ShapeDtypeStruct((n, d), F32),
        grid=(nsteps,),
        in_specs=[
            pl.BlockSpec(memory_space=pl.ANY),
            pl.BlockSpec((2 * tf, LANES), lambda i: (i, 0)),
            pl.BlockSpec((2 * tf, d), lambda i: (i, 0)),
            pl.BlockSpec((2 * tf, pd), lambda i: (i, 0)),
            pl.BlockSpec((pd, d), const, pipeline_mode=pl.Buffered(1)),
            pl.BlockSpec((1, d), const),
            pl.BlockSpec((d, d), const, pipeline_mode=pl.Buffered(1)),
            pl.BlockSpec((1, d), const),
            pl.BlockSpec(memory_space=pl.ANY),
        ],
        out_specs=pl.BlockSpec((2 * tf, d), lambda i: (i, 0)),
        scratch_shapes=[
            pltpu.SMEM((2 * tf,), jnp.int32),
            pltpu.SMEM((2 * tf,), jnp.int32),
            pltpu.VMEM((2, tf * tr, LANES), jnp.uint32),
            pltpu.VMEM((2, tf * tr, LANES), jnp.uint32),
            pltpu.SemaphoreType.DMA((2,)),
            pltpu.SemaphoreType.DMA((2,)),
        ],
        compiler_params=_params(("arbitrary",)),
        name="final",
    )(dest_padded, route, x1, p2d, wple, gple, wgate, gfin, y)


def _tile(n, want):
    t = min(n, want)
    assert n % t == 0, (n, want)
    return t


def kernel(x, p, g_mix, w_in, lb_logits, hg_norm, conv_w, sc_norm, w_out, g_ffn, w_router_group,
           w_router_expert, w_gate, w_up, w_down, w_ple, g_ple, w_ple_gate, g_final):
    b, t, d = x.shape
    n = b * t
    layer = 0
    x2d = x.reshape(n, d)

    lower_bounds = jnp.cumsum(jax.nn.softmax(lb_logits.astype(F32), axis=0), axis=0)
    lb = lower_bounds[layer].reshape(1, HG_WIDTH)

    z = _inproj(x2d, g_mix[layer].reshape(1, d), w_in[layer].astype(BF16), _tile(n, 1024), 1792)
    z3 = z.reshape(b, t, z.shape[1])
    ohg = _hgrn(z3, lb, hg_norm[layer].reshape(1, HEAD_DIM), _tile(t, 1024)).reshape(n, HG_WIDTH)

    wo = w_out[layer].astype(BF16)
    wr = jnp.concatenate([w_router_group[layer], w_router_expert[layer]], axis=1).astype(F32)
    wr = jnp.pad(wr, ((0, 0), (0, LANES - wr.shape[1])))
    wr_hi = wr.astype(BF16)
    wr_lo = (wr - wr_hi.astype(F32)).astype(BF16)
    x1, hn, route, cnt = _outproj(
        x2d, ohg, z, conv_w[layer], sc_norm[layer].reshape(1, SC_WIDTH), wo[:HG_WIDTH], wo[HG_WIDTH:],
        g_ffn[layer].reshape(1, d), jnp.concatenate([wr_hi, wr_lo], axis=1), wr_hi, _tile(t, 512), t)

    tb = 256
    counts = cnt[0, EXPERT_LANE0:EXPERT_LANE0 + N_EXPERTS].astype(jnp.int32)
    padded = (counts + tb - 1) // tb * tb
    pend = jnp.cumsum(padded)
    pstart = pend - padded
    e_ids = route[:, R_E1:R_E2 + 1].astype(jnp.int32)
    ranks = route[:, R_RANK1:R_RANK2 + 1].astype(jnp.int32)
    expert_iota = jnp.arange(N_EXPERTS, dtype=jnp.int32)
    seg_start = jnp.sum(jnp.where(e_ids[..., None] == expert_iota, pstart.astype(jnp.int32), 0), axis=-1)
    dest_flat = (seg_start + ranks).reshape(2 * n)
    nblocks = -(-(2 * n + N_EXPERTS * (tb - 1)) // tb)
    nused = (pend[-1] // tb).astype(jnp.int32)
    blk = jnp.minimum(jnp.arange(nblocks, dtype=jnp.int32), nused - 1) * tb
    block_e = jnp.minimum(jnp.sum((blk[:, None] >= pend[None, :]).astype(jnp.int32), axis=1), N_EXPERTS - 1)

    p_rows = nblocks * tb
    zero = jnp.zeros((1,), jnp.int32)
    xs = _dispatch(jnp.concatenate([counts, zero]), jnp.concatenate([padded, p_rows - pend[-1:]]),
                   jnp.concatenate([pstart, pend[-1:]]).astype(jnp.int32), dest_flat, hn, n, _token_rows(d),
                   p_rows, _tile(n, 4096))
    y = _experts(block_e, counts, nused.reshape(1), xs, w_gate[layer], w_up[layer], w_down[layer], tb)

    out = _final(dest_flat, route, x1, p[layer].reshape(n, -1), w_ple[layer],
                 g_ple[layer].reshape(1, d), w_ple_gate[layer], g_final.reshape(1, d), y,
                 _tile(n, 256))
    return out.reshape(b, t, d)
```

```python
import functools

import jax
import jax.numpy as jnp
from jax import lax
from jax.experimental import pallas as pl
from jax.experimental.pallas import tpu as pltpu

F32 = jnp.float32
BF16 = jnp.bfloat16
EPS = 1e-6

HEAD_DIM = 128
N_HEADS = 8
HG_WIDTH = N_HEADS * HEAD_DIM
SC_WIDTH = 1024
CHUNK = 128
N_GROUPS = 4
EXPERTS_PER_GROUP = 8
N_EXPERTS = N_GROUPS * EXPERTS_PER_GROUP
LANES = 128
EXPERT_LANE0 = EXPERTS_PER_GROUP
NEG_BIG = -1e30
VMEM_LIMIT = 56 * 1024 * 1024


def _rms(v, gain):
    return v * lax.rsqrt(jnp.mean(v * v, axis=-1, keepdims=True) + EPS) * gain


def _token_rows(d):
    return d // (2 * LANES)


def _pack_tokens(v, out_ref):
    rows, d = v.shape
    tr = _token_rows(d)
    bits = lax.bitcast_convert_type(v.astype(BF16).astype(F32), jnp.uint32)
    words = bits[:, :d // 2] | (bits[:, d // 2:] >> 16)
    for s in range(tr):
        out_ref[pl.ds(s, rows, stride=tr), :] = words[:, s * LANES:(s + 1) * LANES]


def _unpack_tokens(load_rows, tr):
    words = [load_rows(s) for s in range(tr)]
    high = [lax.bitcast_convert_type(w & jnp.uint32(0xFFFF0000), F32) for w in words]
    low = [lax.bitcast_convert_type(w << 16, F32) for w in words]
    return jnp.concatenate(high + low, axis=1)


def _params(sem):
    return pltpu.CompilerParams(dimension_semantics=sem, vmem_limit_bytes=VMEM_LIMIT)


def _inproj_kernel(x_ref, g_ref, w_ref, z_ref, xn_ref):
    @pl.when(pl.program_id(1) == 0)
    def _():
        xn_ref[...] = _rms(x_ref[...], g_ref[...]).astype(BF16)

    z_ref[...] = jnp.dot(xn_ref[...], w_ref[...], preferred_element_type=F32).astype(z_ref.dtype)


def _inproj(x2d, g_mix, w_in_bf16, tm, tn):
    n, d = x2d.shape
    cols = w_in_bf16.shape[1]
    return pl.pallas_call(
        _inproj_kernel,
        out_shape=jax.ShapeDtypeStruct((n, cols), BF16),
        grid=(n // tm, cols // tn),
        in_specs=[
            pl.BlockSpec((tm, d), lambda i, j: (i, 0)),
            pl.BlockSpec((1, d), lambda i, j: (0, 0)),
            pl.BlockSpec((d, tn), lambda i, j: (0, j)),
        ],
        out_specs=pl.BlockSpec((tm, tn), lambda i, j: (i, j)),
        scratch_shapes=[pltpu.VMEM((tm, d), BF16)],
        compiler_params=_params(("arbitrary", "arbitrary")),
        name="inproj",
    )(x2d, g_mix, w_in_bf16)


def _hgrn_kernel(q_ref, f_ref, i_ref, g_ref, lb_ref, gain_ref, o_ref, st_ref, *, nchunks):
    @pl.when(pl.program_id(2) == 0)
    def _():
        st_ref[...] = jnp.zeros_like(st_ref)

    c_len = CHUNK
    n_levels = c_len.bit_length() - 1
    lb = lb_ref[...]
    one_minus_lb = 1.0 - lb
    gain = gain_ref[...]
    row = lax.broadcasted_iota(jnp.int32, (c_len, c_len), 0)
    col = lax.broadcasted_iota(jnp.int32, (c_len, c_len), 1)
    tri = (row >= col).astype(BF16)
    tri2 = jnp.concatenate([tri, tri], axis=1)
    differ = jnp.where(col < row, row ^ col, 0)
    sub8 = lax.broadcasted_iota(jnp.int32, (c_len // 8, 8, HEAD_DIM), 1)
    scale = HEAD_DIM ** -0.5
    nt = (((1,), (1,)), ((), ()))
    sign_bit = jnp.uint32(0x80000000)

    def neg_abs(d):
        return lax.bitcast_convert_type(lax.bitcast_convert_type(d, jnp.uint32) | sign_bit, F32)

    def boundary(a, half):
        if half >= 4:
            nb = c_len // (2 * half)
            a3 = a.reshape(nb, 2 * half, HEAD_DIM)
            return jnp.broadcast_to(a3[:, half - 1:half, :], a3.shape).reshape(c_len, HEAD_DIM)
        assert half == 2
        a3 = a.reshape(c_len // 8, 8, HEAD_DIM)
        return jnp.where(sub8 < 4, a3[:, 1:2, :], a3[:, 5:6, :]).reshape(c_len, HEAD_DIM)

    def body(c, st):
        sl = pl.ds(pl.multiple_of(c * c_len, c_len), c_len)
        fz = f_ref[sl, :].astype(F32)
        sg = jax.nn.sigmoid(fz)
        f = lb + one_minus_lb * sg
        log2_f = jnp.log2(f)
        k = (one_minus_lb * (1.0 - sg)).astype(BF16)
        h1 = log2_f.astype(BF16)
        h2 = (log2_f - h1.astype(F32)).astype(BF16)
        a = jnp.dot(tri2, jnp.concatenate([h1, h2], axis=0), preferred_element_type=F32)
        a_last = a[c_len - 1:c_len, :]
        qz = q_ref[sl, :]
        q = qz * jax.nn.sigmoid(qz) * scale
        qa = q * jnp.exp2(a).astype(BF16)
        kd = k * jnp.exp2(a_last - a).astype(BF16)
        v = i_ref[sl, :]
        scores = jnp.where(row == col, lax.dot_general(q, k, nt, preferred_element_type=F32), 0.0)
        for level in range(n_levels):
            if level == 0:
                ql, kl = q * f.astype(BF16), k
            else:
                decay = jnp.exp2(neg_abs(a - boundary(a, 1 << level))).astype(BF16)
                ql, kl = q * decay, k * decay
            gram = lax.dot_general(ql, kl, nt, preferred_element_type=F32)
            scores = jnp.where(differ >= (1 << level), gram, scores)
        p = scores.astype(BF16)
        o = lax.dot_general(qa, st.astype(BF16), nt, preferred_element_type=F32)
        o = o + jnp.dot(p, v, preferred_element_type=F32)
        v_t = v.astype(F32).T.astype(BF16)
        st_new = st * jnp.exp2(a_last) + jnp.dot(v_t, kd, preferred_element_type=F32)
        gz = g_ref[sl, :]
        gate = (gz * jax.nn.sigmoid(gz)).astype(F32)
        o_ref[sl, :] = (_rms(o, gain) * gate).astype(o_ref.dtype)
        return st_new

    st_ref[...] = lax.fori_loop(0, nchunks, body, st_ref[...], unroll=True)


def _hgrn(z3, lb, gain, tc):
    b, t, _ = z3.shape
    hb = HG_WIDTH // HEAD_DIM

    def zspec(k):
        return pl.BlockSpec((None, tc, HEAD_DIM), lambda bi, h, ti, k=k: (bi, ti, h + k * hb))

    return pl.pallas_call(
        functools.partial(_hgrn_kernel, nchunks=tc // CHUNK),
        out_shape=jax.ShapeDtypeStruct((b, t, HG_WIDTH), BF16),
        grid=(b, N_HEADS, t // tc),
        in_specs=[
            zspec(0), zspec(1), zspec(2), zspec(3),
            pl.BlockSpec((1, HEAD_DIM), lambda bi, h, ti: (0, h)),
            pl.BlockSpec((1, HEAD_DIM), lambda bi, h, ti: (0, 0)),
        ],
        out_specs=pl.BlockSpec((None, tc, HEAD_DIM), lambda bi, h, ti: (bi, ti, h)),
        scratch_shapes=[pltpu.VMEM((HEAD_DIM, HEAD_DIM), F32)],
        compiler_params=_params(("arbitrary", "arbitrary", "arbitrary")),
        name="hgrn",
    )(z3, z3, z3, z3, lb, gain)


SC_HALO = 8


def _short_conv(b_ref, c_ref, h_ref, w_ref, gain_ref, u_ref, new_sequence, tc):
    @pl.when(new_sequence)
    def _():
        u_ref[0:SC_HALO, :] = jnp.zeros((SC_HALO, SC_WIDTH), F32)

    u = c_ref[...].astype(F32) * h_ref[...].astype(F32)
    u_ref[SC_HALO:SC_HALO + tc, :] = u
    w = w_ref[...]
    y = (w[0:1, :] * u_ref[SC_HALO - 2:SC_HALO - 2 + tc, :]
         + w[1:2, :] * u_ref[SC_HALO - 1:SC_HALO - 1 + tc, :]
         + w[2:3, :] * u)
    out = _rms(b_ref[...].astype(F32) * y, gain_ref[...]).astype(BF16)
    u_ref[0:SC_HALO, :] = u_ref[tc:tc + SC_HALO, :]
    return out


R_E1, R_E2, R_W1, R_W2, R_RANK1, R_RANK2 = 0, 1, 2, 3, 4, 5


def _outproj_kernel(x_ref, ohg_ref, zb_ref, zc_ref, zh_ref, convw_ref, scgain_ref, wo1_ref, wo2_ref, g_ref,
                    wr_hl_ref, wr_h_ref, x1_ref, hn_ref, route_ref, cnt_ref, carry_ref, u_ref,
                    *, tm, tiles_per_seq):
    @pl.when(pl.program_id(0) == 0)
    def _():
        carry_ref[...] = jnp.zeros_like(carry_ref)

    osc = _short_conv(zb_ref, zc_ref, zh_ref, convw_ref, scgain_ref, u_ref,
                      pl.program_id(0) % tiles_per_seq == 0, tm)
    acc = jnp.dot(ohg_ref[...], wo1_ref[...], preferred_element_type=F32)
    acc = acc + jnp.dot(osc, wo2_ref[...], preferred_element_type=F32)
    x1 = x_ref[...] + acc
    x1_ref[...] = x1
    hn = _rms(x1, g_ref[...])
    _pack_tokens(hn, hn_ref)

    hi = hn.astype(BF16)
    lo = (hn - hi.astype(F32)).astype(BF16)
    both = jnp.dot(hi, wr_hl_ref[...], preferred_element_type=F32)
    logits = both[:, :LANES] + both[:, LANES:] + jnp.dot(lo, wr_h_ref[...], preferred_element_type=F32)

    lt = logits.T
    sub = lax.broadcasted_iota(jnp.int32, (EXPERTS_PER_GROUP, tm), 0)

    def first_argmax(vals, vmax):
        return jnp.min(jnp.where(vals == vmax, sub, EXPERTS_PER_GROUP), axis=0, keepdims=True)

    is_group = sub < N_GROUPS
    gl = jnp.where(is_group, lt[0:EXPERTS_PER_GROUP], NEG_BIG)
    gmax = jnp.max(gl, axis=0, keepdims=True)
    gidx = first_argmax(gl, gmax)
    g_p = 1.0 / jnp.sum(jnp.where(is_group, jnp.exp(gl - gmax), 0.0), axis=0, keepdims=True)

    el = lt[EXPERT_LANE0:EXPERT_LANE0 + EXPERTS_PER_GROUP]
    for g in range(1, N_GROUPS):
        lo_row = EXPERT_LANE0 + g * EXPERTS_PER_GROUP
        el = jnp.where(gidx == g, lt[lo_row:lo_row + EXPERTS_PER_GROUP], el)
    m1 = jnp.max(el, axis=0, keepdims=True)
    i1 = first_argmax(el, m1)
    el2 = jnp.where(sub == i1, NEG_BIG, el)
    m2 = jnp.max(el2, axis=0, keepdims=True)
    i2 = first_argmax(el2, m2)
    r = jnp.exp(m2 - m1)
    w1 = g_p / (1.0 + r)
    w2 = g_p * r / (1.0 + r)
    e1 = gidx * EXPERTS_PER_GROUP + i1
    e2 = gidx * EXPERTS_PER_GROUP + i2

    erow = lax.broadcasted_iota(jnp.int32, (N_EXPERTS, tm), 0)
    oh1 = erow == e1
    oh2 = erow == e2
    onehot = (oh1 | oh2).astype(BF16)
    trow = lax.broadcasted_iota(jnp.int32, (tm, tm), 0)
    tcol = lax.broadcasted_iota(jnp.int32, (tm, tm), 1)
    before = jnp.dot(onehot, (trow < tcol).astype(BF16), preferred_element_type=F32) + carry_ref[...]
    rank1 = jnp.sum(jnp.where(oh1, before, 0.0), axis=0, keepdims=True)
    rank2 = jnp.sum(jnp.where(oh2, before, 0.0), axis=0, keepdims=True)
    carry_ref[...] = carry_ref[...] + jnp.sum(onehot.astype(F32), axis=1, keepdims=True)
    cnt_ref[...] = carry_ref[...]

    rows = [e1.astype(F32), e2.astype(F32), w1, w2, rank1, rank2]
    table_t = jnp.concatenate(rows + [jnp.zeros((LANES - len(rows), tm), F32)], axis=0)
    route_ref[...] = table_t.T


def _outproj(x2d, ohg, z2d, conv_w, sc_gain, wo1, wo2, g_ffn, wr_hl, wr_h, tm, seq_len):
    n, d = x2d.shape
    const = lambda i: (0, 0)
    sc_first = 4 * HG_WIDTH // SC_WIDTH

    def zspec(k):
        return pl.BlockSpec((tm, SC_WIDTH), lambda i, k=k: (i, sc_first + k))

    return pl.pallas_call(
        functools.partial(_outproj_kernel, tm=tm, tiles_per_seq=seq_len // tm),
        out_shape=(
            jax.ShapeDtypeStruct((n, d), F32),
            jax.ShapeDtypeStruct((n * _token_rows(d), LANES), jnp.uint32),
            jax.ShapeDtypeStruct((n, LANES), F32),
            jax.ShapeDtypeStruct((N_EXPERTS, 1), F32),
        ),
        grid=(n // tm,),
        in_specs=[
            pl.BlockSpec((tm, d), lambda i: (i, 0)),
            pl.BlockSpec((tm, HG_WIDTH), lambda i: (i, 0)),
            zspec(0), zspec(1), zspec(2),
            pl.BlockSpec((3, SC_WIDTH), const),
            pl.BlockSpec((1, SC_WIDTH), const),
            pl.BlockSpec((HG_WIDTH, d), const),
            pl.BlockSpec((SC_WIDTH, d), const),
            pl.BlockSpec((1, d), const),
            pl.BlockSpec((d, 2 * LANES), const),
            pl.BlockSpec((d, LANES), const),
        ],
        out_specs=(
            pl.BlockSpec((tm, d), lambda i: (i, 0)),
            pl.BlockSpec((tm * _token_rows(d), LANES), lambda i: (i, 0)),
            pl.BlockSpec((tm, LANES), lambda i: (i, 0)),
            pl.BlockSpec((N_EXPERTS, 1), const),
        ),
        scratch_shapes=[pltpu.VMEM((N_EXPERTS, 1), F32), pltpu.VMEM((tm + SC_HALO, SC_WIDTH), F32)],
        compiler_params=_params(("arbitrary",)),
        name="outproj",
    )(x2d, ohg, z2d, z2d, z2d, conv_w, sc_gain, wo1, wo2, g_ffn, wr_hl, wr_h)


def _dispatch_kernel(cnt_ref, pad_ref, start_ref, dest_hbm, hn_ref, xs_hbm, idx_ref, zero_ref, idx_sem, sem,
                     *, td, tr, nsteps):
    i = pl.program_id(0)
    idx_copy = pltpu.make_async_copy(dest_hbm.at[pl.ds(i * 2 * td, 2 * td)], idx_ref, idx_sem)
    idx_copy.start()
    idx_copy.wait()

    def token_copy(t, slot):
        src = hn_ref.at[pl.ds(pl.multiple_of(t * tr, tr), tr)]
        return pltpu.make_async_copy(src, xs_hbm.at[pl.ds(pl.multiple_of(slot * tr, tr), tr)], sem)

    def issue(t, carry):
        token_copy(t, idx_ref[2 * t]).start(priority=0)
        token_copy(t, idx_ref[2 * t + 1]).start(priority=1)
        return carry

    lax.fori_loop(0, td, issue, 0, unroll=8)
    tile_copy = pltpu.make_async_copy(hn_ref, xs_hbm.at[pl.ds(0, td * tr)], sem)
    tile_copy.wait()
    tile_copy.wait()

    @pl.when(i == nsteps - 1)
    def _():
        zero_ref[...] = jnp.zeros_like(zero_ref)

        def pad_copy(slot):
            return pltpu.make_async_copy(zero_ref, xs_hbm.at[pl.ds(pl.multiple_of(slot * tr, tr), tr)], sem)

        def per_expert(e, carry):
            base = start_ref[e]

            def issue_pad(s, c2):
                pad_copy(base + s).start()
                return c2

            def drain_pad(s, c2):
                pad_copy(0).wait()
                return c2

            lax.fori_loop(cnt_ref[e], pad_ref[e], issue_pad, 0)
            lax.fori_loop(cnt_ref[e], pad_ref[e], drain_pad, 0)
            return carry

        lax.fori_loop(0, N_EXPERTS + 1, per_expert, 0)


def _dispatch(counts, padded, pstart, dest_flat, hn_packed, n, tr, p_rows, td):
    nsteps = n // td
    grid_spec = pltpu.PrefetchScalarGridSpec(
        num_scalar_prefetch=3,
        grid=(nsteps,),
        in_specs=[
            pl.BlockSpec(memory_space=pl.ANY),
            pl.BlockSpec((td * tr, LANES), lambda i, *_: (i, 0)),
        ],
        out_specs=pl.BlockSpec(memory_space=pl.ANY),
        scratch_shapes=[
            pltpu.SMEM((2 * td,), jnp.int32),
            pltpu.VMEM((tr, LANES), jnp.uint32),
            pltpu.SemaphoreType.DMA,
            pltpu.SemaphoreType.DMA,
        ],
    )
    return pl.pallas_call(
        functools.partial(_dispatch_kernel, td=td, tr=tr, nsteps=nsteps),
        out_shape=jax.ShapeDtypeStruct((p_rows * tr, LANES), jnp.uint32),
        grid_spec=grid_spec,
        compiler_params=_params(("arbitrary",)),
        name="dispatch",
    )(counts, padded, pstart, dest_flat, hn_packed)


def _experts_kernel(blk_e_ref, first_ref, slot_ref, next_ref, nused_ref, xs_ref, wg_hbm, wu_hbm, wd_hbm, y_ref,
                    wg_buf, wu_buf, wd_buf, sem, *, tb, tr):
    i = pl.program_id(0)

    def fetch(e, slot):
        pairs = ((wg_hbm, wg_buf), (wu_hbm, wu_buf), (wd_hbm, wd_buf))
        return [pltpu.make_async_copy(src.at[e], buf.at[slot], sem.at[slot, j]) for j, (src, buf) in enumerate(pairs)]

    @pl.when(i >= nused_ref[0])
    def _():
        y_ref[...] = jnp.zeros_like(y_ref)

    @pl.when(i < nused_ref[0])
    def _():
        slot = slot_ref[i]

        @pl.when(first_ref[i] == 1)
        def _():
            @pl.when(i == 0)
            def _():
                for copy in fetch(blk_e_ref[0], slot):
                    copy.start()

            for copy in fetch(blk_e_ref[i], slot):
                copy.wait()

            @pl.when(next_ref[i] >= 0)
            def _():
                for copy in fetch(next_ref[i], 1 - slot):
                    copy.start()

        x = _unpack_tokens(lambda s: xs_ref[pl.ds(s, tb, stride=tr), :], tr)
        a = jnp.dot(x, wg_buf[slot], preferred_element_type=F32)
        b = jnp.dot(x, wu_buf[slot], preferred_element_type=F32)
        h = a * jax.nn.sigmoid(a) * b
        _pack_tokens(jnp.dot(h, wd_buf[slot], preferred_element_type=F32), y_ref)


def _experts(block_e, counts, nused, xs_packed, wg, wu, wd, tb):
    d, ff = wg.shape[1], wg.shape[2]
    tr = _token_rows(d)
    nblocks = xs_packed.shape[0] // (tb * tr)

    blk = jnp.arange(nblocks, dtype=jnp.int32)
    first = ((blk == 0) | (block_e != jnp.roll(block_e, 1))) & (blk < nused[0])
    slot = (jnp.cumsum(first.astype(jnp.int32)) - 1) % 2
    e_iota = jnp.arange(N_EXPERTS, dtype=jnp.int32)
    later_used = (e_iota[None, :] > e_iota[:, None]) & (counts > 0)[None, :]
    next_used = jnp.min(jnp.where(later_used, e_iota[None, :], N_EXPERTS), axis=1)
    next_used = jnp.where(next_used == N_EXPERTS, -1, next_used)
    next_e = jnp.sum(jnp.where(block_e[:, None] == e_iota[None, :], next_used[None, :], 0), axis=1)

    def row_map(i, *prefetch):
        return (jnp.minimum(i, prefetch[-1][0] - 1), 0)

    grid_spec = pltpu.PrefetchScalarGridSpec(
        num_scalar_prefetch=5,
        grid=(nblocks,),
        in_specs=[
            pl.BlockSpec((tb * tr, LANES), row_map),
            pl.BlockSpec(memory_space=pl.ANY),
            pl.BlockSpec(memory_space=pl.ANY),
            pl.BlockSpec(memory_space=pl.ANY),
        ],
        out_specs=pl.BlockSpec((tb * tr, LANES), lambda i, *prefetch: (i, 0)),
        scratch_shapes=[
            pltpu.VMEM((2, d, ff), F32),
            pltpu.VMEM((2, d, ff), F32),
            pltpu.VMEM((2, ff, d), F32),
            pltpu.SemaphoreType.DMA((2, 3)),
        ],
    )
    return pl.pallas_call(
        functools.partial(_experts_kernel, tb=tb, tr=tr),
        out_shape=jax.ShapeDtypeStruct(xs_packed.shape, jnp.uint32),
        grid_spec=grid_spec,
        compiler_params=_params(("arbitrary",)),
        name="experts",
    )(block_e, first.astype(jnp.int32), slot.astype(jnp.int32), next_e.astype(jnp.int32), nused,
      xs_packed, wg, wu, wd)


def _final_kernel(dest_hbm, route_ref, x1_ref, p_ref, wple_ref, gple_ref, wgate_ref, gfin_ref, y_hbm,
                  o_ref, idx_a, idx_b, ybuf_a, ybuf_b, idx_sem, sem, *, tf, tr, nsteps):
    i = pl.program_id(0)
    halves = ((idx_a, ybuf_a), (idx_b, ybuf_b))

    def idx_copy(tile, h):
        return pltpu.make_async_copy(dest_hbm.at[pl.ds(tile * 2 * tf, 2 * tf)], halves[h][0], idx_sem.at[h])

    def gather(h):
        idx_ref, ybuf_ref = halves[h]

        def issue(t, carry):
            for k in range(2):
                src = y_hbm.at[pl.ds(pl.multiple_of(idx_ref[2 * t + k] * tr, tr), tr)]
                dst = ybuf_ref.at[k, pl.ds(pl.multiple_of(t * tr, tr), tr)]
                pltpu.make_async_copy(src, dst, sem.at[h]).start(priority=k)
            return carry

        lax.fori_loop(0, tf, issue, 0, unroll=True)

    def wait_rows(h):
        for k in range(2):
            pltpu.make_async_copy(y_hbm.at[pl.ds(0, tf * tr)], halves[h][1].at[k], sem.at[h]).wait()

    def combine(h):
        ybuf_ref = halves[h][1]
        rows = pl.ds(h * tf, tf)
        wait_rows(h)
        route = route_ref[rows, :]
        w1 = route[:, R_W1:R_W1 + 1]
        w2 = route[:, R_W2:R_W2 + 1]
        y1 = _unpack_tokens(lambda s: ybuf_ref[0, pl.ds(s, tf, stride=tr), :], tr)
        y2 = _unpack_tokens(lambda s: ybuf_ref[1, pl.ds(s, tf, stride=tr), :], tr)
        return x1_ref[rows, :] + w1 * y1 + w2 * y2

    def finish(h, x2):
        rows = pl.ds(h * tf, tf)
        ple = _rms(jnp.dot(p_ref[rows, :], wple_ref[...], preferred_element_type=F32), gple_ref[...])
        gate = jax.nn.sigmoid(jnp.dot(x2, wgate_ref[...], preferred_element_type=F32))
        o_ref[rows, :] = _rms(x2 + gate * ple, gfin_ref[...])

    @pl.when(i == 0)
    def _():
        for h in range(2):
            idx_copy(h, h).start()
            idx_copy(h, h).wait()
            gather(h)

    for h in range(2):
        idx_copy(2 * i + 2 + h, h).start()
    for h in range(2):
        x2 = combine(h)
        idx_copy(2 * i + 2 + h, h).wait()
        gather(h)
        finish(h, x2)

    @pl.when(i == nsteps - 1)
    def _():
        for h in range(2):
            wait_rows(h)


def _final(dest_flat, route, x1, p2d, wple, gple, wgate, gfin, y, tf):
    n, d = x1.shape
    pd = p2d.shape[1]
    tr = _token_rows(d)
    nsteps = n // (2 * tf)
    dest_padded = jnp.concatenate([dest_flat, jnp.zeros((4 * tf,), jnp.int32)])
    const = lambda i: (0, 0)
    return pl.pallas_call(
        functools.partial(_final_kernel, tf=tf, tr=tr, nsteps=nsteps),
        out_shape=jax.ShapeDtypeStruct((n, d), F32),
        grid=(nsteps,),
        in_specs=[
            pl.BlockSpec(memory_space=pl.ANY),
            pl.BlockSpec((2 * tf, LANES), lambda i: (i, 0)),
            pl.BlockSpec((2 * tf, d), lambda i: (i, 0)),
            pl.BlockSpec((2 * tf, pd), lambda i: (i, 0)),
            pl.BlockSpec((pd, d), const, pipeline_mode=pl.Buffered(1)),
            pl.BlockSpec((1, d), const),
            pl.BlockSpec((d, d), const, pipeline_mode=pl.Buffered(1)),
            pl.BlockSpec((1, d), const),
            pl.BlockSpec(memory_space=pl.ANY),
        ],
        out_specs=pl.BlockSpec((2 * tf, d), lambda i: (i, 0)),
        scratch_shapes=[
            pltpu.SMEM((2 * tf,), jnp.int32),
            pltpu.SMEM((2 * tf,), jnp.int32),
            pltpu.VMEM((2, tf * tr, LANES), jnp.uint32),
            pltpu.VMEM((2, tf * tr, LANES), jnp.uint32),
            pltpu.SemaphoreType.DMA((2,)),
            pltpu.SemaphoreType.DMA((2,)),
        ],
        compiler_params=_params(("arbitrary",)),
        name="final",
    )(dest_padded, route, x1, p2d, wple, gple, wgate, gfin, y)


def _tile(n, want):
    t = min(n, want)
    assert n % t == 0, (n, want)
    return t


def kernel(x, p, g_mix, w_in, lb_logits, hg_norm, conv_w, sc_norm, w_out, g_ffn, w_router_group,
           w_router_expert, w_gate, w_up, w_down, w_ple, g_ple, w_ple_gate, g_final):
    b, t, d = x.shape
    n = b * t
    layer = 0
    x2d = x.reshape(n, d)

    lower_bounds = jnp.cumsum(jax.nn.softmax(lb_logits.astype(F32), axis=0), axis=0)
    lb = lower_bounds[layer].reshape(1, HG_WIDTH)

    z = _inproj(x2d, g_mix[layer].reshape(1, d), w_in[layer].astype(BF16), _tile(n, 1024), 1792)
    z3 = z.reshape(b, t, z.shape[1])
    ohg = _hgrn(z3, lb, hg_norm[layer].reshape(1, HEAD_DIM), _tile(t, 1024)).reshape(n, HG_WIDTH)

    wo = w_out[layer].astype(BF16)
    wr = jnp.concatenate([jnp.pad(w_router_group[layer], ((0, 0), (0, EXPERT_LANE0 - N_GROUPS))),
                          w_router_expert[layer]], axis=1).astype(F32)
    wr = jnp.pad(wr, ((0, 0), (0, LANES - wr.shape[1])))
    wr_hi = wr.astype(BF16)
    wr_lo = (wr - wr_hi.astype(F32)).astype(BF16)
    x1, hn, route, cnt = _outproj(
        x2d, ohg, z, conv_w[layer], sc_norm[layer].reshape(1, SC_WIDTH), wo[:HG_WIDTH], wo[HG_WIDTH:],
        g_ffn[layer].reshape(1, d), jnp.concatenate([wr_hi, wr_lo], axis=1), wr_hi, _tile(t, 512), t)

    tb = 256
    counts = cnt[:, 0].astype(jnp.int32)
    padded = (counts + tb - 1) // tb * tb
    pend = jnp.cumsum(padded)
    pstart = pend - padded
    e_ids = route[:, R_E1:R_E2 + 1].astype(jnp.int32)
    ranks = route[:, R_RANK1:R_RANK2 + 1].astype(jnp.int32)
    expert_iota = jnp.arange(N_EXPERTS, dtype=jnp.int32)
    seg_start = jnp.sum(jnp.where(e_ids[..., None] == expert_iota, pstart.astype(jnp.int32), 0), axis=-1)
    dest_flat = (seg_start + ranks).reshape(2 * n)
    nblocks = -(-(2 * n + N_EXPERTS * (tb - 1)) // tb)
    nused = (pend[-1] // tb).astype(jnp.int32)
    blk = jnp.minimum(jnp.arange(nblocks, dtype=jnp.int32), nused - 1) * tb
    block_e = jnp.minimum(jnp.sum((blk[:, None] >= pend[None, :]).astype(jnp.int32), axis=1), N_EXPERTS - 1)

    p_rows = nblocks * tb
    zero = jnp.zeros((1,), jnp.int32)
    xs = _dispatch(jnp.concatenate([counts, zero]), jnp.concatenate([padded, p_rows - pend[-1:]]),
                   jnp.concatenate([pstart, pend[-1:]]).astype(jnp.int32), dest_flat, hn, n, _token_rows(d),
                   p_rows, _tile(n, 4096))
    y = _experts(block_e, counts, nused.reshape(1), xs, w_gate[layer], w_up[layer], w_down[layer], tb)

    out = _final(dest_flat, route, x1, p[layer].reshape(n, -1), w_ple[layer],
                 g_ple[layer].reshape(1, d), w_ple_gate[layer], g_final.reshape(1, d), y,
                 _tile(n, 256))
    return out.reshape(b, t, d)
```

```python
import functools

import jax
import jax.numpy as jnp
from jax import lax
from jax.experimental import pallas as pl
from jax.experimental.pallas import tpu as pltpu

F32 = jnp.float32
BF16 = jnp.bfloat16
EPS = 1e-6

HEAD_DIM = 128
N_HEADS = 8
HG_WIDTH = N_HEADS * HEAD_DIM
SC_WIDTH = 1024
CHUNK = 128
N_GROUPS = 4
EXPERTS_PER_GROUP = 8
N_EXPERTS = N_GROUPS * EXPERTS_PER_GROUP
LANES = 128
EXPERT_LANE0 = EXPERTS_PER_GROUP
NEG_BIG = -1e30
VMEM_LIMIT = 56 * 1024 * 1024


def _rms(v, gain):
    return v * lax.rsqrt(jnp.mean(v * v, axis=-1, keepdims=True) + EPS) * gain


def _token_rows(d):
    return d // (2 * LANES)


def _pack_tokens(v, out_ref):
    rows, d = v.shape
    tr = _token_rows(d)
    bits = lax.bitcast_convert_type(v.astype(BF16).astype(F32), jnp.uint32)
    words = bits[:, :d // 2] | (bits[:, d // 2:] >> 16)
    for s in range(tr):
        out_ref[pl.ds(s, rows, stride=tr), :] = words[:, s * LANES:(s + 1) * LANES]


def _unpack_tokens(load_rows, tr):
    words = [load_rows(s) for s in range(tr)]
    high = [lax.bitcast_convert_type(w & jnp.uint32(0xFFFF0000), F32) for w in words]
    low = [lax.bitcast_convert_type(w << 16, F32) for w in words]
    return jnp.concatenate(high + low, axis=1)


def _params(sem):
    return pltpu.CompilerParams(dimension_semantics=sem, vmem_limit_bytes=VMEM_LIMIT)


def _inproj_kernel(x_ref, g_ref, w_ref, z_ref, xn_ref):
    @pl.when(pl.program_id(1) == 0)
    def _():
        xn_ref[...] = _rms(x_ref[...], g_ref[...]).astype(BF16)

    z_ref[...] = jnp.dot(xn_ref[...], w_ref[...], preferred_element_type=F32).astype(z_ref.dtype)


def _inproj(x2d, g_mix, w_in_bf16, tm, tn):
    n, d = x2d.shape
    cols = w_in_bf16.shape[1]
    return pl.pallas_call(
        _inproj_kernel,
        out_shape=jax.ShapeDtypeStruct((n, cols), BF16),
        grid=(n // tm, cols // tn),
        in_specs=[
            pl.BlockSpec((tm, d), lambda i, j: (i, 0)),
            pl.BlockSpec((1, d), lambda i, j: (0, 0)),
            pl.BlockSpec((d, tn), lambda i, j: (0, j)),
        ],
        out_specs=pl.BlockSpec((tm, tn), lambda i, j: (i, j)),
        scratch_shapes=[pltpu.VMEM((tm, d), BF16)],
        compiler_params=_params(("arbitrary", "arbitrary")),
        name="inproj",
    )(x2d, g_mix, w_in_bf16)


def _hgrn_kernel(q_ref, f_ref, i_ref, g_ref, lb_ref, gain_ref, o_ref, st_ref, *, nchunks):
    @pl.when(pl.program_id(2) == 0)
    def _():
        st_ref[...] = jnp.zeros_like(st_ref)

    c_len = CHUNK
    n_levels = c_len.bit_length() - 1
    lb = lb_ref[...]
    one_minus_lb = 1.0 - lb
    gain = gain_ref[...]
    row = lax.broadcasted_iota(jnp.int32, (c_len, c_len), 0)
    col = lax.broadcasted_iota(jnp.int32, (c_len, c_len), 1)
    tri = (row >= col).astype(BF16)
    tri2 = jnp.concatenate([tri, tri], axis=1)
    differ = jnp.where(col < row, row ^ col, 0)
    sub8 = lax.broadcasted_iota(jnp.int32, (c_len // 8, 8, HEAD_DIM), 1)
    scale = HEAD_DIM ** -0.5
    nt = (((1,), (1,)), ((), ()))
    sign_bit = jnp.uint32(0x80000000)

    def neg_abs(d):
        return lax.bitcast_convert_type(lax.bitcast_convert_type(d, jnp.uint32) | sign_bit, F32)

    def boundary(a, half):
        if half >= 4:
            nb = c_len // (2 * half)
            a3 = a.reshape(nb, 2 * half, HEAD_DIM)
            return jnp.broadcast_to(a3[:, half - 1:half, :], a3.shape).reshape(c_len, HEAD_DIM)
        assert half == 2
        a3 = a.reshape(c_len // 8, 8, HEAD_DIM)
        return jnp.where(sub8 < 4, a3[:, 1:2, :], a3[:, 5:6, :]).reshape(c_len, HEAD_DIM)

    def body(c, st):
        sl = pl.ds(pl.multiple_of(c * c_len, c_len), c_len)
        fz = f_ref[sl, :].astype(F32)
        sg = jax.nn.sigmoid(fz)
        f = lb + one_minus_lb * sg
        log2_f = jnp.log2(f)
        k = (one_minus_lb * (1.0 - sg)).astype(BF16)
        h1 = log2_f.astype(BF16)
        h2 = (log2_f - h1.astype(F32)).astype(BF16)
        a = jnp.dot(tri2, jnp.concatenate([h1, h2], axis=0), preferred_element_type=F32)
        a_last = a[c_len - 1:c_len, :]
        qz = q_ref[sl, :]
        q = qz * jax.nn.sigmoid(qz) * scale
        qa = q * jnp.exp2(a).astype(BF16)
        kd = k * jnp.exp2(a_last - a).astype(BF16)
        v = i_ref[sl, :]
        diag = jnp.sum(q.astype(F32) * k.astype(F32), axis=-1, keepdims=True)
        scores = jnp.where(row == col, diag, 0.0)
        for level in range(n_levels):
            if level == 0:
                ql, kl = q * f.astype(BF16), k
            else:
                decay = jnp.exp2(neg_abs(a - boundary(a, 1 << level))).astype(BF16)
                ql, kl = q * decay, k * decay
            gram = lax.dot_general(ql, kl, nt, preferred_element_type=F32)
            scores = jnp.where(differ >= (1 << level), gram, scores)
        p = scores.astype(BF16)
        o = lax.dot_general(qa, st.astype(BF16), nt, preferred_element_type=F32)
        o = o + jnp.dot(p, v, preferred_element_type=F32)
        v_t = v.astype(F32).T.astype(BF16)
        st_new = st * jnp.exp2(a_last) + jnp.dot(v_t, kd, preferred_element_type=F32)
        gz = g_ref[sl, :]
        gate = (gz * jax.nn.sigmoid(gz)).astype(F32)
        o_ref[sl, :] = (_rms(o, gain) * gate).astype(o_ref.dtype)
        return st_new

    st_ref[...] = lax.fori_loop(0, nchunks, body, st_ref[...], unroll=True)


def _hgrn(z3, lb, gain, tc):
    b, t, _ = z3.shape
    hb = HG_WIDTH // HEAD_DIM

    def zspec(k):
        return pl.BlockSpec((None, tc, HEAD_DIM), lambda bi, h, ti, k=k: (bi, ti, h + k * hb))

    return pl.pallas_call(
        functools.partial(_hgrn_kernel, nchunks=tc // CHUNK),
        out_shape=jax.ShapeDtypeStruct((b, t, HG_WIDTH), BF16),
        grid=(b, N_HEADS, t // tc),
        in_specs=[
            zspec(0), zspec(1), zspec(2), zspec(3),
            pl.BlockSpec((1, HEAD_DIM), lambda bi, h, ti: (0, h)),
            pl.BlockSpec((1, HEAD_DIM), lambda bi, h, ti: (0, 0)),
        ],
        out_specs=pl.BlockSpec((None, tc, HEAD_DIM), lambda bi, h, ti: (bi, ti, h)),
        scratch_shapes=[pltpu.VMEM((HEAD_DIM, HEAD_DIM), F32)],
        compiler_params=_params(("arbitrary", "arbitrary", "arbitrary")),
        name="hgrn",
    )(z3, z3, z3, z3, lb, gain)


SC_HALO = 8


def _short_conv(b_ref, c_ref, h_ref, w_ref, gain_ref, u_ref, new_sequence, tc):
    @pl.when(new_sequence)
    def _():
        u_ref[0:SC_HALO, :] = jnp.zeros((SC_HALO, SC_WIDTH), F32)

    u = c_ref[...].astype(F32) * h_ref[...].astype(F32)
    u_ref[SC_HALO:SC_HALO + tc, :] = u
    w = w_ref[...]
    y = (w[0:1, :] * u_ref[SC_HALO - 2:SC_HALO - 2 + tc, :]
         + w[1:2, :] * u_ref[SC_HALO - 1:SC_HALO - 1 + tc, :]
         + w[2:3, :] * u)
    out = _rms(b_ref[...].astype(F32) * y, gain_ref[...]).astype(BF16)
    u_ref[0:SC_HALO, :] = u_ref[tc:tc + SC_HALO, :]
    return out


R_E1, R_E2, R_W1, R_W2, R_RANK1, R_RANK2 = 0, 1, 2, 3, 4, 5


def _outproj_kernel(x_ref, ohg_ref, zb_ref, zc_ref, zh_ref, convw_ref, scgain_ref, wo1_ref, wo2_ref, g_ref,
                    wr_hl_ref, wr_h_ref, x1_ref, hn_ref, route_ref, cnt_ref, carry_ref, u_ref,
                    *, tm, tiles_per_seq):
    @pl.when(pl.program_id(0) == 0)
    def _():
        carry_ref[...] = jnp.zeros_like(carry_ref)

    osc = _short_conv(zb_ref, zc_ref, zh_ref, convw_ref, scgain_ref, u_ref,
                      pl.program_id(0) % tiles_per_seq == 0, tm)
    acc = jnp.dot(ohg_ref[...], wo1_ref[...], preferred_element_type=F32)
    acc = acc + jnp.dot(osc, wo2_ref[...], preferred_element_type=F32)
    x1 = x_ref[...] + acc
    x1_ref[...] = x1
    hn = _rms(x1, g_ref[...])
    _pack_tokens(hn, hn_ref)

    hi = hn.astype(BF16)
    lo = (hn - hi.astype(F32)).astype(BF16)
    both = jnp.dot(hi, wr_hl_ref[...], preferred_element_type=F32)
    logits = both[:, :LANES] + both[:, LANES:] + jnp.dot(lo, wr_h_ref[...], preferred_element_type=F32)

    lt = logits.T
    sub = lax.broadcasted_iota(jnp.int32, (EXPERTS_PER_GROUP, tm), 0)

    def first_argmax(vals, vmax):
        return jnp.min(jnp.where(vals == vmax, sub, EXPERTS_PER_GROUP), axis=0, keepdims=True)

    is_group = sub < N_GROUPS
    gl = jnp.where(is_group, lt[0:EXPERTS_PER_GROUP], NEG_BIG)
    gmax = jnp.max(gl, axis=0, keepdims=True)
    gidx = first_argmax(gl, gmax)
    g_p = 1.0 / jnp.sum(jnp.where(is_group, jnp.exp(gl - gmax), 0.0), axis=0, keepdims=True)

    el = lt[EXPERT_LANE0:EXPERT_LANE0 + EXPERTS_PER_GROUP]
    for g in range(1, N_GROUPS):
        lo_row = EXPERT_LANE0 + g * EXPERTS_PER_GROUP
        el = jnp.where(gidx == g, lt[lo_row:lo_row + EXPERTS_PER_GROUP], el)
    m1 = jnp.max(el, axis=0, keepdims=True)
    i1 = first_argmax(el, m1)
    el2 = jnp.where(sub == i1, NEG_BIG, el)
    m2 = jnp.max(el2, axis=0, keepdims=True)
    i2 = first_argmax(el2, m2)
    r = jnp.exp(m2 - m1)
    w1 = g_p / (1.0 + r)
    w2 = g_p * r / (1.0 + r)
    e1 = gidx * EXPERTS_PER_GROUP + i1
    e2 = gidx * EXPERTS_PER_GROUP + i2

    erow = lax.broadcasted_iota(jnp.int32, (N_EXPERTS, tm), 0)
    oh1 = erow == e1
    oh2 = erow == e2
    onehot = (oh1 | oh2).astype(BF16)
    trow = lax.broadcasted_iota(jnp.int32, (tm, tm), 0)
    tcol = lax.broadcasted_iota(jnp.int32, (tm, tm), 1)
    before = jnp.dot(onehot, (trow < tcol).astype(BF16), preferred_element_type=F32) + carry_ref[...]
    rank1 = jnp.sum(jnp.where(oh1, before, 0.0), axis=0, keepdims=True)
    rank2 = jnp.sum(jnp.where(oh2, before, 0.0), axis=0, keepdims=True)
    carry_ref[...] = carry_ref[...] + jnp.sum(onehot.astype(F32), axis=1, keepdims=True)
    cnt_ref[...] = carry_ref[...]

    rows = [e1.astype(F32), e2.astype(F32), w1, w2, rank1, rank2]
    table_t = jnp.concatenate(rows + [jnp.zeros((LANES - len(rows), tm), F32)], axis=0)
    route_ref[...] = table_t.T


def _outproj(x2d, ohg, z2d, conv_w, sc_gain, wo1, wo2, g_ffn, wr_hl, wr_h, tm, seq_len):
    n, d = x2d.shape
    const = lambda i: (0, 0)
    sc_first = 4 * HG_WIDTH // SC_WIDTH

    def zspec(k):
        return pl.BlockSpec((tm, SC_WIDTH), lambda i, k=k: (i, sc_first + k))

    return pl.pallas_call(
        functools.partial(_outproj_kernel, tm=tm, tiles_per_seq=seq_len // tm),
        out_shape=(
            jax.ShapeDtypeStruct((n, d), F32),
            jax.ShapeDtypeStruct((n * _token_rows(d), LANES), jnp.uint32),
            jax.ShapeDtypeStruct((n, LANES), F32),
            jax.ShapeDtypeStruct((N_EXPERTS, 1), F32),
        ),
        grid=(n // tm,),
        in_specs=[
            pl.BlockSpec((tm, d), lambda i: (i, 0)),
            pl.BlockSpec((tm, HG_WIDTH), lambda i: (i, 0)),
            zspec(0), zspec(1), zspec(2),
            pl.BlockSpec((3, SC_WIDTH), const),
            pl.BlockSpec((1, SC_WIDTH), const),
            pl.BlockSpec((HG_WIDTH, d), const),
            pl.BlockSpec((SC_WIDTH, d), lambda i: (HG_WIDTH // SC_WIDTH, 0)),
            pl.BlockSpec((1, d), const),
            pl.BlockSpec((d, 2 * LANES), const),
            pl.BlockSpec((d, LANES), const),
        ],
        out_specs=(
            pl.BlockSpec((tm, d), lambda i: (i, 0)),
            pl.BlockSpec((tm * _token_rows(d), LANES), lambda i: (i, 0)),
            pl.BlockSpec((tm, LANES), lambda i: (i, 0)),
            pl.BlockSpec((N_EXPERTS, 1), const),
        ),
        scratch_shapes=[pltpu.VMEM((N_EXPERTS, 1), F32), pltpu.VMEM((tm + SC_HALO, SC_WIDTH), F32)],
        compiler_params=_params(("arbitrary",)),
        name="outproj",
    )(x2d, ohg, z2d, z2d, z2d, conv_w, sc_gain, wo1, wo2, g_ffn, wr_hl, wr_h)


def _dispatch_kernel(cnt_ref, pad_ref, start_ref, dest_hbm, hn_ref, xs_hbm, idx_ref, zero_ref, idx_sem, sem,
                     *, td, tr, nsteps):
    i = pl.program_id(0)
    idx_copy = pltpu.make_async_copy(dest_hbm.at[pl.ds(i * 2 * td, 2 * td)], idx_ref, idx_sem)
    idx_copy.start()
    idx_copy.wait()

    def token_copy(t, slot):
        src = hn_ref.at[pl.ds(pl.multiple_of(t * tr, tr), tr)]
        return pltpu.make_async_copy(src, xs_hbm.at[pl.ds(pl.multiple_of(slot * tr, tr), tr)], sem)

    def issue(t, carry):
        token_copy(t, idx_ref[2 * t]).start(priority=0)
        token_copy(t, idx_ref[2 * t + 1]).start(priority=1)
        return carry

    lax.fori_loop(0, td, issue, 0, unroll=8)
    tile_copy = pltpu.make_async_copy(hn_ref, xs_hbm.at[pl.ds(0, td * tr)], sem)
    tile_copy.wait()
    tile_copy.wait()

    @pl.when(i == nsteps - 1)
    def _():
        zero_ref[...] = jnp.zeros_like(zero_ref)

        def pad_copy(slot):
            return pltpu.make_async_copy(zero_ref, xs_hbm.at[pl.ds(pl.multiple_of(slot * tr, tr), tr)], sem)

        def per_expert(e, carry):
            base = start_ref[e]

            def issue_pad(s, c2):
                pad_copy(base + s).start()
                return c2

            def drain_pad(s, c2):
                pad_copy(0).wait()
                return c2

            lax.fori_loop(cnt_ref[e], pad_ref[e], issue_pad, 0)
            lax.fori_loop(cnt_ref[e], pad_ref[e], drain_pad, 0)
            return carry

        lax.fori_loop(0, N_EXPERTS + 1, per_expert, 0)


def _dispatch(counts, padded, pstart, dest_flat, hn_packed, n, tr, p_rows, td):
    nsteps = n // td
    grid_spec = pltpu.PrefetchScalarGridSpec(
        num_scalar_prefetch=3,
        grid=(nsteps,),
        in_specs=[
            pl.BlockSpec(memory_space=pl.ANY),
            pl.BlockSpec((td * tr, LANES), lambda i, *_: (i, 0)),
        ],
        out_specs=pl.BlockSpec(memory_space=pl.ANY),
        scratch_shapes=[
            pltpu.SMEM((2 * td,), jnp.int32),
            pltpu.VMEM((tr, LANES), jnp.uint32),
            pltpu.SemaphoreType.DMA,
            pltpu.SemaphoreType.DMA,
        ],
    )
    return pl.pallas_call(
        functools.partial(_dispatch_kernel, td=td, tr=tr, nsteps=nsteps),
        out_shape=jax.ShapeDtypeStruct((p_rows * tr, LANES), jnp.uint32),
        grid_spec=grid_spec,
        compiler_params=_params(("arbitrary",)),
        name="dispatch",
    )(counts, padded, pstart, dest_flat, hn_packed)


def _experts_kernel(blk_e_ref, first_ref, slot_ref, next_ref, nused_ref, xs_ref, wg_hbm, wu_hbm, wd_hbm, y_ref,
                    wg_buf, wu_buf, wd_buf, sem, *, tb, tr):
    i = pl.program_id(0)

    def fetch(e, slot):
        pairs = ((wg_hbm, wg_buf), (wu_hbm, wu_buf), (wd_hbm, wd_buf))
        return [pltpu.make_async_copy(src.at[e], buf.at[slot], sem.at[slot, j]) for j, (src, buf) in enumerate(pairs)]

    @pl.when(i >= nused_ref[0])
    def _():
        y_ref[...] = jnp.zeros_like(y_ref)

    @pl.when(i < nused_ref[0])
    def _():
        slot = slot_ref[i]

        @pl.when(first_ref[i] == 1)
        def _():
            @pl.when(i == 0)
            def _():
                for copy in fetch(blk_e_ref[0], slot):
                    copy.start()

            for copy in fetch(blk_e_ref[i], slot):
                copy.wait()

            @pl.when(next_ref[i] >= 0)
            def _():
                for copy in fetch(next_ref[i], 1 - slot):
                    copy.start()

        x = _unpack_tokens(lambda s: xs_ref[pl.ds(s, tb, stride=tr), :], tr)
        a = jnp.dot(x, wg_buf[slot], preferred_element_type=F32)
        b = jnp.dot(x, wu_buf[slot], preferred_element_type=F32)
        h = a * jax.nn.sigmoid(a) * b
        _pack_tokens(jnp.dot(h, wd_buf[slot], preferred_element_type=F32), y_ref)


def _experts(block_e, counts, nused, xs_packed, wg, wu, wd, tb):
    d, ff = wg.shape[1], wg.shape[2]
    tr = _token_rows(d)
    nblocks = xs_packed.shape[0] // (tb * tr)

    blk = jnp.arange(nblocks, dtype=jnp.int32)
    first = ((blk == 0) | (block_e != jnp.roll(block_e, 1))) & (blk < nused[0])
    slot = (jnp.cumsum(first.astype(jnp.int32)) - 1) % 2
    e_iota = jnp.arange(N_EXPERTS, dtype=jnp.int32)
    later_used = (e_iota[None, :] > e_iota[:, None]) & (counts > 0)[None, :]
    next_used = jnp.min(jnp.where(later_used, e_iota[None, :], N_EXPERTS), axis=1)
    next_used = jnp.where(next_used == N_EXPERTS, -1, next_used)
    next_e = jnp.sum(jnp.where(block_e[:, None] == e_iota[None, :], next_used[None, :], 0), axis=1)

    def row_map(i, *prefetch):
        return (jnp.minimum(i, prefetch[-1][0] - 1), 0)

    grid_spec = pltpu.PrefetchScalarGridSpec(
        num_scalar_prefetch=5,
        grid=(nblocks,),
        in_specs=[
            pl.BlockSpec((tb * tr, LANES), row_map),
            pl.BlockSpec(memory_space=pl.ANY),
            pl.BlockSpec(memory_space=pl.ANY),
            pl.BlockSpec(memory_space=pl.ANY),
        ],
        out_specs=pl.BlockSpec((tb * tr, LANES), lambda i, *prefetch: (i, 0)),
        scratch_shapes=[
            pltpu.VMEM((2, d, ff), F32),
            pltpu.VMEM((2, d, ff), F32),
            pltpu.VMEM((2, ff, d), F32),
            pltpu.SemaphoreType.DMA((2, 3)),
        ],
    )
    return pl.pallas_call(
        functools.partial(_experts_kernel, tb=tb, tr=tr),
        out_shape=jax.ShapeDtypeStruct(xs_packed.shape, jnp.uint32),
        grid_spec=grid_spec,
        compiler_params=_params(("arbitrary",)),
        name="experts",
    )(block_e, first.astype(jnp.int32), slot.astype(jnp.int32), next_e.astype(jnp.int32), nused,
      xs_packed, wg, wu, wd)


def _final_kernel(dest_hbm, route_ref, x1_ref, p_ref, wple_ref, gple_ref, wgate_ref, gfin_ref, y_hbm,
                  o_ref, idx_a, idx_b, ybuf_a, ybuf_b, idx_sem, sem, *, tf, tr, nsteps):
    i = pl.program_id(0)
    halves = ((idx_a, ybuf_a), (idx_b, ybuf_b))

    def idx_copy(tile, h):
        return pltpu.make_async_copy(dest_hbm.at[pl.ds(tile * 2 * tf, 2 * tf)], halves[h][0], idx_sem.at[h])

    def gather(h):
        idx_ref, ybuf_ref = halves[h]

        def issue(t, carry):
            for k in range(2):
                src = y_hbm.at[pl.ds(pl.multiple_of(idx_ref[2 * t + k] * tr, tr), tr)]
                dst = ybuf_ref.at[k, pl.ds(pl.multiple_of(t * tr, tr), tr)]
                pltpu.make_async_copy(src, dst, sem.at[h]).start(priority=k)
            return carry

        lax.fori_loop(0, tf, issue, 0, unroll=True)

    def wait_rows(h):
        for k in range(2):
            pltpu.make_async_copy(y_hbm.at[pl.ds(0, tf * tr)], halves[h][1].at[k], sem.at[h]).wait()

    def combine(h):
        ybuf_ref = halves[h][1]
        rows = pl.ds(h * tf, tf)
        wait_rows(h)
        route = route_ref[rows, :]
        w1 = route[:, R_W1:R_W1 + 1]
        w2 = route[:, R_W2:R_W2 + 1]
        y1 = _unpack_tokens(lambda s: ybuf_ref[0, pl.ds(s, tf, stride=tr), :], tr)
        y2 = _unpack_tokens(lambda s: ybuf_ref[1, pl.ds(s, tf, stride=tr), :], tr)
        return x1_ref[rows, :] + w1 * y1 + w2 * y2

    def finish(h, x2):
        rows = pl.ds(h * tf, tf)
        ple = _rms(jnp.dot(p_ref[rows, :], wple_ref[...], preferred_element_type=F32), gple_ref[...])
        gate = jax.nn.sigmoid(jnp.dot(x2, wgate_ref[...], preferred_element_type=F32))
        o_ref[rows, :] = _rms(x2 + gate * ple, gfin_ref[...])

    @pl.when(i == 0)
    def _():
        for h in range(2):
            idx_copy(h, h).start()
            idx_copy(h, h).wait()
            gather(h)

    for h in range(2):
        idx_copy(2 * i + 2 + h, h).start()
    for h in range(2):
        x2 = combine(h)
        idx_copy(2 * i + 2 + h, h).wait()
        gather(h)
        finish(h, x2)

    @pl.when(i == nsteps - 1)
    def _():
        for h in range(2):
            wait_rows(h)


def _final(dest_flat, route, x1, p2d, layer, wple, gple, wgate, gfin, y, tf):
    n, d = x1.shape
    pd = p2d.shape[1]
    tr = _token_rows(d)
    nsteps = n // (2 * tf)
    p_block0 = layer * nsteps
    dest_padded = jnp.concatenate([dest_flat, jnp.zeros((4 * tf,), jnp.int32)])
    const = lambda i: (0, 0)
    return pl.pallas_call(
        functools.partial(_final_kernel, tf=tf, tr=tr, nsteps=nsteps),
        out_shape=jax.ShapeDtypeStruct((n, d), F32),
        grid=(nsteps,),
        in_specs=[
            pl.BlockSpec(memory_space=pl.ANY),
            pl.BlockSpec((2 * tf, LANES), lambda i: (i, 0)),
            pl.BlockSpec((2 * tf, d), lambda i: (i, 0)),
            pl.BlockSpec((2 * tf, pd), lambda i: (i + p_block0, 0)),
            pl.BlockSpec((pd, d), const, pipeline_mode=pl.Buffered(1)),
            pl.BlockSpec((1, d), const),
            pl.BlockSpec((d, d), const, pipeline_mode=pl.Buffered(1)),
            pl.BlockSpec((1, d), const),
            pl.BlockSpec(memory_space=pl.ANY),
        ],
        out_specs=pl.BlockSpec((2 * tf, d), lambda i: (i, 0)),
        scratch_shapes=[
            pltpu.SMEM((2 * tf,), jnp.int32),
            pltpu.SMEM((2 * tf,), jnp.int32),
            pltpu.VMEM((2, tf * tr, LANES), jnp.uint32),
            pltpu.VMEM((2, tf * tr, LANES), jnp.uint32),
            pltpu.SemaphoreType.DMA((2,)),
            pltpu.SemaphoreType.DMA((2,)),
        ],
        compiler_params=_params(("arbitrary",)),
        name="final",
    )(dest_padded, route, x1, p2d, wple, gple, wgate, gfin, y)


def _tile(n, want):
    t = min(n, want)
    assert n % t == 0, (n, want)
    return t


def kernel(x, p, g_mix, w_in, lb_logits, hg_norm, conv_w, sc_norm, w_out, g_ffn, w_router_group,
           w_router_expert, w_gate, w_up, w_down, w_ple, g_ple, w_ple_gate, g_final):
    b, t, d = x.shape
    n = b * t
    layer = 0
    x2d = x.reshape(n, d)

    lower_bounds = jnp.cumsum(jax.nn.softmax(lb_logits.astype(F32), axis=0), axis=0)
    lb = lower_bounds[layer].reshape(1, HG_WIDTH)

    z = _inproj(x2d, g_mix[layer].reshape(1, d), w_in[layer].astype(BF16), _tile(n, 1024), 1792)
    z3 = z.reshape(b, t, z.shape[1])
    ohg = _hgrn(z3, lb, hg_norm[layer].reshape(1, HEAD_DIM), _tile(t, 1024)).reshape(n, HG_WIDTH)

    wo = w_out[layer].astype(BF16)
    wr = jnp.concatenate([jnp.pad(w_router_group[layer], ((0, 0), (0, EXPERT_LANE0 - N_GROUPS))),
                          w_router_expert[layer]], axis=1).astype(F32)
    wr = jnp.pad(wr, ((0, 0), (0, LANES - wr.shape[1])))
    wr_hi = wr.astype(BF16)
    wr_lo = (wr - wr_hi.astype(F32)).astype(BF16)
    x1, hn, route, cnt = _outproj(
        x2d, ohg, z, conv_w[layer], sc_norm[layer].reshape(1, SC_WIDTH), wo, wo,
        g_ffn[layer].reshape(1, d), jnp.concatenate([wr_hi, wr_lo], axis=1), wr_hi, _tile(t, 512), t)

    tb = 512
    counts = cnt[:, 0].astype(jnp.int32)
    padded = (counts + tb - 1) // tb * tb
    pend = jnp.cumsum(padded)
    pstart = pend - padded
    e_ids = route[:, R_E1:R_E2 + 1].astype(jnp.int32)
    ranks = route[:, R_RANK1:R_RANK2 + 1].astype(jnp.int32)
    expert_iota = jnp.arange(N_EXPERTS, dtype=jnp.int32)
    seg_start = jnp.sum(jnp.where(e_ids[..., None] == expert_iota, pstart.astype(jnp.int32), 0), axis=-1)
    dest_flat = (seg_start + ranks).reshape(2 * n)
    nblocks = -(-(2 * n + N_EXPERTS * (tb - 1)) // tb)
    nused = (pend[-1] // tb).astype(jnp.int32)
    blk = jnp.minimum(jnp.arange(nblocks, dtype=jnp.int32), nused - 1) * tb
    block_e = jnp.minimum(jnp.sum((blk[:, None] >= pend[None, :]).astype(jnp.int32), axis=1), N_EXPERTS - 1)

    p_rows = nblocks * tb
    zero = jnp.zeros((1,), jnp.int32)
    xs = _dispatch(jnp.concatenate([counts, zero]), jnp.concatenate([padded, p_rows - pend[-1:]]),
                   jnp.concatenate([pstart, pend[-1:]]).astype(jnp.int32), dest_flat, hn, n, _token_rows(d),
                   p_rows, _tile(n, 4096))
    y = _experts(block_e, counts, nused.reshape(1), xs, w_gate[layer], w_up[layer], w_down[layer], tb)

    out = _final(dest_flat, route, x1, p.reshape(-1, p.shape[-1]), layer, w_ple[layer],
                 g_ple[layer].reshape(1, d), w_ple_gate[layer], g_final.reshape(1, d), y,
                 _tile(n, 256))
    return out.reshape(b, t, d)
```

```python
import functools

import jax
import jax.numpy as jnp
from jax import lax
from jax.experimental import pallas as pl
from jax.experimental.pallas import tpu as pltpu

F32 = jnp.float32
BF16 = jnp.bfloat16
EPS = 1e-6

HEAD_DIM = 128
N_HEADS = 8
HG_WIDTH = N_HEADS * HEAD_DIM
SC_WIDTH = 1024
CHUNK = 128
N_GROUPS = 4
EXPERTS_PER_GROUP = 8
N_EXPERTS = N_GROUPS * EXPERTS_PER_GROUP
LANES = 128
EXPERT_LANE0 = EXPERTS_PER_GROUP
NEG_BIG = -1e30
VMEM_LIMIT = 56 * 1024 * 1024


def _rms(v, gain):
    return v * lax.rsqrt(jnp.mean(v * v, axis=-1, keepdims=True) + EPS) * gain


def _token_rows(d):
    return d // (2 * LANES)


def _pack_tokens(v, out_ref):
    rows, d = v.shape
    tr = _token_rows(d)
    bits = lax.bitcast_convert_type(v.astype(BF16).astype(F32), jnp.uint32)
    words = bits[:, :d // 2] | (bits[:, d // 2:] >> 16)
    for s in range(tr):
        out_ref[pl.ds(s, rows, stride=tr), :] = words[:, s * LANES:(s + 1) * LANES]


def _unpack_tokens(load_rows, tr):
    words = [load_rows(s) for s in range(tr)]
    high = [lax.bitcast_convert_type(w & jnp.uint32(0xFFFF0000), F32) for w in words]
    low = [lax.bitcast_convert_type(w << 16, F32) for w in words]
    return jnp.concatenate(high + low, axis=1)


def _params(sem):
    return pltpu.CompilerParams(dimension_semantics=sem, vmem_limit_bytes=VMEM_LIMIT)


def _inproj_kernel(x_ref, g_ref, w_ref, z_ref, xn_ref):
    @pl.when(pl.program_id(1) == 0)
    def _():
        xn_ref[...] = _rms(x_ref[...], g_ref[...]).astype(BF16)

    z_ref[...] = jnp.dot(xn_ref[...], w_ref[...], preferred_element_type=F32).astype(z_ref.dtype)


def _inproj(x2d, g_mix, w_in_bf16, tm, tn):
    n, d = x2d.shape
    cols = w_in_bf16.shape[1]
    return pl.pallas_call(
        _inproj_kernel,
        out_shape=jax.ShapeDtypeStruct((n, cols), BF16),
        grid=(n // tm, cols // tn),
        in_specs=[
            pl.BlockSpec((tm, d), lambda i, j: (i, 0)),
            pl.BlockSpec((1, d), lambda i, j: (0, 0)),
            pl.BlockSpec((d, tn), lambda i, j: (0, j)),
        ],
        out_specs=pl.BlockSpec((tm, tn), lambda i, j: (i, j)),
        scratch_shapes=[pltpu.VMEM((tm, d), BF16)],
        compiler_params=_params(("arbitrary", "arbitrary")),
        name="inproj",
    )(x2d, g_mix, w_in_bf16)


def _hgrn_kernel(q_ref, f_ref, i_ref, g_ref, lb_ref, gain_ref, o_ref, st_ref, *, nchunks):
    @pl.when(pl.program_id(2) == 0)
    def _():
        st_ref[...] = jnp.zeros_like(st_ref)

    c_len = CHUNK
    n_levels = c_len.bit_length() - 1
    lb = lb_ref[...]
    one_minus_lb = 1.0 - lb
    gain = gain_ref[...]
    row = lax.broadcasted_iota(jnp.int32, (c_len, c_len), 0)
    col = lax.broadcasted_iota(jnp.int32, (c_len, c_len), 1)
    tri = (row >= col).astype(BF16)
    tri2 = jnp.concatenate([tri, tri], axis=1)
    differ = jnp.where(col < row, row ^ col, 0)
    sub8 = lax.broadcasted_iota(jnp.int32, (c_len // 8, 8, HEAD_DIM), 1)
    scale = HEAD_DIM ** -0.5
    nt = (((1,), (1,)), ((), ()))
    sign_bit = jnp.uint32(0x80000000)

    def neg_abs(d):
        return lax.bitcast_convert_type(lax.bitcast_convert_type(d, jnp.uint32) | sign_bit, F32)

    def boundary(a, half):
        if half >= 4:
            nb = c_len // (2 * half)
            a3 = a.reshape(nb, 2 * half, HEAD_DIM)
            return jnp.broadcast_to(a3[:, half - 1:half, :], a3.shape).reshape(c_len, HEAD_DIM)
        assert half == 2
        a3 = a.reshape(c_len // 8, 8, HEAD_DIM)
        return jnp.where(sub8 < 4, a3[:, 1:2, :], a3[:, 5:6, :]).reshape(c_len, HEAD_DIM)

    def body(c, st):
        sl = pl.ds(pl.multiple_of(c * c_len, c_len), c_len)
        fz = f_ref[sl, :].astype(F32)
        sg = jax.nn.sigmoid(fz)
        f = lb + one_minus_lb * sg
        log2_f = jnp.log2(f)
        k = (one_minus_lb * (1.0 - sg)).astype(BF16)
        h1 = log2_f.astype(BF16)
        h2 = (log2_f - h1.astype(F32)).astype(BF16)
        a = jnp.dot(tri2, jnp.concatenate([h1, h2], axis=0), preferred_element_type=F32)
        a_last = a[c_len - 1:c_len, :]
        qz = q_ref[sl, :]
        q = qz * jax.nn.sigmoid(qz) * scale
        qa = q * jnp.exp2(a).astype(BF16)
        kd = k * jnp.exp2(a_last - a).astype(BF16)
        v = i_ref[sl, :]
        diag = jnp.sum(q.astype(F32) * k.astype(F32), axis=-1, keepdims=True)
        scores = jnp.where(row == col, diag, 0.0)
        for level in range(n_levels):
            if level == 0:
                ql, kl = q * f.astype(BF16), k
            else:
                decay = jnp.exp2(neg_abs(a - boundary(a, 1 << level))).astype(BF16)
                ql, kl = q * decay, k * decay
            gram = lax.dot_general(ql, kl, nt, preferred_element_type=F32)
            scores = jnp.where(differ >= (1 << level), gram, scores)
        p = scores.astype(BF16)
        o = lax.dot_general(qa, st.astype(BF16), nt, preferred_element_type=F32)
        o = o + jnp.dot(p, v, preferred_element_type=F32)
        v_t = v.astype(F32).T.astype(BF16)
        st_new = st * jnp.exp2(a_last) + jnp.dot(v_t, kd, preferred_element_type=F32)
        gz = g_ref[sl, :]
        gate = (gz * jax.nn.sigmoid(gz)).astype(F32)
        o_ref[sl, :] = (_rms(o, gain) * gate).astype(o_ref.dtype)
        return st_new

    st_ref[...] = lax.fori_loop(0, nchunks, body, st_ref[...], unroll=True)


def _hgrn(z3, lb, gain, tc):
    b, t, _ = z3.shape
    hb = HG_WIDTH // HEAD_DIM

    def zspec(k):
        return pl.BlockSpec((None, tc, HEAD_DIM), lambda bi, h, ti, k=k: (bi, ti, h + k * hb))

    return pl.pallas_call(
        functools.partial(_hgrn_kernel, nchunks=tc // CHUNK),
        out_shape=jax.ShapeDtypeStruct((b, t, HG_WIDTH), BF16),
        grid=(b, N_HEADS, t // tc),
        in_specs=[
            zspec(0), zspec(1), zspec(2), zspec(3),
            pl.BlockSpec((1, HEAD_DIM), lambda bi, h, ti: (0, h)),
            pl.BlockSpec((1, HEAD_DIM), lambda bi, h, ti: (0, 0)),
        ],
        out_specs=pl.BlockSpec((None, tc, HEAD_DIM), lambda bi, h, ti: (bi, ti, h)),
        scratch_shapes=[pltpu.VMEM((HEAD_DIM, HEAD_DIM), F32)],
        compiler_params=_params(("arbitrary", "arbitrary", "arbitrary")),
        name="hgrn",
    )(z3, z3, z3, z3, lb, gain)


SC_HALO = 8


def _short_conv(b_ref, c_ref, h_ref, w_ref, gain_ref, u_ref, new_sequence, tc):
    @pl.when(new_sequence)
    def _():
        u_ref[0:SC_HALO, :] = jnp.zeros((SC_HALO, SC_WIDTH), F32)

    u = c_ref[...].astype(F32) * h_ref[...].astype(F32)
    u_ref[SC_HALO:SC_HALO + tc, :] = u
    w = w_ref[...]
    y = (w[0:1, :] * u_ref[SC_HALO - 2:SC_HALO - 2 + tc, :]
         + w[1:2, :] * u_ref[SC_HALO - 1:SC_HALO - 1 + tc, :]
         + w[2:3, :] * u)
    out = _rms(b_ref[...].astype(F32) * y, gain_ref[...]).astype(BF16)
    u_ref[0:SC_HALO, :] = u_ref[tc:tc + SC_HALO, :]
    return out


R_E1, R_E2, R_W1, R_W2, R_RANK1, R_RANK2 = 0, 1, 2, 3, 4, 5


def _outproj_kernel(x_ref, ohg_ref, zb_ref, zc_ref, zh_ref, convw_ref, scgain_ref, wo1_ref, wo2_ref, g_ref,
                    wr_hl_ref, wr_h_ref, x1_ref, hn_ref, route_ref, cnt_ref, carry_ref, u_ref,
                    *, tm, tiles_per_seq):
    @pl.when(pl.program_id(0) == 0)
    def _():
        carry_ref[...] = jnp.zeros_like(carry_ref)

    osc = _short_conv(zb_ref, zc_ref, zh_ref, convw_ref, scgain_ref, u_ref,
                      pl.program_id(0) % tiles_per_seq == 0, tm)
    acc = jnp.dot(ohg_ref[...], wo1_ref[...], preferred_element_type=F32)
    acc = acc + jnp.dot(osc, wo2_ref[...], preferred_element_type=F32)
    x1 = x_ref[...] + acc
    x1_ref[...] = x1
    hn = _rms(x1, g_ref[...])
    _pack_tokens(hn, hn_ref)

    hi = hn.astype(BF16)
    lo = (hn - hi.astype(F32)).astype(BF16)
    both = jnp.dot(hi, wr_hl_ref[...], preferred_element_type=F32)
    logits = both[:, :LANES] + both[:, LANES:] + jnp.dot(lo, wr_h_ref[...], preferred_element_type=F32)

    lt = logits.T
    sub = lax.broadcasted_iota(jnp.int32, (EXPERTS_PER_GROUP, tm), 0)

    def first_argmax(vals, vmax):
        return jnp.min(jnp.where(vals == vmax, sub, EXPERTS_PER_GROUP), axis=0, keepdims=True)

    is_group = sub < N_GROUPS
    gl = jnp.where(is_group, lt[0:EXPERTS_PER_GROUP], NEG_BIG)
    gmax = jnp.max(gl, axis=0, keepdims=True)
    gidx = first_argmax(gl, gmax)
    g_p = 1.0 / jnp.sum(jnp.where(is_group, jnp.exp(gl - gmax), 0.0), axis=0, keepdims=True)

    el = lt[EXPERT_LANE0:EXPERT_LANE0 + EXPERTS_PER_GROUP]
    for g in range(1, N_GROUPS):
        lo_row = EXPERT_LANE0 + g * EXPERTS_PER_GROUP
        el = jnp.where(gidx == g, lt[lo_row:lo_row + EXPERTS_PER_GROUP], el)
    m1 = jnp.max(el, axis=0, keepdims=True)
    i1 = first_argmax(el, m1)
    el2 = jnp.where(sub == i1, NEG_BIG, el)
    m2 = jnp.max(el2, axis=0, keepdims=True)
    i2 = first_argmax(el2, m2)
    r = jnp.exp(m2 - m1)
    w1 = g_p / (1.0 + r)
    w2 = g_p * r / (1.0 + r)
    e1 = gidx * EXPERTS_PER_GROUP + i1
    e2 = gidx * EXPERTS_PER_GROUP + i2

    erow = lax.broadcasted_iota(jnp.int32, (N_EXPERTS, tm), 0)
    oh1 = erow == e1
    oh2 = erow == e2
    onehot = (oh1 | oh2).astype(BF16)
    trow = lax.broadcasted_iota(jnp.int32, (tm, tm), 0)
    tcol = lax.broadcasted_iota(jnp.int32, (tm, tm), 1)
    before = jnp.dot(onehot, (trow < tcol).astype(BF16), preferred_element_type=F32) + carry_ref[...]
    rank1 = jnp.sum(jnp.where(oh1, before, 0.0), axis=0, keepdims=True)
    rank2 = jnp.sum(jnp.where(oh2, before, 0.0), axis=0, keepdims=True)
    carry_ref[...] = carry_ref[...] + jnp.sum(onehot.astype(F32), axis=1, keepdims=True)
    cnt_ref[...] = carry_ref[...]

    rows = [e1.astype(F32), e2.astype(F32), w1, w2, rank1, rank2]
    table_t = jnp.concatenate(rows + [jnp.zeros((LANES - len(rows), tm), F32)], axis=0)
    route_ref[...] = table_t.T


def _outproj(x2d, ohg, z2d, conv_w, sc_gain, wo1, wo2, g_ffn, wr_hl, wr_h, tm, seq_len):
    n, d = x2d.shape
    const = lambda i: (0, 0)
    sc_first = 4 * HG_WIDTH // SC_WIDTH

    def zspec(k):
        return pl.BlockSpec((tm, SC_WIDTH), lambda i, k=k: (i, sc_first + k))

    return pl.pallas_call(
        functools.partial(_outproj_kernel, tm=tm, tiles_per_seq=seq_len // tm),
        out_shape=(
            jax.ShapeDtypeStruct((n, d), F32),
            jax.ShapeDtypeStruct((n * _token_rows(d), LANES), jnp.uint32),
            jax.ShapeDtypeStruct((n, LANES), F32),
            jax.ShapeDtypeStruct((N_EXPERTS, 1), F32),
        ),
        grid=(n // tm,),
        in_specs=[
            pl.BlockSpec((tm, d), lambda i: (i, 0)),
            pl.BlockSpec((tm, HG_WIDTH), lambda i: (i, 0)),
            zspec(0), zspec(1), zspec(2),
            pl.BlockSpec((3, SC_WIDTH), const),
            pl.BlockSpec((1, SC_WIDTH), const),
            pl.BlockSpec((HG_WIDTH, d), const),
            pl.BlockSpec((SC_WIDTH, d), lambda i: (HG_WIDTH // SC_WIDTH, 0)),
            pl.BlockSpec((1, d), const),
            pl.BlockSpec((d, 2 * LANES), const),
            pl.BlockSpec((d, LANES), const),
        ],
        out_specs=(
            pl.BlockSpec((tm, d), lambda i: (i, 0)),
            pl.BlockSpec((tm * _token_rows(d), LANES), lambda i: (i, 0)),
            pl.BlockSpec((tm, LANES), lambda i: (i, 0)),
            pl.BlockSpec((N_EXPERTS, 1), const),
        ),
        scratch_shapes=[pltpu.VMEM((N_EXPERTS, 1), F32), pltpu.VMEM((tm + SC_HALO, SC_WIDTH), F32)],
        compiler_params=_params(("arbitrary",)),
        name="outproj",
    )(x2d, ohg, z2d, z2d, z2d, conv_w, sc_gain, wo1, wo2, g_ffn, wr_hl, wr_h)


ZERO_ROWS = 512


def _dispatch_kernel(cnt_ref, pad_ref, start_ref, dest_hbm, hn_ref, xs_hbm, idx_ref, zero_ref, idx_sem, sem,
                     *, td, tr, nsteps):
    i = pl.program_id(0)
    idx_copy = pltpu.make_async_copy(dest_hbm.at[pl.ds(i * 2 * td, 2 * td)], idx_ref, idx_sem)
    idx_copy.start()
    idx_copy.wait()

    def token_copy(t, slot):
        src = hn_ref.at[pl.ds(pl.multiple_of(t * tr, tr), tr)]
        return pltpu.make_async_copy(src, xs_hbm.at[pl.ds(pl.multiple_of(slot * tr, tr), tr)], sem)

    def issue(t, carry):
        token_copy(t, idx_ref[2 * t]).start(priority=0)
        token_copy(t, idx_ref[2 * t + 1]).start(priority=1)
        return carry

    lax.fori_loop(0, td, issue, 0, unroll=8)
    tile_copy = pltpu.make_async_copy(hn_ref, xs_hbm.at[pl.ds(0, td * tr)], sem)
    tile_copy.wait()
    tile_copy.wait()

    @pl.when(i == nsteps - 1)
    def _():
        zero_ref[...] = jnp.zeros_like(zero_ref)
        zrows = zero_ref.shape[0] // tr

        def run_copy(pos, rows):
            dst = xs_hbm.at[pl.ds(pl.multiple_of(pos * tr, tr), rows * tr)]
            return pltpu.make_async_copy(zero_ref.at[pl.ds(0, rows * tr)], dst, sem)

        def per_run(e, carry):
            n = pad_ref[e] - cnt_ref[e]
            pos0 = start_ref[e] + cnt_ref[e]
            nfull = n // zrows
            rem = n - nfull * zrows
            bits = [1 << s for s in reversed(range(zrows.bit_length() - 1))]

            def start_full(j, c2):
                run_copy(pos0 + j * zrows, zrows).start()
                return c2

            def wait_full(j, c2):
                run_copy(0, zrows).wait()
                return c2

            lax.fori_loop(0, nfull, start_full, 0)
            pos = pos0 + nfull * zrows
            for bit in bits:
                @pl.when((rem & bit) != 0)
                def _(pos=pos, bit=bit):
                    run_copy(pos, bit).start()

                pos = pos + (rem & bit)
            lax.fori_loop(0, nfull, wait_full, 0)
            for bit in bits:
                @pl.when((rem & bit) != 0)
                def _(bit=bit):
                    run_copy(0, bit).wait()

            return carry

        lax.fori_loop(0, N_EXPERTS + 1, per_run, 0)


def _dispatch(counts, padded, pstart, dest_flat, hn_packed, n, tr, p_rows, td):
    nsteps = n // td
    grid_spec = pltpu.PrefetchScalarGridSpec(
        num_scalar_prefetch=3,
        grid=(nsteps,),
        in_specs=[
            pl.BlockSpec(memory_space=pl.ANY),
            pl.BlockSpec((td * tr, LANES), lambda i, *_: (i, 0)),
        ],
        out_specs=pl.BlockSpec(memory_space=pl.ANY),
        scratch_shapes=[
            pltpu.SMEM((2 * td,), jnp.int32),
            pltpu.VMEM((ZERO_ROWS * tr, LANES), jnp.uint32),
            pltpu.SemaphoreType.DMA,
            pltpu.SemaphoreType.DMA,
        ],
    )
    return pl.pallas_call(
        functools.partial(_dispatch_kernel, td=td, tr=tr, nsteps=nsteps),
        out_shape=jax.ShapeDtypeStruct((p_rows * tr, LANES), jnp.uint32),
        grid_spec=grid_spec,
        compiler_params=_params(("arbitrary",)),
        name="dispatch",
    )(counts, padded, pstart, dest_flat, hn_packed)


def _experts_kernel(blk_e_ref, first_ref, slot_ref, next_ref, nused_ref, xs_ref, wg_hbm, wu_hbm, wd_hbm, y_ref,
                    wg_buf, wu_buf, wd_buf, sem, *, tb, tr):
    i = pl.program_id(0)

    def fetch(e, slot):
        pairs = ((wg_hbm, wg_buf), (wu_hbm, wu_buf), (wd_hbm, wd_buf))
        return [pltpu.make_async_copy(src.at[e], buf.at[slot], sem.at[slot, j]) for j, (src, buf) in enumerate(pairs)]

    @pl.when(i >= nused_ref[0])
    def _():
        y_ref[...] = jnp.zeros_like(y_ref)

    @pl.when(i < nused_ref[0])
    def _():
        slot = slot_ref[i]

        @pl.when(first_ref[i] == 1)
        def _():
            @pl.when(i == 0)
            def _():
                for copy in fetch(blk_e_ref[0], slot):
                    copy.start()

            for copy in fetch(blk_e_ref[i], slot):
                copy.wait()

            @pl.when(next_ref[i] >= 0)
            def _():
                for copy in fetch(next_ref[i], 1 - slot):
                    copy.start()

        x = _unpack_tokens(lambda s: xs_ref[pl.ds(s, tb, stride=tr), :], tr)
        a = jnp.dot(x, wg_buf[slot], preferred_element_type=F32)
        b = jnp.dot(x, wu_buf[slot], preferred_element_type=F32)
        h = a * jax.nn.sigmoid(a) * b
        _pack_tokens(jnp.dot(h, wd_buf[slot], preferred_element_type=F32), y_ref)


def _experts(block_e, counts, nused, xs_packed, wg, wu, wd, tb):
    d, ff = wg.shape[1], wg.shape[2]
    tr = _token_rows(d)
    nblocks = xs_packed.shape[0] // (tb * tr)

    blk = jnp.arange(nblocks, dtype=jnp.int32)
    first = ((blk == 0) | (block_e != jnp.roll(block_e, 1))) & (blk < nused[0])
    slot = (jnp.cumsum(first.astype(jnp.int32)) - 1) % 2
    e_iota = jnp.arange(N_EXPERTS, dtype=jnp.int32)
    later_used = (e_iota[None, :] > e_iota[:, None]) & (counts > 0)[None, :]
    next_used = jnp.min(jnp.where(later_used, e_iota[None, :], N_EXPERTS), axis=1)
    next_used = jnp.where(next_used == N_EXPERTS, -1, next_used)
    next_e = jnp.sum(jnp.where(block_e[:, None] == e_iota[None, :], next_used[None, :], 0), axis=1)

    def row_map(i, *prefetch):
        return (jnp.minimum(i, prefetch[-1][0] - 1), 0)

    grid_spec = pltpu.PrefetchScalarGridSpec(
        num_scalar_prefetch=5,
        grid=(nblocks,),
        in_specs=[
            pl.BlockSpec((tb * tr, LANES), row_map),
            pl.BlockSpec(memory_space=pl.ANY),
            pl.BlockSpec(memory_space=pl.ANY),
            pl.BlockSpec(memory_space=pl.ANY),
        ],
        out_specs=pl.BlockSpec((tb * tr, LANES), lambda i, *prefetch: (i, 0)),
        scratch_shapes=[
            pltpu.VMEM((2, d, ff), F32),
            pltpu.VMEM((2, d, ff), F32),
            pltpu.VMEM((2, ff, d), F32),
            pltpu.SemaphoreType.DMA((2, 3)),
        ],
    )
    return pl.pallas_call(
        functools.partial(_experts_kernel, tb=tb, tr=tr),
        out_shape=jax.ShapeDtypeStruct(xs_packed.shape, jnp.uint32),
        grid_spec=grid_spec,
        compiler_params=_params(("arbitrary",)),
        name="experts",
    )(block_e, first.astype(jnp.int32), slot.astype(jnp.int32), next_e.astype(jnp.int32), nused,
      xs_packed, wg, wu, wd)


def _final_kernel(dest_hbm, route_ref, x1_ref, p_ref, wple_ref, gple_ref, wgate_ref, gfin_ref, y_hbm,
                  o_ref, idx_a, idx_b, ybuf_a, ybuf_b, idx_sem, sem, *, tf, tr, nsteps):
    i = pl.program_id(0)
    halves = ((idx_a, ybuf_a), (idx_b, ybuf_b))

    def idx_copy(tile, h):
        return pltpu.make_async_copy(dest_hbm.at[pl.ds(tile * 2 * tf, 2 * tf)], halves[h][0], idx_sem.at[h])

    def gather(h):
        idx_ref, ybuf_ref = halves[h]

        def issue(t, carry):
            for k in range(2):
                src = y_hbm.at[pl.ds(pl.multiple_of(idx_ref[2 * t + k] * tr, tr), tr)]
                dst = ybuf_ref.at[k, pl.ds(pl.multiple_of(t * tr, tr), tr)]
                pltpu.make_async_copy(src, dst, sem.at[h]).start(priority=k)
            return carry

        lax.fori_loop(0, tf, issue, 0, unroll=True)

    def wait_rows(h):
        for k in range(2):
            pltpu.make_async_copy(y_hbm.at[pl.ds(0, tf * tr)], halves[h][1].at[k], sem.at[h]).wait()

    def combine(h):
        ybuf_ref = halves[h][1]
        rows = pl.ds(h * tf, tf)
        wait_rows(h)
        route = route_ref[rows, :]
        w1 = route[:, R_W1:R_W1 + 1]
        w2 = route[:, R_W2:R_W2 + 1]
        y1 = _unpack_tokens(lambda s: ybuf_ref[0, pl.ds(s, tf, stride=tr), :], tr)
        y2 = _unpack_tokens(lambda s: ybuf_ref[1, pl.ds(s, tf, stride=tr), :], tr)
        return x1_ref[rows, :] + w1 * y1 + w2 * y2

    def finish(h, x2):
        rows = pl.ds(h * tf, tf)
        ple = _rms(jnp.dot(p_ref[rows, :], wple_ref[...], preferred_element_type=F32), gple_ref[...])
        gate = jax.nn.sigmoid(jnp.dot(x2, wgate_ref[...], preferred_element_type=F32))
        o_ref[rows, :] = _rms(x2 + gate * ple, gfin_ref[...])

    @pl.when(i == 0)
    def _():
        for h in range(2):
            idx_copy(h, h).start()
            idx_copy(h, h).wait()
            gather(h)

    for h in range(2):
        idx_copy(2 * i + 2 + h, h).start()
    for h in range(2):
        x2 = combine(h)
        idx_copy(2 * i + 2 + h, h).wait()
        gather(h)
        finish(h, x2)

    @pl.when(i == nsteps - 1)
    def _():
        for h in range(2):
            wait_rows(h)


def _final(dest_flat, route, x1, p2d, layer, wple, gple, wgate, gfin, y, tf):
    n, d = x1.shape
    pd = p2d.shape[1]
    tr = _token_rows(d)
    assert n % (2 * tf) == 0, (n, tf)
    nsteps = n // (2 * tf)
    p_block0 = layer * nsteps
    dest_padded = jnp.concatenate([dest_flat, jnp.zeros((4 * tf,), jnp.int32)])
    const = lambda i: (0, 0)
    return pl.pallas_call(
        functools.partial(_final_kernel, tf=tf, tr=tr, nsteps=nsteps),
        out_shape=jax.ShapeDtypeStruct((n, d), F32),
        grid=(nsteps,),
        in_specs=[
            pl.BlockSpec(memory_space=pl.ANY),
            pl.BlockSpec((2 * tf, LANES), lambda i: (i, 0)),
            pl.BlockSpec((2 * tf, d), lambda i: (i, 0)),
            pl.BlockSpec((2 * tf, pd), lambda i: (i + p_block0, 0)),
            pl.BlockSpec((pd, d), const, pipeline_mode=pl.Buffered(1)),
            pl.BlockSpec((1, d), const),
            pl.BlockSpec((d, d), const, pipeline_mode=pl.Buffered(1)),
            pl.BlockSpec((1, d), const),
            pl.BlockSpec(memory_space=pl.ANY),
        ],
        out_specs=pl.BlockSpec((2 * tf, d), lambda i: (i, 0)),
        scratch_shapes=[
            pltpu.SMEM((2 * tf,), jnp.int32),
            pltpu.SMEM((2 * tf,), jnp.int32),
            pltpu.VMEM((2, tf * tr, LANES), jnp.uint32),
            pltpu.VMEM((2, tf * tr, LANES), jnp.uint32),
            pltpu.SemaphoreType.DMA((2,)),
            pltpu.SemaphoreType.DMA((2,)),
        ],
        compiler_params=_params(("arbitrary",)),
        name="final",
    )(dest_padded, route, x1, p2d, wple, gple, wgate, gfin, y)


def _tile(n, want):
    t = min(n, want)
    assert n % t == 0, (n, want)
    return t


def kernel(x, p, g_mix, w_in, lb_logits, hg_norm, conv_w, sc_norm, w_out, g_ffn, w_router_group,
           w_router_expert, w_gate, w_up, w_down, w_ple, g_ple, w_ple_gate, g_final):
    b, t, d = x.shape
    n = b * t
    layer = 0
    x2d = x.reshape(n, d)

    lower_bounds = jnp.cumsum(jax.nn.softmax(lb_logits.astype(F32), axis=0), axis=0)
    lb = lower_bounds[layer].reshape(1, HG_WIDTH)

    z = _inproj(x2d, g_mix[layer].reshape(1, d), w_in[layer].astype(BF16), _tile(n, 1024), 1792)
    z3 = z.reshape(b, t, z.shape[1])
    ohg = _hgrn(z3, lb, hg_norm[layer].reshape(1, HEAD_DIM), _tile(t, 1024)).reshape(n, HG_WIDTH)

    wo = w_out[layer].astype(BF16)
    wr = jnp.concatenate([jnp.pad(w_router_group[layer], ((0, 0), (0, EXPERT_LANE0 - N_GROUPS))),
                          w_router_expert[layer]], axis=1).astype(F32)
    wr = jnp.pad(wr, ((0, 0), (0, LANES - wr.shape[1])))
    wr_hi = wr.astype(BF16)
    wr_lo = (wr - wr_hi.astype(F32)).astype(BF16)
    x1, hn, route, cnt = _outproj(
        x2d, ohg, z, conv_w[layer], sc_norm[layer].reshape(1, SC_WIDTH), wo, wo,
        g_ffn[layer].reshape(1, d), jnp.concatenate([wr_hi, wr_lo], axis=1), wr_hi, _tile(t, 512), t)

    tb = 512
    counts = cnt[:, 0].astype(jnp.int32)
    padded = (counts + tb - 1) // tb * tb
    pend = jnp.cumsum(padded)
    pstart = pend - padded
    e_ids = route[:, R_E1:R_E2 + 1].astype(jnp.int32)
    ranks = route[:, R_RANK1:R_RANK2 + 1].astype(jnp.int32)
    expert_iota = jnp.arange(N_EXPERTS, dtype=jnp.int32)
    seg_start = jnp.sum(jnp.where(e_ids[..., None] == expert_iota, pstart.astype(jnp.int32), 0), axis=-1)
    dest_flat = (seg_start + ranks).reshape(2 * n)
    nblocks = -(-(2 * n + N_EXPERTS * (tb - 1)) // tb)
    nused = (pend[-1] // tb).astype(jnp.int32)
    blk = jnp.minimum(jnp.arange(nblocks, dtype=jnp.int32), nused - 1) * tb
    block_e = jnp.minimum(jnp.sum((blk[:, None] >= pend[None, :]).astype(jnp.int32), axis=1), N_EXPERTS - 1)

    p_rows = nblocks * tb
    zero = jnp.zeros((1,), jnp.int32)
    xs = _dispatch(jnp.concatenate([counts, zero]), jnp.concatenate([padded, p_rows - pend[-1:]]),
                   jnp.concatenate([pstart, pend[-1:]]).astype(jnp.int32), dest_flat, hn, n, _token_rows(d),
                   p_rows, _tile(n, 4096))
    y = _experts(block_e, counts, nused.reshape(1), xs, w_gate[layer], w_up[layer], w_down[layer], tb)

    out = _final(dest_flat, route, x1, p.reshape(-1, p.shape[-1]), layer, w_ple[layer],
                 g_ple[layer].reshape(1, d), w_ple_gate[layer], g_final.reshape(1, d), y,
                 _tile(n, 256))
    return out.reshape(b, t, d)
```

```python
import functools

import jax
import jax.numpy as jnp
from jax import lax
from jax.experimental import pallas as pl
from jax.experimental.pallas import tpu as pltpu

F32 = jnp.float32
BF16 = jnp.bfloat16
EPS = 1e-6

HEAD_DIM = 128
N_HEADS = 8
HG_WIDTH = N_HEADS * HEAD_DIM
SC_WIDTH = 1024
CHUNK = 128
N_GROUPS = 4
EXPERTS_PER_GROUP = 8
N_EXPERTS = N_GROUPS * EXPERTS_PER_GROUP
LANES = 128
EXPERT_LANE0 = EXPERTS_PER_GROUP
NEG_BIG = -1e30
VMEM_LIMIT = 56 * 1024 * 1024


def _rms(v, gain):
    return v * lax.rsqrt(jnp.mean(v * v, axis=-1, keepdims=True) + EPS) * gain


def _token_rows(d):
    return d // (2 * LANES)


def _pack_tokens(v, out_ref):
    rows, d = v.shape
    tr = _token_rows(d)
    bits = lax.bitcast_convert_type(v.astype(BF16).astype(F32), jnp.uint32)
    words = bits[:, :d // 2] | (bits[:, d // 2:] >> 16)
    for s in range(tr):
        out_ref[pl.ds(s, rows, stride=tr), :] = words[:, s * LANES:(s + 1) * LANES]


def _unpack_tokens(load_rows, tr):
    words = [load_rows(s) for s in range(tr)]
    high = [lax.bitcast_convert_type(w & jnp.uint32(0xFFFF0000), F32) for w in words]
    low = [lax.bitcast_convert_type(w << 16, F32) for w in words]
    return jnp.concatenate(high + low, axis=1)


def _params(sem):
    return pltpu.CompilerParams(dimension_semantics=sem, vmem_limit_bytes=VMEM_LIMIT)


def _inproj_kernel(x_ref, g_ref, w_ref, z_ref, xn_ref):
    @pl.when(pl.program_id(1) == 0)
    def _():
        xn_ref[...] = _rms(x_ref[...], g_ref[...]).astype(BF16)

    z_ref[...] = jnp.dot(xn_ref[...], w_ref[...], preferred_element_type=F32).astype(z_ref.dtype)


def _inproj(x2d, g_mix, w_in_bf16, tm, tn):
    n, d = x2d.shape
    cols = w_in_bf16.shape[1]
    return pl.pallas_call(
        _inproj_kernel,
        out_shape=jax.ShapeDtypeStruct((n, cols), BF16),
        grid=(n // tm, cols // tn),
        in_specs=[
            pl.BlockSpec((tm, d), lambda i, j: (i, 0)),
            pl.BlockSpec((1, d), lambda i, j: (0, 0)),
            pl.BlockSpec((d, tn), lambda i, j: (0, j)),
        ],
        out_specs=pl.BlockSpec((tm, tn), lambda i, j: (i, j)),
        scratch_shapes=[pltpu.VMEM((tm, d), BF16)],
        compiler_params=_params(("arbitrary", "arbitrary")),
        name="inproj",
    )(x2d, g_mix, w_in_bf16)


def _hgrn_kernel(q_ref, f_ref, i_ref, g_ref, lb_ref, gain_ref, o_ref, st_ref, *, nchunks):
    @pl.when(pl.program_id(2) == 0)
    def _():
        st_ref[...] = jnp.zeros_like(st_ref)

    c_len = CHUNK
    n_levels = c_len.bit_length() - 1
    lb = lb_ref[...]
    one_minus_lb = 1.0 - lb
    gain = gain_ref[...]
    row = lax.broadcasted_iota(jnp.int32, (c_len, c_len), 0)
    col = lax.broadcasted_iota(jnp.int32, (c_len, c_len), 1)
    tri = (row >= col).astype(BF16)
    tri2 = jnp.concatenate([tri, tri], axis=1)
    differ = jnp.where(col < row, row ^ col, 0)
    sub8 = lax.broadcasted_iota(jnp.int32, (c_len // 8, 8, HEAD_DIM), 1)
    scale = HEAD_DIM ** -0.5
    nt = (((1,), (1,)), ((), ()))
    sign_bit = jnp.uint32(0x80000000)

    def neg_abs(d):
        return lax.bitcast_convert_type(lax.bitcast_convert_type(d, jnp.uint32) | sign_bit, F32)

    def boundary(a, half):
        if half >= 4:
            nb = c_len // (2 * half)
            a3 = a.reshape(nb, 2 * half, HEAD_DIM)
            return jnp.broadcast_to(a3[:, half - 1:half, :], a3.shape).reshape(c_len, HEAD_DIM)
        assert half == 2
        a3 = a.reshape(c_len // 8, 8, HEAD_DIM)
        return jnp.where(sub8 < 4, a3[:, 1:2, :], a3[:, 5:6, :]).reshape(c_len, HEAD_DIM)

    def body(c, st):
        sl = pl.ds(pl.multiple_of(c * c_len, c_len), c_len)
        fz = f_ref[sl, :].astype(F32)
        sg = jax.nn.sigmoid(fz)
        f = lb + one_minus_lb * sg
        log2_f = jnp.log2(f)
        k = (one_minus_lb * (1.0 - sg)).astype(BF16)
        h1 = log2_f.astype(BF16)
        h2 = (log2_f - h1.astype(F32)).astype(BF16)
        a = jnp.dot(tri2, jnp.concatenate([h1, h2], axis=0), preferred_element_type=F32)
        a_last = a[c_len - 1:c_len, :]
        qz = q_ref[sl, :]
        q = qz * jax.nn.sigmoid(qz) * scale
        qa = q * jnp.exp2(a).astype(BF16)
        kd = k * jnp.exp2(a_last - a).astype(BF16)
        v = i_ref[sl, :]
        diag = jnp.sum(q.astype(F32) * k.astype(F32), axis=-1, keepdims=True)
        scores = jnp.where(row == col, diag, 0.0)
        for level in range(n_levels):
            if level == 0:
                ql, kl = q * f.astype(BF16), k
            else:
                decay = jnp.exp2(neg_abs(a - boundary(a, 1 << level))).astype(BF16)
                ql, kl = q * decay, k * decay
            gram = lax.dot_general(ql, kl, nt, preferred_element_type=F32)
            scores = jnp.where(differ >= (1 << level), gram, scores)
        p = scores.astype(BF16)
        o = lax.dot_general(qa, st.astype(BF16), nt, preferred_element_type=F32)
        o = o + jnp.dot(p, v, preferred_element_type=F32)
        v_t = v.astype(F32).T.astype(BF16)
        st_new = st * jnp.exp2(a_last) + jnp.dot(v_t, kd, preferred_element_type=F32)
        gz = g_ref[sl, :]
        gate = (gz * jax.nn.sigmoid(gz)).astype(F32)
        o_ref[sl, :] = (_rms(o, gain) * gate).astype(o_ref.dtype)
        return st_new

    st_ref[...] = lax.fori_loop(0, nchunks, body, st_ref[...], unroll=True)


def _hgrn(z3, lb, gain, tc):
    b, t, _ = z3.shape
    hb = HG_WIDTH // HEAD_DIM

    def zspec(k):
        return pl.BlockSpec((None, tc, HEAD_DIM), lambda bi, h, ti, k=k: (bi, ti, h + k * hb))

    return pl.pallas_call(
        functools.partial(_hgrn_kernel, nchunks=tc // CHUNK),
        out_shape=jax.ShapeDtypeStruct((b, t, HG_WIDTH), BF16),
        grid=(b, N_HEADS, t // tc),
        in_specs=[
            zspec(0), zspec(1), zspec(2), zspec(3),
            pl.BlockSpec((1, HEAD_DIM), lambda bi, h, ti: (0, h)),
            pl.BlockSpec((1, HEAD_DIM), lambda bi, h, ti: (0, 0)),
        ],
        out_specs=pl.BlockSpec((None, tc, HEAD_DIM), lambda bi, h, ti: (bi, ti, h)),
        scratch_shapes=[pltpu.VMEM((HEAD_DIM, HEAD_DIM), F32)],
        compiler_params=_params(("arbitrary", "arbitrary", "arbitrary")),
        name="hgrn",
    )(z3, z3, z3, z3, lb, gain)


SC_HALO = 8


def _short_conv(b_ref, c_ref, h_ref, w_ref, gain_ref, u_ref, new_sequence, tc):
    @pl.when(new_sequence)
    def _():
        u_ref[0:SC_HALO, :] = jnp.zeros((SC_HALO, SC_WIDTH), F32)

    u = c_ref[...].astype(F32) * h_ref[...].astype(F32)
    u_ref[SC_HALO:SC_HALO + tc, :] = u
    w = w_ref[...]
    y = (w[0:1, :] * u_ref[SC_HALO - 2:SC_HALO - 2 + tc, :]
         + w[1:2, :] * u_ref[SC_HALO - 1:SC_HALO - 1 + tc, :]
         + w[2:3, :] * u)
    out = _rms(b_ref[...].astype(F32) * y, gain_ref[...]).astype(BF16)
    u_ref[0:SC_HALO, :] = u_ref[tc:tc + SC_HALO, :]
    return out


R_E1, R_E2, R_W1, R_W2, R_RANK1, R_RANK2 = 0, 1, 2, 3, 4, 5
ROUTE_ROWS = 8


def _outproj_kernel(x_ref, ohg_ref, zb_ref, zc_ref, zh_ref, convw_ref, scgain_ref, wo1_ref, wo2_ref, g_ref,
                    wr_hl_ref, wr_h_ref, x1_ref, hn_ref, route_ref, route_t_ref, cnt_ref, carry_ref, u_ref,
                    *, tm, tiles_per_seq):
    @pl.when(pl.program_id(0) == 0)
    def _():
        carry_ref[...] = jnp.zeros_like(carry_ref)

    osc = _short_conv(zb_ref, zc_ref, zh_ref, convw_ref, scgain_ref, u_ref,
                      pl.program_id(0) % tiles_per_seq == 0, tm)
    acc = jnp.dot(ohg_ref[...], wo1_ref[...], preferred_element_type=F32)
    acc = acc + jnp.dot(osc, wo2_ref[...], preferred_element_type=F32)
    x1 = x_ref[...] + acc
    x1_ref[...] = x1
    hn = _rms(x1, g_ref[...])
    _pack_tokens(hn, hn_ref)

    hi = hn.astype(BF16)
    lo = (hn - hi.astype(F32)).astype(BF16)
    both = jnp.dot(hi, wr_hl_ref[...], preferred_element_type=F32)
    logits = both[:, :LANES] + both[:, LANES:] + jnp.dot(lo, wr_h_ref[...], preferred_element_type=F32)

    lt = logits.T
    sub = lax.broadcasted_iota(jnp.int32, (EXPERTS_PER_GROUP, tm), 0)

    def first_argmax(vals, vmax):
        return jnp.min(jnp.where(vals == vmax, sub, EXPERTS_PER_GROUP), axis=0, keepdims=True)

    is_group = sub < N_GROUPS
    gl = jnp.where(is_group, lt[0:EXPERTS_PER_GROUP], NEG_BIG)
    gmax = jnp.max(gl, axis=0, keepdims=True)
    gidx = first_argmax(gl, gmax)
    g_p = 1.0 / jnp.sum(jnp.where(is_group, jnp.exp(gl - gmax), 0.0), axis=0, keepdims=True)

    el = lt[EXPERT_LANE0:EXPERT_LANE0 + EXPERTS_PER_GROUP]
    for g in range(1, N_GROUPS):
        lo_row = EXPERT_LANE0 + g * EXPERTS_PER_GROUP
        el = jnp.where(gidx == g, lt[lo_row:lo_row + EXPERTS_PER_GROUP], el)
    m1 = jnp.max(el, axis=0, keepdims=True)
    i1 = first_argmax(el, m1)
    el2 = jnp.where(sub == i1, NEG_BIG, el)
    m2 = jnp.max(el2, axis=0, keepdims=True)
    i2 = first_argmax(el2, m2)
    r = jnp.exp(m2 - m1)
    w1 = g_p / (1.0 + r)
    w2 = g_p * r / (1.0 + r)
    e1 = gidx * EXPERTS_PER_GROUP + i1
    e2 = gidx * EXPERTS_PER_GROUP + i2

    erow = lax.broadcasted_iota(jnp.int32, (N_EXPERTS, tm), 0)
    oh1 = erow == e1
    oh2 = erow == e2
    onehot = (oh1 | oh2).astype(BF16)
    trow = lax.broadcasted_iota(jnp.int32, (tm, tm), 0)
    tcol = lax.broadcasted_iota(jnp.int32, (tm, tm), 1)
    before = jnp.dot(onehot, (trow < tcol).astype(BF16), preferred_element_type=F32) + carry_ref[...]
    rank1 = jnp.sum(jnp.where(oh1, before, 0.0), axis=0, keepdims=True)
    rank2 = jnp.sum(jnp.where(oh2, before, 0.0), axis=0, keepdims=True)
    carry_ref[...] = carry_ref[...] + jnp.sum(onehot.astype(F32), axis=1, keepdims=True)
    cnt_ref[...] = carry_ref[...]

    rows = [e1.astype(F32), e2.astype(F32), w1, w2, rank1, rank2]
    table_t = jnp.concatenate(rows + [jnp.zeros((LANES - len(rows), tm), F32)], axis=0)
    route_t_ref[...] = table_t[0:ROUTE_ROWS]
    route_ref[...] = table_t.T


def _outproj(x2d, ohg, z2d, conv_w, sc_gain, wo1, wo2, g_ffn, wr_hl, wr_h, tm, seq_len):
    n, d = x2d.shape
    const = lambda i: (0, 0)
    sc_first = 4 * HG_WIDTH // SC_WIDTH

    def zspec(k):
        return pl.BlockSpec((tm, SC_WIDTH), lambda i, k=k: (i, sc_first + k))

    return pl.pallas_call(
        functools.partial(_outproj_kernel, tm=tm, tiles_per_seq=seq_len // tm),
        out_shape=(
            jax.ShapeDtypeStruct((n, d), F32),
            jax.ShapeDtypeStruct((n * _token_rows(d), LANES), jnp.uint32),
            jax.ShapeDtypeStruct((n, LANES), F32),
            jax.ShapeDtypeStruct((ROUTE_ROWS, n), F32),
            jax.ShapeDtypeStruct((N_EXPERTS, 1), F32),
        ),
        grid=(n // tm,),
        in_specs=[
            pl.BlockSpec((tm, d), lambda i: (i, 0)),
            pl.BlockSpec((tm, HG_WIDTH), lambda i: (i, 0)),
            zspec(0), zspec(1), zspec(2),
            pl.BlockSpec((3, SC_WIDTH), const),
            pl.BlockSpec((1, SC_WIDTH), const),
            pl.BlockSpec((HG_WIDTH, d), const),
            pl.BlockSpec((SC_WIDTH, d), lambda i: (HG_WIDTH // SC_WIDTH, 0)),
            pl.BlockSpec((1, d), const),
            pl.BlockSpec((d, 2 * LANES), const),
            pl.BlockSpec((d, LANES), const),
        ],
        out_specs=(
            pl.BlockSpec((tm, d), lambda i: (i, 0)),
            pl.BlockSpec((tm * _token_rows(d), LANES), lambda i: (i, 0)),
            pl.BlockSpec((tm, LANES), lambda i: (i, 0)),
            pl.BlockSpec((ROUTE_ROWS, tm), lambda i: (0, i)),
            pl.BlockSpec((N_EXPERTS, 1), const),
        ),
        scratch_shapes=[pltpu.VMEM((N_EXPERTS, 1), F32), pltpu.VMEM((tm + SC_HALO, SC_WIDTH), F32)],
        compiler_params=_params(("arbitrary",)),
        name="outproj",
    )(x2d, ohg, z2d, z2d, z2d, conv_w, sc_gain, wo1, wo2, g_ffn, wr_hl, wr_h)


ZERO_ROWS = 512


def _dispatch_kernel(cnt_ref, pad_ref, start_ref, dest_hbm, hn_ref, xs_hbm, idx_ref, zero_ref, idx_sem, sem,
                     *, td, tr, nsteps):
    i = pl.program_id(0)
    idx_copy = pltpu.make_async_copy(dest_hbm.at[pl.ds(i * 2 * td, 2 * td)], idx_ref, idx_sem)
    idx_copy.start()
    idx_copy.wait()

    def token_copy(t, slot):
        src = hn_ref.at[pl.ds(pl.multiple_of(t * tr, tr), tr)]
        return pltpu.make_async_copy(src, xs_hbm.at[pl.ds(pl.multiple_of(slot * tr, tr), tr)], sem)

    def issue(t, carry):
        token_copy(t, idx_ref[2 * t]).start(priority=0)
        token_copy(t, idx_ref[2 * t + 1]).start(priority=1)
        return carry

    lax.fori_loop(0, td, issue, 0, unroll=8)
    tile_copy = pltpu.make_async_copy(hn_ref, xs_hbm.at[pl.ds(0, td * tr)], sem)
    tile_copy.wait()
    tile_copy.wait()

    @pl.when(i == nsteps - 1)
    def _():
        zero_ref[...] = jnp.zeros_like(zero_ref)
        zrows = zero_ref.shape[0] // tr

        def run_copy(pos, rows):
            dst = xs_hbm.at[pl.ds(pl.multiple_of(pos * tr, tr), rows * tr)]
            return pltpu.make_async_copy(zero_ref.at[pl.ds(0, rows * tr)], dst, sem)

        def per_run(e, carry):
            n = pad_ref[e] - cnt_ref[e]
            pos0 = start_ref[e] + cnt_ref[e]
            nfull = n // zrows
            rem = n - nfull * zrows
            bits = [1 << s for s in reversed(range(zrows.bit_length() - 1))]

            def start_full(j, c2):
                run_copy(pos0 + j * zrows, zrows).start()
                return c2

            def wait_full(j, c2):
                run_copy(0, zrows).wait()
                return c2

            lax.fori_loop(0, nfull, start_full, 0)
            pos = pos0 + nfull * zrows
            for bit in bits:
                @pl.when((rem & bit) != 0)
                def _(pos=pos, bit=bit):
                    run_copy(pos, bit).start()

                pos = pos + (rem & bit)
            lax.fori_loop(0, nfull, wait_full, 0)
            for bit in bits:
                @pl.when((rem & bit) != 0)
                def _(bit=bit):
                    run_copy(0, bit).wait()

            return carry

        lax.fori_loop(0, N_EXPERTS + 1, per_run, 0)


def _dispatch(counts, padded, pstart, dest_flat, hn_packed, n, tr, p_rows, td):
    nsteps = n // td
    grid_spec = pltpu.PrefetchScalarGridSpec(
        num_scalar_prefetch=3,
        grid=(nsteps,),
        in_specs=[
            pl.BlockSpec(memory_space=pl.ANY),
            pl.BlockSpec((td * tr, LANES), lambda i, *_: (i, 0)),
        ],
        out_specs=pl.BlockSpec(memory_space=pl.ANY),
        scratch_shapes=[
            pltpu.SMEM((2 * td,), jnp.int32),
            pltpu.VMEM((ZERO_ROWS * tr, LANES), jnp.uint32),
            pltpu.SemaphoreType.DMA,
            pltpu.SemaphoreType.DMA,
        ],
    )
    return pl.pallas_call(
        functools.partial(_dispatch_kernel, td=td, tr=tr, nsteps=nsteps),
        out_shape=jax.ShapeDtypeStruct((p_rows * tr, LANES), jnp.uint32),
        grid_spec=grid_spec,
        compiler_params=_params(("arbitrary",)),
        name="dispatch",
    )(counts, padded, pstart, dest_flat, hn_packed)


def _experts_kernel(blk_e_ref, first_ref, slot_ref, next_ref, nused_ref, xs_ref, wg_hbm, wu_hbm, wd_hbm, y_ref,
                    wg_buf, wu_buf, wd_buf, sem, *, tb, tr):
    i = pl.program_id(0)

    def fetch(e, slot):
        pairs = ((wg_hbm, wg_buf), (wu_hbm, wu_buf), (wd_hbm, wd_buf))
        return [pltpu.make_async_copy(src.at[e], buf.at[slot], sem.at[slot, j]) for j, (src, buf) in enumerate(pairs)]

    @pl.when(i >= nused_ref[0])
    def _():
        y_ref[...] = jnp.zeros_like(y_ref)

    @pl.when(i < nused_ref[0])
    def _():
        slot = slot_ref[i]

        @pl.when(first_ref[i] == 1)
        def _():
            @pl.when(i == 0)
            def _():
                for copy in fetch(blk_e_ref[0], slot):
                    copy.start()

            for copy in fetch(blk_e_ref[i], slot):
                copy.wait()

            @pl.when(next_ref[i] >= 0)
            def _():
                for copy in fetch(next_ref[i], 1 - slot):
                    copy.start()

        x = _unpack_tokens(lambda s: xs_ref[pl.ds(s, tb, stride=tr), :], tr)
        a = jnp.dot(x, wg_buf[slot], preferred_element_type=F32)
        b = jnp.dot(x, wu_buf[slot], preferred_element_type=F32)
        h = a * jax.nn.sigmoid(a) * b
        _pack_tokens(jnp.dot(h, wd_buf[slot], preferred_element_type=F32), y_ref)


def _experts(block_e, counts, nused, xs_packed, wg, wu, wd, tb):
    d, ff = wg.shape[1], wg.shape[2]
    tr = _token_rows(d)
    nblocks = xs_packed.shape[0] // (tb * tr)

    blk = jnp.arange(nblocks, dtype=jnp.int32)
    first = ((blk == 0) | (block_e != jnp.roll(block_e, 1))) & (blk < nused[0])
    slot = (jnp.cumsum(first.astype(jnp.int32)) - 1) % 2
    e_iota = jnp.arange(N_EXPERTS, dtype=jnp.int32)
    later_used = (e_iota[None, :] > e_iota[:, None]) & (counts > 0)[None, :]
    next_used = jnp.min(jnp.where(later_used, e_iota[None, :], N_EXPERTS), axis=1)
    next_used = jnp.where(next_used == N_EXPERTS, -1, next_used)
    next_e = jnp.sum(jnp.where(block_e[:, None] == e_iota[None, :], next_used[None, :], 0), axis=1)

    def row_map(i, *prefetch):
        return (jnp.minimum(i, prefetch[-1][0] - 1), 0)

    grid_spec = pltpu.PrefetchScalarGridSpec(
        num_scalar_prefetch=5,
        grid=(nblocks,),
        in_specs=[
            pl.BlockSpec((tb * tr, LANES), row_map),
            pl.BlockSpec(memory_space=pl.ANY),
            pl.BlockSpec(memory_space=pl.ANY),
            pl.BlockSpec(memory_space=pl.ANY),
        ],
        out_specs=pl.BlockSpec((tb * tr, LANES), lambda i, *prefetch: (i, 0)),
        scratch_shapes=[
            pltpu.VMEM((2, d, ff), F32),
            pltpu.VMEM((2, d, ff), F32),
            pltpu.VMEM((2, ff, d), F32),
            pltpu.SemaphoreType.DMA((2, 3)),
        ],
    )
    return pl.pallas_call(
        functools.partial(_experts_kernel, tb=tb, tr=tr),
        out_shape=jax.ShapeDtypeStruct(xs_packed.shape, jnp.uint32),
        grid_spec=grid_spec,
        compiler_params=_params(("arbitrary",)),
        name="experts",
    )(block_e, first.astype(jnp.int32), slot.astype(jnp.int32), next_e.astype(jnp.int32), nused,
      xs_packed, wg, wu, wd)


def _final_kernel(dest_hbm, route_ref, x1_ref, p_ref, wple_ref, gple_ref, wgate_ref, gfin_ref, y_hbm,
                  o_ref, idx_a, idx_b, ybuf_a, ybuf_b, idx_sem, sem, *, tf, tr, nsteps):
    i = pl.program_id(0)
    halves = ((idx_a, ybuf_a), (idx_b, ybuf_b))

    def idx_copy(tile, h):
        return pltpu.make_async_copy(dest_hbm.at[pl.ds(tile * 2 * tf, 2 * tf)], halves[h][0], idx_sem.at[h])

    def gather(h):
        idx_ref, ybuf_ref = halves[h]

        def issue(t, carry):
            for k in range(2):
                src = y_hbm.at[pl.ds(pl.multiple_of(idx_ref[2 * t + k] * tr, tr), tr)]
                dst = ybuf_ref.at[k, pl.ds(pl.multiple_of(t * tr, tr), tr)]
                pltpu.make_async_copy(src, dst, sem.at[h]).start(priority=k)
            return carry

        lax.fori_loop(0, tf, issue, 0, unroll=True)

    def wait_rows(h):
        for k in range(2):
            pltpu.make_async_copy(y_hbm.at[pl.ds(0, tf * tr)], halves[h][1].at[k], sem.at[h]).wait()

    def combine(h):
        ybuf_ref = halves[h][1]
        rows = pl.ds(h * tf, tf)
        wait_rows(h)
        route = route_ref[rows, :]
        w1 = route[:, R_W1:R_W1 + 1]
        w2 = route[:, R_W2:R_W2 + 1]
        y1 = _unpack_tokens(lambda s: ybuf_ref[0, pl.ds(s, tf, stride=tr), :], tr)
        y2 = _unpack_tokens(lambda s: ybuf_ref[1, pl.ds(s, tf, stride=tr), :], tr)
        return x1_ref[rows, :] + w1 * y1 + w2 * y2

    def finish(h, x2):
        rows = pl.ds(h * tf, tf)
        ple = _rms(jnp.dot(p_ref[rows, :], wple_ref[...], preferred_element_type=F32), gple_ref[...])
        gate = jax.nn.sigmoid(jnp.dot(x2, wgate_ref[...], preferred_element_type=F32))
        o_ref[rows, :] = _rms(x2 + gate * ple, gfin_ref[...])

    @pl.when(i == 0)
    def _():
        for h in range(2):
            idx_copy(h, h).start()
            idx_copy(h, h).wait()
            gather(h)

    for h in range(2):
        idx_copy(2 * i + 2 + h, h).start()
    for h in range(2):
        x2 = combine(h)
        idx_copy(2 * i + 2 + h, h).wait()
        gather(h)
        finish(h, x2)

    @pl.when(i == nsteps - 1)
    def _():
        for h in range(2):
            wait_rows(h)


def _final(dest_flat, route, x1, p2d, layer, wple, gple, wgate, gfin, y, tf):
    n, d = x1.shape
    pd = p2d.shape[1]
    tr = _token_rows(d)
    assert n % (2 * tf) == 0, (n, tf)
    nsteps = n // (2 * tf)
    p_block0 = layer * nsteps
    dest_padded = jnp.concatenate([dest_flat, jnp.zeros((4 * tf,), jnp.int32)])
    const = lambda i: (0, 0)
    return pl.pallas_call(
        functools.partial(_final_kernel, tf=tf, tr=tr, nsteps=nsteps),
        out_shape=jax.ShapeDtypeStruct((n, d), F32),
        grid=(nsteps,),
        in_specs=[
            pl.BlockSpec(memory_space=pl.ANY),
            pl.BlockSpec((2 * tf, LANES), lambda i: (i, 0)),
            pl.BlockSpec((2 * tf, d), lambda i: (i, 0)),
            pl.BlockSpec((2 * tf, pd), lambda i: (i + p_block0, 0)),
            pl.BlockSpec((pd, d), const, pipeline_mode=pl.Buffered(1)),
            pl.BlockSpec((1, d), const),
            pl.BlockSpec((d, d), const, pipeline_mode=pl.Buffered(1)),
            pl.BlockSpec((1, d), const),
            pl.BlockSpec(memory_space=pl.ANY),
        ],
        out_specs=pl.BlockSpec((2 * tf, d), lambda i: (i, 0)),
        scratch_shapes=[
            pltpu.SMEM((2 * tf,), jnp.int32),
            pltpu.SMEM((2 * tf,), jnp.int32),
            pltpu.VMEM((2, tf * tr, LANES), jnp.uint32),
            pltpu.VMEM((2, tf * tr, LANES), jnp.uint32),
            pltpu.SemaphoreType.DMA((2,)),
            pltpu.SemaphoreType.DMA((2,)),
        ],
        compiler_params=_params(("arbitrary",)),
        name="final",
    )(dest_padded, route, x1, p2d, wple, gple, wgate, gfin, y)


def _tile(n, want):
    t = min(n, want)
    assert n % t == 0, (n, want)
    return t


def kernel(x, p, g_mix, w_in, lb_logits, hg_norm, conv_w, sc_norm, w_out, g_ffn, w_router_group,
           w_router_expert, w_gate, w_up, w_down, w_ple, g_ple, w_ple_gate, g_final):
    b, t, d = x.shape
    n = b * t
    layer = 0
    x2d = x.reshape(n, d)

    lower_bounds = jnp.cumsum(jax.nn.softmax(lb_logits.astype(F32), axis=0), axis=0)
    lb = lower_bounds[layer].reshape(1, HG_WIDTH)

    z = _inproj(x2d, g_mix[layer].reshape(1, d), w_in[layer].astype(BF16), _tile(n, 1024), 1792)
    z3 = z.reshape(b, t, z.shape[1])
    ohg = _hgrn(z3, lb, hg_norm[layer].reshape(1, HEAD_DIM), _tile(t, 2048)).reshape(n, HG_WIDTH)

    wo = w_out[layer].astype(BF16)
    wr = jnp.concatenate([jnp.pad(w_router_group[layer], ((0, 0), (0, EXPERT_LANE0 - N_GROUPS))),
                          w_router_expert[layer]], axis=1).astype(F32)
    wr = jnp.pad(wr, ((0, 0), (0, LANES - wr.shape[1])))
    wr_hi = wr.astype(BF16)
    wr_lo = (wr - wr_hi.astype(F32)).astype(BF16)
    x1, hn, route, route_t, cnt = _outproj(
        x2d, ohg, z, conv_w[layer], sc_norm[layer].reshape(1, SC_WIDTH), wo, wo,
        g_ffn[layer].reshape(1, d), jnp.concatenate([wr_hi, wr_lo], axis=1), wr_hi, _tile(t, 512), t)

    tb = 512
    counts = cnt[:, 0].astype(jnp.int32)
    padded = (counts + tb - 1) // tb * tb
    pend = jnp.cumsum(padded)
    pstart = pend - padded
    e_ids = route_t[R_E1:R_E2 + 1].astype(jnp.int32)
    ranks = route_t[R_RANK1:R_RANK2 + 1].astype(jnp.int32)
    expert_iota = jnp.arange(N_EXPERTS, dtype=jnp.int32)[None, :, None]
    seg_start = jnp.sum(jnp.where(e_ids[:, None, :] == expert_iota, pstart.astype(jnp.int32)[None, :, None], 0), axis=1)
    dest_flat = (seg_start + ranks).T.reshape(2 * n)
    nblocks = -(-(2 * n + N_EXPERTS * (tb - 1)) // tb)
    nused = (pend[-1] // tb).astype(jnp.int32)
    blk = jnp.minimum(jnp.arange(nblocks, dtype=jnp.int32), nused - 1) * tb
    block_e = jnp.minimum(jnp.sum((blk[:, None] >= pend[None, :]).astype(jnp.int32), axis=1), N_EXPERTS - 1)

    p_rows = nblocks * tb
    zero = jnp.zeros((1,), jnp.int32)
    xs = _dispatch(jnp.concatenate([counts, zero]), jnp.concatenate([padded, p_rows - pend[-1:]]),
                   jnp.concatenate([pstart, pend[-1:]]).astype(jnp.int32), dest_flat, hn, n, _token_rows(d),
                   p_rows, _tile(n, 4096))
    y = _experts(block_e, counts, nused.reshape(1), xs, w_gate[layer], w_up[layer], w_down[layer], tb)

    out = _final(dest_flat, route, x1, p.reshape(-1, p.shape[-1]), layer, w_ple[layer],
                 g_ple[layer].reshape(1, d), w_ple_gate[layer], g_final.reshape(1, d), y,
                 _tile(n, 256))
    return out.reshape(b, t, d)
```

```python
import functools
from typing import NamedTuple

import jax
import jax.numpy as jnp
from jax import lax
from jax.experimental import pallas as pl
from jax.experimental.pallas import tpu as pltpu

F32 = jnp.float32
BF16 = jnp.bfloat16
EPS = 1e-6

HEAD_DIM = 128
N_HEADS = 8
HG_WIDTH = N_HEADS * HEAD_DIM
SC_WIDTH = 1024
CHUNK = 128
N_GROUPS = 4
EXPERTS_PER_GROUP = 8
N_EXPERTS = N_GROUPS * EXPERTS_PER_GROUP
LANES = 128
EXPERT_LANE0 = EXPERTS_PER_GROUP
NEG_BIG = -1e30
VMEM_LIMIT = 56 * 1024 * 1024


def _rms(v, gain):
    return v * lax.rsqrt(jnp.mean(v * v, axis=-1, keepdims=True) + EPS) * gain


def _token_rows(d):
    return d // (2 * LANES)


def _pack_tokens(v, out_ref):
    rows, d = v.shape
    tr = _token_rows(d)
    bits = lax.bitcast_convert_type(v.astype(BF16).astype(F32), jnp.uint32)
    words = bits[:, :d // 2] | (bits[:, d // 2:] >> 16)
    for s in range(tr):
        out_ref[pl.ds(s, rows, stride=tr), :] = words[:, s * LANES:(s + 1) * LANES]


def _unpack_tokens(load_rows, tr):
    words = [load_rows(s) for s in range(tr)]
    high = [lax.bitcast_convert_type(w & jnp.uint32(0xFFFF0000), F32) for w in words]
    low = [lax.bitcast_convert_type(w << 16, F32) for w in words]
    return jnp.concatenate(high + low, axis=1)


def _params(sem):
    return pltpu.CompilerParams(dimension_semantics=sem, vmem_limit_bytes=VMEM_LIMIT)


def _inproj_kernel(x_ref, g_ref, w_ref, z_ref, xn_ref):
    @pl.when(pl.program_id(1) == 0)
    def _():
        xn_ref[...] = _rms(x_ref[...], g_ref[...]).astype(BF16)

    z_ref[...] = jnp.dot(xn_ref[...], w_ref[...], preferred_element_type=F32).astype(z_ref.dtype)


def _inproj(x2d, g_mix, w_in_bf16, tm, tn):
    n, d = x2d.shape
    cols = w_in_bf16.shape[1]
    return pl.pallas_call(
        _inproj_kernel,
        out_shape=jax.ShapeDtypeStruct((n, cols), BF16),
        grid=(n // tm, cols // tn),
        in_specs=[
            pl.BlockSpec((tm, d), lambda i, j: (i, 0)),
            pl.BlockSpec((1, d), lambda i, j: (0, 0)),
            pl.BlockSpec((d, tn), lambda i, j: (0, j)),
        ],
        out_specs=pl.BlockSpec((tm, tn), lambda i, j: (i, j)),
        scratch_shapes=[pltpu.VMEM((tm, d), BF16)],
        compiler_params=_params(("arbitrary", "arbitrary")),
        name="inproj",
    )(x2d, g_mix, w_in_bf16)


def _hgrn_kernel(q_ref, f_ref, i_ref, g_ref, lb_ref, gain_ref, o_ref, st_ref, *, nchunks):
    @pl.when(pl.program_id(2) == 0)
    def _():
        st_ref[...] = jnp.zeros_like(st_ref)

    c_len = CHUNK
    n_levels = c_len.bit_length() - 1
    lb = lb_ref[...]
    one_minus_lb = 1.0 - lb
    gain = gain_ref[...]
    row = lax.broadcasted_iota(jnp.int32, (c_len, c_len), 0)
    col = lax.broadcasted_iota(jnp.int32, (c_len, c_len), 1)
    tri = (row >= col).astype(BF16)
    tri2 = jnp.concatenate([tri, tri], axis=1)
    differ = jnp.where(col < row, row ^ col, 0)
    sub8 = lax.broadcasted_iota(jnp.int32, (c_len // 8, 8, HEAD_DIM), 1)
    scale = HEAD_DIM ** -0.5
    nt = (((1,), (1,)), ((), ()))
    sign_bit = jnp.uint32(0x80000000)

    def neg_abs(d):
        return lax.bitcast_convert_type(lax.bitcast_convert_type(d, jnp.uint32) | sign_bit, F32)

    def boundary(a, half):
        if half >= 4:
            nb = c_len // (2 * half)
            a3 = a.reshape(nb, 2 * half, HEAD_DIM)
            return jnp.broadcast_to(a3[:, half - 1:half, :], a3.shape).reshape(c_len, HEAD_DIM)
        assert half == 2
        a3 = a.reshape(c_len // 8, 8, HEAD_DIM)
        return jnp.where(sub8 < 4, a3[:, 1:2, :], a3[:, 5:6, :]).reshape(c_len, HEAD_DIM)

    def body(c, st):
        sl = pl.ds(pl.multiple_of(c * c_len, c_len), c_len)
        fz = f_ref[sl, :].astype(F32)
        sg = jax.nn.sigmoid(fz)
        f = lb + one_minus_lb * sg
        log2_f = jnp.log2(f)
        k = (one_minus_lb * (1.0 - sg)).astype(BF16)
        h1 = log2_f.astype(BF16)
        h2 = (log2_f - h1.astype(F32)).astype(BF16)
        a = jnp.dot(tri2, jnp.concatenate([h1, h2], axis=0), preferred_element_type=F32)
        a_last = a[c_len - 1:c_len, :]
        qz = q_ref[sl, :]
        q = qz * jax.nn.sigmoid(qz) * scale
        qa = q * jnp.exp2(a).astype(BF16)
        kd = k * jnp.exp2(a_last - a).astype(BF16)
        v = i_ref[sl, :]
        diag = jnp.sum(q.astype(F32) * k.astype(F32), axis=-1, keepdims=True)
        scores = jnp.where(row == col, diag, 0.0)
        for level in range(n_levels):
            if level == 0:
                ql, kl = q * f.astype(BF16), k
            else:
                decay = jnp.exp2(neg_abs(a - boundary(a, 1 << level))).astype(BF16)
                ql, kl = q * decay, k * decay
            gram = lax.dot_general(ql, kl, nt, preferred_element_type=F32)
            scores = jnp.where(differ >= (1 << level), gram, scores)
        p = scores.astype(BF16)
        o = lax.dot_general(qa, st.astype(BF16), nt, preferred_element_type=F32)
        o = o + jnp.dot(p, v, preferred_element_type=F32)
        v_t = v.astype(F32).T.astype(BF16)
        st_new = st * jnp.exp2(a_last) + jnp.dot(v_t, kd, preferred_element_type=F32)
        gz = g_ref[sl, :]
        gate = (gz * jax.nn.sigmoid(gz)).astype(F32)
        o_ref[sl, :] = (_rms(o, gain) * gate).astype(o_ref.dtype)
        return st_new

    st_ref[...] = lax.fori_loop(0, nchunks, body, st_ref[...], unroll=True)


def _hgrn(z3, lb, gain, tc):
    b, t, _ = z3.shape
    hb = HG_WIDTH // HEAD_DIM

    def zspec(k):
        return pl.BlockSpec((None, tc, HEAD_DIM), lambda bi, h, ti, k=k: (bi, ti, h + k * hb))

    return pl.pallas_call(
        functools.partial(_hgrn_kernel, nchunks=tc // CHUNK),
        out_shape=jax.ShapeDtypeStruct((b, t, HG_WIDTH), BF16),
        grid=(b, N_HEADS, t // tc),
        in_specs=[
            zspec(0), zspec(1), zspec(2), zspec(3),
            pl.BlockSpec((1, HEAD_DIM), lambda bi, h, ti: (0, h)),
            pl.BlockSpec((1, HEAD_DIM), lambda bi, h, ti: (0, 0)),
        ],
        out_specs=pl.BlockSpec((None, tc, HEAD_DIM), lambda bi, h, ti: (bi, ti, h)),
        scratch_shapes=[pltpu.VMEM((HEAD_DIM, HEAD_DIM), F32)],
        compiler_params=_params(("arbitrary", "arbitrary", "arbitrary")),
        name="hgrn",
    )(z3, z3, z3, z3, lb, gain)


SC_HALO = 8


def _short_conv(b_ref, c_ref, h_ref, w_ref, gain_ref, u_ref, new_sequence, tc):
    @pl.when(new_sequence)
    def _():
        u_ref[0:SC_HALO, :] = jnp.zeros((SC_HALO, SC_WIDTH), F32)

    u = c_ref[...].astype(F32) * h_ref[...].astype(F32)
    u_ref[SC_HALO:SC_HALO + tc, :] = u
    w = w_ref[...]
    y = (w[0:1, :] * u_ref[SC_HALO - 2:SC_HALO - 2 + tc, :]
         + w[1:2, :] * u_ref[SC_HALO - 1:SC_HALO - 1 + tc, :]
         + w[2:3, :] * u)
    out = _rms(b_ref[...].astype(F32) * y, gain_ref[...]).astype(BF16)
    u_ref[0:SC_HALO, :] = u_ref[tc:tc + SC_HALO, :]
    return out


R_E1, R_E2, R_W1, R_W2, R_RANK1, R_RANK2 = 0, 1, 2, 3, 4, 5
ROUTE_ROWS = 8


def _outproj_kernel(x_ref, ohg_ref, zb_ref, zc_ref, zh_ref, convw_ref, scgain_ref, wo1_ref, wo2_ref, g_ref,
                    wr_hl_ref, wr_h_ref, x1_ref, hn_ref, route_ref, route_t_ref, cnt_ref, carry_ref, u_ref,
                    *, tm, tiles_per_seq):
    @pl.when(pl.program_id(0) == 0)
    def _():
        carry_ref[...] = jnp.zeros_like(carry_ref)

    osc = _short_conv(zb_ref, zc_ref, zh_ref, convw_ref, scgain_ref, u_ref,
                      pl.program_id(0) % tiles_per_seq == 0, tm)
    acc = jnp.dot(ohg_ref[...], wo1_ref[...], preferred_element_type=F32)
    acc = acc + jnp.dot(osc, wo2_ref[...], preferred_element_type=F32)
    x1 = x_ref[...] + acc
    x1_ref[...] = x1
    hn = _rms(x1, g_ref[...])
    _pack_tokens(hn, hn_ref)

    hi = hn.astype(BF16)
    lo = (hn - hi.astype(F32)).astype(BF16)
    both = jnp.dot(hi, wr_hl_ref[...], preferred_element_type=F32)
    logits = both[:, :LANES] + both[:, LANES:] + jnp.dot(lo, wr_h_ref[...], preferred_element_type=F32)

    lt = logits.T
    sub = lax.broadcasted_iota(jnp.int32, (EXPERTS_PER_GROUP, tm), 0)

    def first_argmax(vals, vmax):
        return jnp.min(jnp.where(vals == vmax, sub, EXPERTS_PER_GROUP), axis=0, keepdims=True)

    is_group = sub < N_GROUPS
    gl = jnp.where(is_group, lt[0:EXPERTS_PER_GROUP], NEG_BIG)
    gmax = jnp.max(gl, axis=0, keepdims=True)
    gidx = first_argmax(gl, gmax)
    g_p = 1.0 / jnp.sum(jnp.where(is_group, jnp.exp(gl - gmax), 0.0), axis=0, keepdims=True)

    el = lt[EXPERT_LANE0:EXPERT_LANE0 + EXPERTS_PER_GROUP]
    for g in range(1, N_GROUPS):
        lo_row = EXPERT_LANE0 + g * EXPERTS_PER_GROUP
        el = jnp.where(gidx == g, lt[lo_row:lo_row + EXPERTS_PER_GROUP], el)
    m1 = jnp.max(el, axis=0, keepdims=True)
    i1 = first_argmax(el, m1)
    el2 = jnp.where(sub == i1, NEG_BIG, el)
    m2 = jnp.max(el2, axis=0, keepdims=True)
    i2 = first_argmax(el2, m2)
    r = jnp.exp(m2 - m1)
    w1 = g_p / (1.0 + r)
    w2 = g_p * r / (1.0 + r)
    e1 = gidx * EXPERTS_PER_GROUP + i1
    e2 = gidx * EXPERTS_PER_GROUP + i2

    erow = lax.broadcasted_iota(jnp.int32, (N_EXPERTS, tm), 0)
    oh1 = erow == e1
    oh2 = erow == e2
    onehot = (oh1 | oh2).astype(BF16)
    trow = lax.broadcasted_iota(jnp.int32, (tm, tm), 0)
    tcol = lax.broadcasted_iota(jnp.int32, (tm, tm), 1)
    before = jnp.dot(onehot, (trow < tcol).astype(BF16), preferred_element_type=F32) + carry_ref[...]
    rank1 = jnp.sum(jnp.where(oh1, before, 0.0), axis=0, keepdims=True)
    rank2 = jnp.sum(jnp.where(oh2, before, 0.0), axis=0, keepdims=True)
    carry_ref[...] = carry_ref[...] + jnp.sum(onehot.astype(F32), axis=1, keepdims=True)
    cnt_ref[...] = carry_ref[...]

    rows = [e1.astype(F32), e2.astype(F32), w1, w2, rank1, rank2]
    table_t = jnp.concatenate(rows + [jnp.zeros((LANES - len(rows), tm), F32)], axis=0)
    route_t_ref[...] = table_t[0:ROUTE_ROWS]
    route_ref[...] = table_t.T


def _outproj(x2d, ohg, z2d, conv_w, sc_gain, wo1, wo2, g_ffn, wr_hl, wr_h, tm, seq_len):
    n, d = x2d.shape
    const = lambda i: (0, 0)
    sc_first = 4 * HG_WIDTH // SC_WIDTH

    def zspec(k):
        return pl.BlockSpec((tm, SC_WIDTH), lambda i, k=k: (i, sc_first + k))

    return pl.pallas_call(
        functools.partial(_outproj_kernel, tm=tm, tiles_per_seq=seq_len // tm),
        out_shape=(
            jax.ShapeDtypeStruct((n, d), F32),
            jax.ShapeDtypeStruct((n * _token_rows(d), LANES), jnp.uint32),
            jax.ShapeDtypeStruct((n, LANES), F32),
            jax.ShapeDtypeStruct((ROUTE_ROWS, n), F32),
            jax.ShapeDtypeStruct((N_EXPERTS, 1), F32),
        ),
        grid=(n // tm,),
        in_specs=[
            pl.BlockSpec((tm, d), lambda i: (i, 0)),
            pl.BlockSpec((tm, HG_WIDTH), lambda i: (i, 0)),
            zspec(0), zspec(1), zspec(2),
            pl.BlockSpec((3, SC_WIDTH), const),
            pl.BlockSpec((1, SC_WIDTH), const),
            pl.BlockSpec((HG_WIDTH, d), const),
            pl.BlockSpec((SC_WIDTH, d), lambda i: (HG_WIDTH // SC_WIDTH, 0)),
            pl.BlockSpec((1, d), const),
            pl.BlockSpec((d, 2 * LANES), const),
            pl.BlockSpec((d, LANES), const),
        ],
        out_specs=(
            pl.BlockSpec((tm, d), lambda i: (i, 0)),
            pl.BlockSpec((tm * _token_rows(d), LANES), lambda i: (i, 0)),
            pl.BlockSpec((tm, LANES), lambda i: (i, 0)),
            pl.BlockSpec((ROUTE_ROWS, tm), lambda i: (0, i)),
            pl.BlockSpec((N_EXPERTS, 1), const),
        ),
        scratch_shapes=[pltpu.VMEM((N_EXPERTS, 1), F32), pltpu.VMEM((tm + SC_HALO, SC_WIDTH), F32)],
        compiler_params=_params(("arbitrary",)),
        name="outproj",
    )(x2d, ohg, z2d, z2d, z2d, conv_w, sc_gain, wo1, wo2, g_ffn, wr_hl, wr_h)


ZERO_ROWS = 512


def _dispatch_kernel(cnt_ref, pad_ref, start_ref, dest_hbm, hn_ref, xs_hbm, idx_ref, zero_ref, idx_sem, sem,
                     *, td, tr, nsteps):
    i = pl.program_id(0)
    idx_copy = pltpu.make_async_copy(dest_hbm.at[pl.ds(i * 2 * td, 2 * td)], idx_ref, idx_sem)
    idx_copy.start()
    idx_copy.wait()

    def token_copy(t, slot):
        src = hn_ref.at[pl.ds(pl.multiple_of(t * tr, tr), tr)]
        return pltpu.make_async_copy(src, xs_hbm.at[pl.ds(pl.multiple_of(slot * tr, tr), tr)], sem)

    def issue(t, carry):
        token_copy(t, idx_ref[2 * t]).start(priority=0)
        token_copy(t, idx_ref[2 * t + 1]).start(priority=1)
        return carry

    lax.fori_loop(0, td, issue, 0, unroll=8)
    tile_copy = pltpu.make_async_copy(hn_ref, xs_hbm.at[pl.ds(0, td * tr)], sem)
    tile_copy.wait()
    tile_copy.wait()

    @pl.when(i == nsteps - 1)
    def _():
        zero_ref[...] = jnp.zeros_like(zero_ref)
        zrows = zero_ref.shape[0] // tr

        def run_copy(pos, rows):
            dst = xs_hbm.at[pl.ds(pl.multiple_of(pos * tr, tr), rows * tr)]
            return pltpu.make_async_copy(zero_ref.at[pl.ds(0, rows * tr)], dst, sem)

        def per_run(e, carry):
            n = pad_ref[e] - cnt_ref[e]
            pos0 = start_ref[e] + cnt_ref[e]
            nfull = n // zrows
            rem = n - nfull * zrows
            bits = [1 << s for s in reversed(range(zrows.bit_length() - 1))]

            def start_full(j, c2):
                run_copy(pos0 + j * zrows, zrows).start()
                return c2

            def wait_full(j, c2):
                run_copy(0, zrows).wait()
                return c2

            lax.fori_loop(0, nfull, start_full, 0)
            pos = pos0 + nfull * zrows
            for bit in bits:
                @pl.when((rem & bit) != 0)
                def _(pos=pos, bit=bit):
                    run_copy(pos, bit).start()

                pos = pos + (rem & bit)
            lax.fori_loop(0, nfull, wait_full, 0)
            for bit in bits:
                @pl.when((rem & bit) != 0)
                def _(bit=bit):
                    run_copy(0, bit).wait()

            return carry

        lax.fori_loop(0, N_EXPERTS + 1, per_run, 0)


def _dispatch(counts, padded, pstart, dest_flat, hn_packed, n, tr, p_rows, td):
    nsteps = n // td
    grid_spec = pltpu.PrefetchScalarGridSpec(
        num_scalar_prefetch=3,
        grid=(nsteps,),
        in_specs=[
            pl.BlockSpec(memory_space=pl.ANY),
            pl.BlockSpec((td * tr, LANES), lambda i, *_: (i, 0)),
        ],
        out_specs=pl.BlockSpec(memory_space=pl.ANY),
        scratch_shapes=[
            pltpu.SMEM((2 * td,), jnp.int32),
            pltpu.VMEM((ZERO_ROWS * tr, LANES), jnp.uint32),
            pltpu.SemaphoreType.DMA,
            pltpu.SemaphoreType.DMA,
        ],
    )
    return pl.pallas_call(
        functools.partial(_dispatch_kernel, td=td, tr=tr, nsteps=nsteps),
        out_shape=jax.ShapeDtypeStruct((p_rows * tr, LANES), jnp.uint32),
        grid_spec=grid_spec,
        compiler_params=_params(("arbitrary",)),
        name="dispatch",
    )(counts, padded, pstart, dest_flat, hn_packed)


def _experts_kernel(blk_e_ref, first_ref, slot_ref, next_ref, nused_ref, xs_ref, wg_hbm, wu_hbm, wd_hbm, y_ref,
                    wg_buf, wu_buf, wd_buf, sem, *, tb, tr):
    i = pl.program_id(0)

    def fetch(e, slot):
        pairs = ((wg_hbm, wg_buf), (wu_hbm, wu_buf), (wd_hbm, wd_buf))
        return [pltpu.make_async_copy(src.at[e], buf.at[slot], sem.at[slot, j]) for j, (src, buf) in enumerate(pairs)]

    @pl.when(i >= nused_ref[0])
    def _():
        y_ref[...] = jnp.zeros_like(y_ref)

    @pl.when(i < nused_ref[0])
    def _():
        slot = slot_ref[i]

        @pl.when(first_ref[i] == 1)
        def _():
            @pl.when(i == 0)
            def _():
                for copy in fetch(blk_e_ref[0], slot):
                    copy.start()

            for copy in fetch(blk_e_ref[i], slot):
                copy.wait()

            @pl.when(next_ref[i] >= 0)
            def _():
                for copy in fetch(next_ref[i], 1 - slot):
                    copy.start()

        x = _unpack_tokens(lambda s: xs_ref[pl.ds(s, tb, stride=tr), :], tr)
        a = jnp.dot(x, wg_buf[slot], preferred_element_type=F32)
        b = jnp.dot(x, wu_buf[slot], preferred_element_type=F32)
        h = a * jax.nn.sigmoid(a) * b
        _pack_tokens(jnp.dot(h, wd_buf[slot], preferred_element_type=F32), y_ref)


def _experts(block_e, counts, nused, xs_packed, wg, wu, wd, tb):
    d, ff = wg.shape[1], wg.shape[2]
    tr = _token_rows(d)
    nblocks = xs_packed.shape[0] // (tb * tr)

    blk = jnp.arange(nblocks, dtype=jnp.int32)
    first = ((blk == 0) | (block_e != jnp.roll(block_e, 1))) & (blk < nused[0])
    slot = (jnp.cumsum(first.astype(jnp.int32)) - 1) % 2
    e_iota = jnp.arange(N_EXPERTS, dtype=jnp.int32)
    later_used = (e_iota[None, :] > e_iota[:, None]) & (counts > 0)[None, :]
    next_used = jnp.min(jnp.where(later_used, e_iota[None, :], N_EXPERTS), axis=1)
    next_used = jnp.where(next_used == N_EXPERTS, -1, next_used)
    next_e = jnp.sum(jnp.where(block_e[:, None] == e_iota[None, :], next_used[None, :], 0), axis=1)

    def row_map(i, *prefetch):
        return (jnp.minimum(i, prefetch[-1][0] - 1), 0)

    grid_spec = pltpu.PrefetchScalarGridSpec(
        num_scalar_prefetch=5,
        grid=(nblocks,),
        in_specs=[
            pl.BlockSpec((tb * tr, LANES), row_map),
            pl.BlockSpec(memory_space=pl.ANY),
            pl.BlockSpec(memory_space=pl.ANY),
            pl.BlockSpec(memory_space=pl.ANY),
        ],
        out_specs=pl.BlockSpec((tb * tr, LANES), lambda i, *prefetch: (i, 0)),
        scratch_shapes=[
            pltpu.VMEM((2, d, ff), F32),
            pltpu.VMEM((2, d, ff), F32),
            pltpu.VMEM((2, ff, d), F32),
            pltpu.SemaphoreType.DMA((2, 3)),
        ],
    )
    return pl.pallas_call(
        functools.partial(_experts_kernel, tb=tb, tr=tr),
        out_shape=jax.ShapeDtypeStruct(xs_packed.shape, jnp.uint32),
        grid_spec=grid_spec,
        compiler_params=_params(("arbitrary",)),
        name="experts",
    )(block_e, first.astype(jnp.int32), slot.astype(jnp.int32), next_e.astype(jnp.int32), nused,
      xs_packed, wg, wu, wd)


def _final_kernel(dest_hbm, route_ref, x1_ref, p_ref, wple_ref, gple_ref, wgate_ref, gfin_ref, y_hbm,
                  o_ref, idx_a, idx_b, ybuf_a, ybuf_b, idx_sem, sem, *, tf, tr, nsteps):
    i = pl.program_id(0)
    halves = ((idx_a, ybuf_a), (idx_b, ybuf_b))

    def idx_copy(tile, h):
        return pltpu.make_async_copy(dest_hbm.at[pl.ds(tile * 2 * tf, 2 * tf)], halves[h][0], idx_sem.at[h])

    def gather(h):
        idx_ref, ybuf_ref = halves[h]

        def issue(t, carry):
            for k in range(2):
                src = y_hbm.at[pl.ds(pl.multiple_of(idx_ref[2 * t + k] * tr, tr), tr)]
                dst = ybuf_ref.at[k, pl.ds(pl.multiple_of(t * tr, tr), tr)]
                pltpu.make_async_copy(src, dst, sem.at[h]).start(priority=k)
            return carry

        lax.fori_loop(0, tf, issue, 0, unroll=True)

    def wait_rows(h):
        for k in range(2):
            pltpu.make_async_copy(y_hbm.at[pl.ds(0, tf * tr)], halves[h][1].at[k], sem.at[h]).wait()

    def combine(h):
        ybuf_ref = halves[h][1]
        rows = pl.ds(h * tf, tf)
        wait_rows(h)
        route = route_ref[rows, :]
        w1 = route[:, R_W1:R_W1 + 1]
        w2 = route[:, R_W2:R_W2 + 1]
        y1 = _unpack_tokens(lambda s: ybuf_ref[0, pl.ds(s, tf, stride=tr), :], tr)
        y2 = _unpack_tokens(lambda s: ybuf_ref[1, pl.ds(s, tf, stride=tr), :], tr)
        return x1_ref[rows, :] + w1 * y1 + w2 * y2

    def finish(h, x2):
        rows = pl.ds(h * tf, tf)
        ple = _rms(jnp.dot(p_ref[rows, :], wple_ref[...], preferred_element_type=F32), gple_ref[...])
        gate = jax.nn.sigmoid(jnp.dot(x2, wgate_ref[...], preferred_element_type=F32))
        o_ref[rows, :] = _rms(x2 + gate * ple, gfin_ref[...])

    @pl.when(i == 0)
    def _():
        for h in range(2):
            idx_copy(h, h).start()
            idx_copy(h, h).wait()
            gather(h)

    for h in range(2):
        idx_copy(2 * i + 2 + h, h).start()
    for h in range(2):
        x2 = combine(h)
        idx_copy(2 * i + 2 + h, h).wait()
        gather(h)
        finish(h, x2)

    @pl.when(i == nsteps - 1)
    def _():
        for h in range(2):
            wait_rows(h)


def _final(dest_flat, route, x1, p2d, layer, wple, gple, wgate, gfin, y, tf):
    n, d = x1.shape
    pd = p2d.shape[1]
    tr = _token_rows(d)
    assert n % (2 * tf) == 0, (n, tf)
    nsteps = n // (2 * tf)
    p_block0 = layer * nsteps
    dest_padded = jnp.concatenate([dest_flat, jnp.zeros((4 * tf,), jnp.int32)])
    const = lambda i: (0, 0)
    return pl.pallas_call(
        functools.partial(_final_kernel, tf=tf, tr=tr, nsteps=nsteps),
        out_shape=jax.ShapeDtypeStruct((n, d), F32),
        grid=(nsteps,),
        in_specs=[
            pl.BlockSpec(memory_space=pl.ANY),
            pl.BlockSpec((2 * tf, LANES), lambda i: (i, 0)),
            pl.BlockSpec((2 * tf, d), lambda i: (i, 0)),
            pl.BlockSpec((2 * tf, pd), lambda i: (i + p_block0, 0)),
            pl.BlockSpec((pd, d), const, pipeline_mode=pl.Buffered(1)),
            pl.BlockSpec((1, d), const),
            pl.BlockSpec((d, d), const, pipeline_mode=pl.Buffered(1)),
            pl.BlockSpec((1, d), const),
            pl.BlockSpec(memory_space=pl.ANY),
        ],
        out_specs=pl.BlockSpec((2 * tf, d), lambda i: (i, 0)),
        scratch_shapes=[
            pltpu.SMEM((2 * tf,), jnp.int32),
            pltpu.SMEM((2 * tf,), jnp.int32),
            pltpu.VMEM((2, tf * tr, LANES), jnp.uint32),
            pltpu.VMEM((2, tf * tr, LANES), jnp.uint32),
            pltpu.SemaphoreType.DMA((2,)),
            pltpu.SemaphoreType.DMA((2,)),
        ],
        compiler_params=_params(("arbitrary",)),
        name="final",
    )(dest_padded, route, x1, p2d, wple, gple, wgate, gfin, y)


def _tile(n, want):
    t = min(n, want)
    assert n % t == 0, (n, want)
    return t


class _Tiles(NamedTuple):
    inproj_rows: int
    inproj_cols: int
    hgrn_rows: int
    outproj_rows: int
    dispatch_rows: int
    expert_rows: int
    final_rows: int


def _tiles(b, t):
    n = b * t
    return _Tiles(inproj_rows=_tile(n, 1024), inproj_cols=1792, hgrn_rows=_tile(t, 4096),
                  outproj_rows=_tile(t, 512), dispatch_rows=_tile(n, 4096), expert_rows=512,
                  final_rows=_tile(n, 256))


def kernel(x, p, g_mix, w_in, lb_logits, hg_norm, conv_w, sc_norm, w_out, g_ffn, w_router_group,
           w_router_expert, w_gate, w_up, w_down, w_ple, g_ple, w_ple_gate, g_final):
    b, t, d = x.shape
    n = b * t
    layer = 0
    tiles = _tiles(b, t)
    assert w_in.shape[2] == 4 * HG_WIDTH + 3 * SC_WIDTH and w_out.shape[1] == HG_WIDTH + SC_WIDTH
    x2d = x.reshape(n, d)

    lower_bounds = jnp.cumsum(jax.nn.softmax(lb_logits.astype(F32), axis=0), axis=0)
    lb = lower_bounds[layer].reshape(1, HG_WIDTH)

    z = _inproj(x2d, g_mix[layer].reshape(1, d), w_in[layer].astype(BF16), tiles.inproj_rows, tiles.inproj_cols)
    z3 = z.reshape(b, t, z.shape[1])
    ohg = _hgrn(z3, lb, hg_norm[layer].reshape(1, HEAD_DIM), tiles.hgrn_rows).reshape(n, HG_WIDTH)

    wo = w_out[layer].astype(BF16)
    wr = jnp.concatenate([jnp.pad(w_router_group[layer], ((0, 0), (0, EXPERT_LANE0 - N_GROUPS))),
                          w_router_expert[layer]], axis=1).astype(F32)
    wr = jnp.pad(wr, ((0, 0), (0, LANES - wr.shape[1])))
    wr_hi = wr.astype(BF16)
    wr_lo = (wr - wr_hi.astype(F32)).astype(BF16)
    x1, hn, route, route_t, cnt = _outproj(
        x2d, ohg, z, conv_w[layer], sc_norm[layer].reshape(1, SC_WIDTH), wo, wo,
        g_ffn[layer].reshape(1, d), jnp.concatenate([wr_hi, wr_lo], axis=1), wr_hi, tiles.outproj_rows, t)

    tb = tiles.expert_rows
    counts = cnt[:, 0].astype(jnp.int32)
    padded = (counts + tb - 1) // tb * tb
    pend = jnp.cumsum(padded)
    pstart = pend - padded
    e_ids = route_t[R_E1:R_E2 + 1].astype(jnp.int32)
    ranks = route_t[R_RANK1:R_RANK2 + 1].astype(jnp.int32)
    expert_iota = jnp.arange(N_EXPERTS, dtype=jnp.int32)[None, :, None]
    seg_start = jnp.sum(jnp.where(e_ids[:, None, :] == expert_iota, pstart.astype(jnp.int32)[None, :, None], 0), axis=1)
    dest_flat = (seg_start + ranks).T.reshape(2 * n)
    nblocks = -(-(2 * n + N_EXPERTS * (tb - 1)) // tb)
    nused = (pend[-1] // tb).astype(jnp.int32)
    blk = jnp.minimum(jnp.arange(nblocks, dtype=jnp.int32), nused - 1) * tb
    block_e = jnp.minimum(jnp.sum((blk[:, None] >= pend[None, :]).astype(jnp.int32), axis=1), N_EXPERTS - 1)

    p_rows = nblocks * tb
    zero = jnp.zeros((1,), jnp.int32)
    xs = _dispatch(jnp.concatenate([counts, zero]), jnp.concatenate([padded, p_rows - pend[-1:]]),
                   jnp.concatenate([pstart, pend[-1:]]).astype(jnp.int32), dest_flat, hn, n, _token_rows(d),
                   p_rows, tiles.dispatch_rows)
    y = _experts(block_e, counts, nused.reshape(1), xs, w_gate[layer], w_up[layer], w_down[layer], tb)

    out = _final(dest_flat, route, x1, p.reshape(-1, p.shape[-1]), layer, w_ple[layer],
                 g_ple[layer].reshape(1, d), w_ple_gate[layer], g_final.reshape(1, d), y,
                 tiles.final_rows)
    return out.reshape(b, t, d)
```

```python
import functools
from typing import NamedTuple

import jax
import jax.numpy as jnp
from jax import lax
from jax.experimental import pallas as pl
from jax.experimental.pallas import tpu as pltpu

F32 = jnp.float32
BF16 = jnp.bfloat16
EPS = 1e-6

HEAD_DIM = 128
N_HEADS = 8
HG_WIDTH = N_HEADS * HEAD_DIM
SC_WIDTH = 1024
CHUNK = 128
N_GROUPS = 4
EXPERTS_PER_GROUP = 8
N_EXPERTS = N_GROUPS * EXPERTS_PER_GROUP
LANES = 128
EXPERT_LANE0 = EXPERTS_PER_GROUP
NEG_BIG = -1e30
VMEM_LIMIT = 56 * 1024 * 1024


def _rms(v, gain):
    return v * lax.rsqrt(jnp.mean(v * v, axis=-1, keepdims=True) + EPS) * gain


def _token_rows(d):
    return d // (2 * LANES)


def _pack_tokens(v, out_ref):
    rows, d = v.shape
    tr = _token_rows(d)
    bits = lax.bitcast_convert_type(v.astype(BF16).astype(F32), jnp.uint32)
    words = bits[:, :d // 2] | (bits[:, d // 2:] >> 16)
    for s in range(tr):
        out_ref[pl.ds(s, rows, stride=tr), :] = words[:, s * LANES:(s + 1) * LANES]


def _unpack_tokens(load_rows, tr):
    words = [load_rows(s) for s in range(tr)]
    high = [lax.bitcast_convert_type(w & jnp.uint32(0xFFFF0000), F32) for w in words]
    low = [lax.bitcast_convert_type(w << 16, F32) for w in words]
    return jnp.concatenate(high + low, axis=1)


def _params(sem):
    return pltpu.CompilerParams(dimension_semantics=sem, vmem_limit_bytes=VMEM_LIMIT)


def _inproj_kernel(x_ref, g_ref, w_ref, z_ref, xn_ref):
    @pl.when(pl.program_id(1) == 0)
    def _():
        xn_ref[...] = _rms(x_ref[...], g_ref[...]).astype(BF16)

    z_ref[...] = jnp.dot(xn_ref[...], w_ref[...], preferred_element_type=F32).astype(z_ref.dtype)


def _inproj(x2d, g_mix, w_in_bf16, tm, tn):
    n, d = x2d.shape
    cols = w_in_bf16.shape[1]
    return pl.pallas_call(
        _inproj_kernel,
        out_shape=jax.ShapeDtypeStruct((n, cols), BF16),
        grid=(n // tm, cols // tn),
        in_specs=[
            pl.BlockSpec((tm, d), lambda i, j: (i, 0)),
            pl.BlockSpec((1, d), lambda i, j: (0, 0)),
            pl.BlockSpec((d, tn), lambda i, j: (0, j)),
        ],
        out_specs=pl.BlockSpec((tm, tn), lambda i, j: (i, j)),
        scratch_shapes=[pltpu.VMEM((tm, d), BF16)],
        compiler_params=_params(("arbitrary", "arbitrary")),
        name="inproj",
    )(x2d, g_mix, w_in_bf16)


def _hgrn_kernel(q_ref, f_ref, i_ref, g_ref, lb_ref, gain_ref, o_ref, st_ref, *, nchunks):
    @pl.when(pl.program_id(2) == 0)
    def _():
        st_ref[...] = jnp.zeros_like(st_ref)

    c_len = CHUNK
    n_levels = c_len.bit_length() - 1
    lb = lb_ref[...]
    one_minus_lb = 1.0 - lb
    gain = gain_ref[...]
    row = lax.broadcasted_iota(jnp.int32, (c_len, c_len), 0)
    col = lax.broadcasted_iota(jnp.int32, (c_len, c_len), 1)
    tri = (row >= col).astype(BF16)
    tri2 = jnp.concatenate([tri, tri], axis=1)
    differ = jnp.where(col < row, row ^ col, 0)
    sub8 = lax.broadcasted_iota(jnp.int32, (c_len // 8, 8, HEAD_DIM), 1)
    scale = HEAD_DIM ** -0.5
    nt = (((1,), (1,)), ((), ()))
    sign_bit = jnp.uint32(0x80000000)

    def neg_abs(d):
        return lax.bitcast_convert_type(lax.bitcast_convert_type(d, jnp.uint32) | sign_bit, F32)

    def boundary(a, half):
        if half >= 4:
            nb = c_len // (2 * half)
            a3 = a.reshape(nb, 2 * half, HEAD_DIM)
            return jnp.broadcast_to(a3[:, half - 1:half, :], a3.shape).reshape(c_len, HEAD_DIM)
        assert half == 2
        a3 = a.reshape(c_len // 8, 8, HEAD_DIM)
        return jnp.where(sub8 < 4, a3[:, 1:2, :], a3[:, 5:6, :]).reshape(c_len, HEAD_DIM)

    def body(c, st):
        sl = pl.ds(pl.multiple_of(c * c_len, c_len), c_len)
        fz = f_ref[sl, :].astype(F32)
        sg = jax.nn.sigmoid(fz)
        f = lb + one_minus_lb * sg
        log2_f = jnp.log2(f)
        k = (one_minus_lb * (1.0 - sg)).astype(BF16)
        h1 = log2_f.astype(BF16)
        h2 = (log2_f - h1.astype(F32)).astype(BF16)
        a = jnp.dot(tri2, jnp.concatenate([h1, h2], axis=0), preferred_element_type=F32)
        a_last = a[c_len - 1:c_len, :]
        qz = q_ref[sl, :]
        q = qz * jax.nn.sigmoid(qz) * scale
        qa = q * jnp.exp2(a).astype(BF16)
        kd = k * jnp.exp2(a_last - a).astype(BF16)
        v = i_ref[sl, :]
        diag = jnp.sum(q.astype(F32) * k.astype(F32), axis=-1, keepdims=True)
        scores = jnp.where(row == col, diag, 0.0)
        for level in range(n_levels):
            if level == 0:
                ql, kl = q * f.astype(BF16), k
            else:
                decay = jnp.exp2(neg_abs(a - boundary(a, 1 << level))).astype(BF16)
                ql, kl = q * decay, k * decay
            gram = lax.dot_general(ql, kl, nt, preferred_element_type=F32)
            scores = jnp.where(differ >= (1 << level), gram, scores)
        p = scores.astype(BF16)
        o = lax.dot_general(qa, st.astype(BF16), nt, preferred_element_type=F32)
        o = o + jnp.dot(p, v, preferred_element_type=F32)
        v_t = v.astype(F32).T.astype(BF16)
        st_new = st * jnp.exp2(a_last) + jnp.dot(v_t, kd, preferred_element_type=F32)
        gz = g_ref[sl, :]
        gate = (gz * jax.nn.sigmoid(gz)).astype(F32)
        o_ref[sl, :] = (_rms(o, gain) * gate).astype(o_ref.dtype)
        return st_new

    st_ref[...] = lax.fori_loop(0, nchunks, body, st_ref[...], unroll=True)


def _hgrn(z3, lb, gain, tc):
    b, t, _ = z3.shape
    hb = HG_WIDTH // HEAD_DIM

    def zspec(k):
        return pl.BlockSpec((None, tc, HEAD_DIM), lambda bi, h, ti, k=k: (bi, ti, h + k * hb))

    return pl.pallas_call(
        functools.partial(_hgrn_kernel, nchunks=tc // CHUNK),
        out_shape=jax.ShapeDtypeStruct((b, t, HG_WIDTH), BF16),
        grid=(b, N_HEADS, t // tc),
        in_specs=[
            zspec(0), zspec(1), zspec(2), zspec(3),
            pl.BlockSpec((1, HEAD_DIM), lambda bi, h, ti: (0, h)),
            pl.BlockSpec((1, HEAD_DIM), lambda bi, h, ti: (0, 0)),
        ],
        out_specs=pl.BlockSpec((None, tc, HEAD_DIM), lambda bi, h, ti: (bi, ti, h)),
        scratch_shapes=[pltpu.VMEM((HEAD_DIM, HEAD_DIM), F32)],
        compiler_params=_params(("arbitrary", "arbitrary", "arbitrary")),
        name="hgrn",
    )(z3, z3, z3, z3, lb, gain)


SC_HALO = 8


def _short_conv(b_ref, c_ref, h_ref, w_ref, gain_ref, u_ref, new_sequence, tc):
    @pl.when(new_sequence)
    def _():
        u_ref[...] = jnp.zeros_like(u_ref)

    u = c_ref[...].astype(F32) * h_ref[...].astype(F32)
    prev = u_ref[...]
    sub = lax.broadcasted_iota(jnp.int32, (SC_HALO, SC_WIDTH), 0)

    def delayed(k):
        rolled = pltpu.roll(u, k, 0)
        head = jnp.where(sub < k, pltpu.roll(prev, k, 0), rolled[0:SC_HALO])
        return jnp.concatenate([head, rolled[SC_HALO:]], axis=0)

    w = w_ref[...]
    y = w[0:1, :] * delayed(2) + w[1:2, :] * delayed(1) + w[2:3, :] * u
    out = _rms(b_ref[...].astype(F32) * y, gain_ref[...]).astype(BF16)
    u_ref[...] = u[tc - SC_HALO:tc]
    return out


R_E1, R_E2, R_W1, R_W2, R_RANK1, R_RANK2 = 0, 1, 2, 3, 4, 5
ROUTE_ROWS = 8


def _outproj_kernel(x_ref, ohg_ref, zb_ref, zc_ref, zh_ref, convw_ref, scgain_ref, wo1_ref, wo2_ref, g_ref,
                    wr_hl_ref, wr_h_ref, x1_ref, hn_ref, route_ref, route_t_ref, cnt_ref, carry_ref, u_ref,
                    *, tm, tiles_per_seq):
    @pl.when(pl.program_id(0) == 0)
    def _():
        carry_ref[...] = jnp.zeros_like(carry_ref)

    osc = _short_conv(zb_ref, zc_ref, zh_ref, convw_ref, scgain_ref, u_ref,
                      pl.program_id(0) % tiles_per_seq == 0, tm)
    acc = jnp.dot(ohg_ref[...], wo1_ref[...], preferred_element_type=F32)
    acc = acc + jnp.dot(osc, wo2_ref[...], preferred_element_type=F32)
    x1 = x_ref[...] + acc
    x1_ref[...] = x1
    hn = _rms(x1, g_ref[...])
    _pack_tokens(hn, hn_ref)

    hi = hn.astype(BF16)
    lo = (hn - hi.astype(F32)).astype(BF16)
    both = jnp.dot(hi, wr_hl_ref[...], preferred_element_type=F32)
    logits = both[:, :LANES] + both[:, LANES:] + jnp.dot(lo, wr_h_ref[...], preferred_element_type=F32)

    lt = logits.T
    sub = lax.broadcasted_iota(jnp.int32, (EXPERTS_PER_GROUP, tm), 0)

    def first_argmax(vals, vmax):
        return jnp.min(jnp.where(vals == vmax, sub, EXPERTS_PER_GROUP), axis=0, keepdims=True)

    is_group = sub < N_GROUPS
    gl = jnp.where(is_group, lt[0:EXPERTS_PER_GROUP], NEG_BIG)
    gmax = jnp.max(gl, axis=0, keepdims=True)
    gidx = first_argmax(gl, gmax)
    g_p = 1.0 / jnp.sum(jnp.where(is_group, jnp.exp(gl - gmax), 0.0), axis=0, keepdims=True)

    el = lt[EXPERT_LANE0:EXPERT_LANE0 + EXPERTS_PER_GROUP]
    for g in range(1, N_GROUPS):
        lo_row = EXPERT_LANE0 + g * EXPERTS_PER_GROUP
        el = jnp.where(gidx == g, lt[lo_row:lo_row + EXPERTS_PER_GROUP], el)
    m1 = jnp.max(el, axis=0, keepdims=True)
    i1 = first_argmax(el, m1)
    el2 = jnp.where(sub == i1, NEG_BIG, el)
    m2 = jnp.max(el2, axis=0, keepdims=True)
    i2 = first_argmax(el2, m2)
    r = jnp.exp(m2 - m1)
    w1 = g_p / (1.0 + r)
    w2 = g_p * r / (1.0 + r)
    e1 = gidx * EXPERTS_PER_GROUP + i1
    e2 = gidx * EXPERTS_PER_GROUP + i2

    erow = lax.broadcasted_iota(jnp.int32, (N_EXPERTS, tm), 0)
    oh1 = erow == e1
    oh2 = erow == e2
    onehot = (oh1 | oh2).astype(BF16)
    trow = lax.broadcasted_iota(jnp.int32, (tm, tm), 0)
    tcol = lax.broadcasted_iota(jnp.int32, (tm, tm), 1)
    before = jnp.dot(onehot, (trow < tcol).astype(BF16), preferred_element_type=F32) + carry_ref[...]
    rank1 = jnp.sum(jnp.where(oh1, before, 0.0), axis=0, keepdims=True)
    rank2 = jnp.sum(jnp.where(oh2, before, 0.0), axis=0, keepdims=True)
    carry_ref[...] = carry_ref[...] + jnp.sum(onehot.astype(F32), axis=1, keepdims=True)
    cnt_ref[...] = carry_ref[...]

    rows = [e1.astype(F32), e2.astype(F32), w1, w2, rank1, rank2]
    table_t = jnp.concatenate(rows + [jnp.zeros((LANES - len(rows), tm), F32)], axis=0)
    route_t_ref[...] = table_t[0:ROUTE_ROWS]
    route_ref[...] = table_t.T


def _outproj(x2d, ohg, z2d, conv_w, sc_gain, wo1, wo2, g_ffn, wr_hl, wr_h, tm, seq_len):
    n, d = x2d.shape
    const = lambda i: (0, 0)
    sc_first = 4 * HG_WIDTH // SC_WIDTH

    def zspec(k):
        return pl.BlockSpec((tm, SC_WIDTH), lambda i, k=k: (i, sc_first + k))

    return pl.pallas_call(
        functools.partial(_outproj_kernel, tm=tm, tiles_per_seq=seq_len // tm),
        out_shape=(
            jax.ShapeDtypeStruct((n, d), F32),
            jax.ShapeDtypeStruct((n * _token_rows(d), LANES), jnp.uint32),
            jax.ShapeDtypeStruct((n, LANES), F32),
            jax.ShapeDtypeStruct((ROUTE_ROWS, n), F32),
            jax.ShapeDtypeStruct((N_EXPERTS, 1), F32),
        ),
        grid=(n // tm,),
        in_specs=[
            pl.BlockSpec((tm, d), lambda i: (i, 0)),
            pl.BlockSpec((tm, HG_WIDTH), lambda i: (i, 0)),
            zspec(0), zspec(1), zspec(2),
            pl.BlockSpec((3, SC_WIDTH), const),
            pl.BlockSpec((1, SC_WIDTH), const),
            pl.BlockSpec((HG_WIDTH, d), const),
            pl.BlockSpec((SC_WIDTH, d), lambda i: (HG_WIDTH // SC_WIDTH, 0)),
            pl.BlockSpec((1, d), const),
            pl.BlockSpec((d, 2 * LANES), const),
            pl.BlockSpec((d, LANES), const),
        ],
        out_specs=(
            pl.BlockSpec((tm, d), lambda i: (i, 0)),
            pl.BlockSpec((tm * _token_rows(d), LANES), lambda i: (i, 0)),
            pl.BlockSpec((tm, LANES), lambda i: (i, 0)),
            pl.BlockSpec((ROUTE_ROWS, tm), lambda i: (0, i)),
            pl.BlockSpec((N_EXPERTS, 1), const),
        ),
        scratch_shapes=[pltpu.VMEM((N_EXPERTS, 1), F32), pltpu.VMEM((SC_HALO, SC_WIDTH), F32)],
        compiler_params=_params(("arbitrary",)),
        name="outproj",
    )(x2d, ohg, z2d, z2d, z2d, conv_w, sc_gain, wo1, wo2, g_ffn, wr_hl, wr_h)


ZERO_ROWS = 512


def _dispatch_kernel(cnt_ref, pad_ref, start_ref, dest_hbm, hn_ref, xs_hbm, idx_ref, zero_ref, idx_sem, sem,
                     *, td, tr, nsteps):
    i = pl.program_id(0)
    idx_copy = pltpu.make_async_copy(dest_hbm.at[pl.ds(i * 2 * td, 2 * td)], idx_ref, idx_sem)
    idx_copy.start()
    idx_copy.wait()

    def token_copy(t, slot):
        src = hn_ref.at[pl.ds(pl.multiple_of(t * tr, tr), tr)]
        return pltpu.make_async_copy(src, xs_hbm.at[pl.ds(pl.multiple_of(slot * tr, tr), tr)], sem)

    def issue(t, carry):
        token_copy(t, idx_ref[2 * t]).start(priority=0)
        token_copy(t, idx_ref[2 * t + 1]).start(priority=1)
        return carry

    lax.fori_loop(0, td, issue, 0, unroll=8)
    tile_copy = pltpu.make_async_copy(hn_ref, xs_hbm.at[pl.ds(0, td * tr)], sem)
    tile_copy.wait()
    tile_copy.wait()

    @pl.when(i == nsteps - 1)
    def _():
        zero_ref[...] = jnp.zeros_like(zero_ref)
        zrows = zero_ref.shape[0] // tr

        def run_copy(pos, rows):
            dst = xs_hbm.at[pl.ds(pl.multiple_of(pos * tr, tr), rows * tr)]
            return pltpu.make_async_copy(zero_ref.at[pl.ds(0, rows * tr)], dst, sem)

        def per_run(e, carry):
            n = pad_ref[e] - cnt_ref[e]
            pos0 = start_ref[e] + cnt_ref[e]
            nfull = n // zrows
            rem = n - nfull * zrows
            bits = [1 << s for s in reversed(range(zrows.bit_length() - 1))]

            def start_full(j, c2):
                run_copy(pos0 + j * zrows, zrows).start()
                return c2

            def wait_full(j, c2):
                run_copy(0, zrows).wait()
                return c2

            lax.fori_loop(0, nfull, start_full, 0)
            pos = pos0 + nfull * zrows
            for bit in bits:
                @pl.when((rem & bit) != 0)
                def _(pos=pos, bit=bit):
                    run_copy(pos, bit).start()

                pos = pos + (rem & bit)
            lax.fori_loop(0, nfull, wait_full, 0)
            for bit in bits:
                @pl.when((rem & bit) != 0)
                def _(bit=bit):
                    run_copy(0, bit).wait()

            return carry

        lax.fori_loop(0, N_EXPERTS + 1, per_run, 0)


def _dispatch(counts, padded, pstart, dest_flat, hn_packed, n, tr, p_rows, td):
    nsteps = n // td
    grid_spec = pltpu.PrefetchScalarGridSpec(
        num_scalar_prefetch=3,
        grid=(nsteps,),
        in_specs=[
            pl.BlockSpec(memory_space=pl.ANY),
            pl.BlockSpec((td * tr, LANES), lambda i, *_: (i, 0)),
        ],
        out_specs=pl.BlockSpec(memory_space=pl.ANY),
        scratch_shapes=[
            pltpu.SMEM((2 * td,), jnp.int32),
            pltpu.VMEM((ZERO_ROWS * tr, LANES), jnp.uint32),
            pltpu.SemaphoreType.DMA,
            pltpu.SemaphoreType.DMA,
        ],
    )
    return pl.pallas_call(
        functools.partial(_dispatch_kernel, td=td, tr=tr, nsteps=nsteps),
        out_shape=jax.ShapeDtypeStruct((p_rows * tr, LANES), jnp.uint32),
        grid_spec=grid_spec,
        compiler_params=_params(("arbitrary",)),
        name="dispatch",
    )(counts, padded, pstart, dest_flat, hn_packed)


def _experts_kernel(blk_e_ref, first_ref, slot_ref, next_ref, nused_ref, xs_ref, wg_hbm, wu_hbm, wd_hbm, y_ref,
                    wg_buf, wu_buf, wd_buf, sem, *, tb, tr):
    i = pl.program_id(0)

    def fetch(e, slot):
        pairs = ((wg_hbm, wg_buf), (wu_hbm, wu_buf), (wd_hbm, wd_buf))
        return [pltpu.make_async_copy(src.at[e], buf.at[slot], sem.at[slot, j]) for j, (src, buf) in enumerate(pairs)]

    @pl.when(i >= nused_ref[0])
    def _():
        y_ref[...] = jnp.zeros_like(y_ref)

    @pl.when(i < nused_ref[0])
    def _():
        slot = slot_ref[i]

        @pl.when(first_ref[i] == 1)
        def _():
            @pl.when(i == 0)
            def _():
                for copy in fetch(blk_e_ref[0], slot):
                    copy.start()

            for copy in fetch(blk_e_ref[i], slot):
                copy.wait()

            @pl.when(next_ref[i] >= 0)
            def _():
                for copy in fetch(next_ref[i], 1 - slot):
                    copy.start()

        x = _unpack_tokens(lambda s: xs_ref[pl.ds(s, tb, stride=tr), :], tr)
        a = jnp.dot(x, wg_buf[slot], preferred_element_type=F32)
        b = jnp.dot(x, wu_buf[slot], preferred_element_type=F32)
        h = a * jax.nn.sigmoid(a) * b
        _pack_tokens(jnp.dot(h, wd_buf[slot], preferred_element_type=F32), y_ref)


def _experts(block_e, counts, nused, xs_packed, wg, wu, wd, tb):
    d, ff = wg.shape[1], wg.shape[2]
    tr = _token_rows(d)
    nblocks = xs_packed.shape[0] // (tb * tr)

    blk = jnp.arange(nblocks, dtype=jnp.int32)
    first = ((blk == 0) | (block_e != jnp.roll(block_e, 1))) & (blk < nused[0])
    slot = (jnp.cumsum(first.astype(jnp.int32)) - 1) % 2
    e_iota = jnp.arange(N_EXPERTS, dtype=jnp.int32)
    later_used = (e_iota[None, :] > e_iota[:, None]) & (counts > 0)[None, :]
    next_used = jnp.min(jnp.where(later_used, e_iota[None, :], N_EXPERTS), axis=1)
    next_used = jnp.where(next_used == N_EXPERTS, -1, next_used)
    next_e = jnp.sum(jnp.where(block_e[:, None] == e_iota[None, :], next_used[None, :], 0), axis=1)

    def row_map(i, *prefetch):
        return (jnp.minimum(i, prefetch[-1][0] - 1), 0)

    grid_spec = pltpu.PrefetchScalarGridSpec(
        num_scalar_prefetch=5,
        grid=(nblocks,),
        in_specs=[
            pl.BlockSpec((tb * tr, LANES), row_map),
            pl.BlockSpec(memory_space=pl.ANY),
            pl.BlockSpec(memory_space=pl.ANY),
            pl.BlockSpec(memory_space=pl.ANY),
        ],
        out_specs=pl.BlockSpec((tb * tr, LANES), lambda i, *prefetch: (i, 0)),
        scratch_shapes=[
            pltpu.VMEM((2, d, ff), F32),
            pltpu.VMEM((2, d, ff), F32),
            pltpu.VMEM((2, ff, d), F32),
            pltpu.SemaphoreType.DMA((2, 3)),
        ],
    )
    return pl.pallas_call(
        functools.partial(_experts_kernel, tb=tb, tr=tr),
        out_shape=jax.ShapeDtypeStruct(xs_packed.shape, jnp.uint32),
        grid_spec=grid_spec,
        compiler_params=_params(("arbitrary",)),
        name="experts",
    )(block_e, first.astype(jnp.int32), slot.astype(jnp.int32), next_e.astype(jnp.int32), nused,
      xs_packed, wg, wu, wd)


def _final_kernel(dest_hbm, route_ref, x1_ref, p_ref, wple_ref, gple_ref, wgate_ref, gfin_ref, y_hbm,
                  o_ref, idx_a, idx_b, ybuf_a, ybuf_b, idx_sem, sem, *, tf, tr, nsteps):
    i = pl.program_id(0)
    halves = ((idx_a, ybuf_a), (idx_b, ybuf_b))

    def idx_copy(tile, h):
        return pltpu.make_async_copy(dest_hbm.at[pl.ds(tile * 2 * tf, 2 * tf)], halves[h][0], idx_sem.at[h])

    def gather(h):
        idx_ref, ybuf_ref = halves[h]

        def issue(t, carry):
            for k in range(2):
                src = y_hbm.at[pl.ds(pl.multiple_of(idx_ref[2 * t + k] * tr, tr), tr)]
                dst = ybuf_ref.at[k, pl.ds(pl.multiple_of(t * tr, tr), tr)]
                pltpu.make_async_copy(src, dst, sem.at[h]).start(priority=k)
            return carry

        lax.fori_loop(0, tf, issue, 0, unroll=True)

    def wait_rows(h):
        for k in range(2):
            pltpu.make_async_copy(y_hbm.at[pl.ds(0, tf * tr)], halves[h][1].at[k], sem.at[h]).wait()

    def combine(h):
        ybuf_ref = halves[h][1]
        rows = pl.ds(h * tf, tf)
        wait_rows(h)
        route = route_ref[rows, :]
        w1 = route[:, R_W1:R_W1 + 1]
        w2 = route[:, R_W2:R_W2 + 1]
        y1 = _unpack_tokens(lambda s: ybuf_ref[0, pl.ds(s, tf, stride=tr), :], tr)
        y2 = _unpack_tokens(lambda s: ybuf_ref[1, pl.ds(s, tf, stride=tr), :], tr)
        return x1_ref[rows, :] + w1 * y1 + w2 * y2

    def finish(h, x2):
        rows = pl.ds(h * tf, tf)
        ple = _rms(jnp.dot(p_ref[rows, :], wple_ref[...], preferred_element_type=F32), gple_ref[...])
        gate = jax.nn.sigmoid(jnp.dot(x2, wgate_ref[...], preferred_element_type=F32))
        o_ref[rows, :] = _rms(x2 + gate * ple, gfin_ref[...])

    @pl.when(i == 0)
    def _():
        for h in range(2):
            idx_copy(h, h).start()
            idx_copy(h, h).wait()
            gather(h)

    for h in range(2):
        idx_copy(2 * i + 2 + h, h).start()
    for h in range(2):
        x2 = combine(h)
        idx_copy(2 * i + 2 + h, h).wait()
        gather(h)
        finish(h, x2)

    @pl.when(i == nsteps - 1)
    def _():
        for h in range(2):
            wait_rows(h)


def _final(dest_flat, route, x1, p2d, layer, wple, gple, wgate, gfin, y, tf):
    n, d = x1.shape
    pd = p2d.shape[1]
    tr = _token_rows(d)
    assert n % (2 * tf) == 0, (n, tf)
    nsteps = n // (2 * tf)
    p_block0 = layer * nsteps
    dest_padded = jnp.concatenate([dest_flat, jnp.zeros((4 * tf,), jnp.int32)])
    const = lambda i: (0, 0)
    return pl.pallas_call(
        functools.partial(_final_kernel, tf=tf, tr=tr, nsteps=nsteps),
        out_shape=jax.ShapeDtypeStruct((n, d), F32),
        grid=(nsteps,),
        in_specs=[
            pl.BlockSpec(memory_space=pl.ANY),
            pl.BlockSpec((2 * tf, LANES), lambda i: (i, 0)),
            pl.BlockSpec((2 * tf, d), lambda i: (i, 0)),
            pl.BlockSpec((2 * tf, pd), lambda i: (i + p_block0, 0)),
            pl.BlockSpec((pd, d), const, pipeline_mode=pl.Buffered(1)),
            pl.BlockSpec((1, d), const),
            pl.BlockSpec((d, d), const, pipeline_mode=pl.Buffered(1)),
            pl.BlockSpec((1, d), const),
            pl.BlockSpec(memory_space=pl.ANY),
        ],
        out_specs=pl.BlockSpec((2 * tf, d), lambda i: (i, 0)),
        scratch_shapes=[
            pltpu.SMEM((2 * tf,), jnp.int32),
            pltpu.SMEM((2 * tf,), jnp.int32),
            pltpu.VMEM((2, tf * tr, LANES), jnp.uint32),
            pltpu.VMEM((2, tf * tr, LANES), jnp.uint32),
            pltpu.SemaphoreType.DMA((2,)),
            pltpu.SemaphoreType.DMA((2,)),
        ],
        compiler_params=_params(("arbitrary",)),
        name="final",
    )(dest_padded, route, x1, p2d, wple, gple, wgate, gfin, y)


def _tile(n, want):
    t = min(n, want)
    assert n % t == 0, (n, want)
    return t


class _Tiles(NamedTuple):
    inproj_rows: int
    inproj_cols: int
    hgrn_rows: int
    outproj_rows: int
    dispatch_rows: int
    expert_rows: int
    final_rows: int


def _tiles(b, t):
    n = b * t
    return _Tiles(inproj_rows=_tile(n, 1024), inproj_cols=1792, hgrn_rows=_tile(t, 8192),
                  outproj_rows=_tile(t, 512), dispatch_rows=_tile(n, 4096), expert_rows=512,
                  final_rows=_tile(n, 256))


def kernel(x, p, g_mix, w_in, lb_logits, hg_norm, conv_w, sc_norm, w_out, g_ffn, w_router_group,
           w_router_expert, w_gate, w_up, w_down, w_ple, g_ple, w_ple_gate, g_final):
    b, t, d = x.shape
    n = b * t
    layer = 0
    tiles = _tiles(b, t)
    assert w_in.shape[2] == 4 * HG_WIDTH + 3 * SC_WIDTH and w_out.shape[1] == HG_WIDTH + SC_WIDTH
    x2d = x.reshape(n, d)

    lower_bounds = jnp.cumsum(jax.nn.softmax(lb_logits.astype(F32), axis=0), axis=0)
    lb = lower_bounds[layer].reshape(1, HG_WIDTH)

    z = _inproj(x2d, g_mix[layer].reshape(1, d), w_in[layer].astype(BF16), tiles.inproj_rows, tiles.inproj_cols)
    z3 = z.reshape(b, t, z.shape[1])
    ohg = _hgrn(z3, lb, hg_norm[layer].reshape(1, HEAD_DIM), tiles.hgrn_rows).reshape(n, HG_WIDTH)

    wo = w_out[layer].astype(BF16)
    wr = jnp.concatenate([jnp.pad(w_router_group[layer], ((0, 0), (0, EXPERT_LANE0 - N_GROUPS))),
                          w_router_expert[layer]], axis=1).astype(F32)
    wr = jnp.pad(wr, ((0, 0), (0, LANES - wr.shape[1])))
    wr_hi = wr.astype(BF16)
    wr_lo = (wr - wr_hi.astype(F32)).astype(BF16)
    x1, hn, route, route_t, cnt = _outproj(
        x2d, ohg, z, conv_w[layer], sc_norm[layer].reshape(1, SC_WIDTH), wo, wo,
        g_ffn[layer].reshape(1, d), jnp.concatenate([wr_hi, wr_lo], axis=1), wr_hi, tiles.outproj_rows, t)

    tb = tiles.expert_rows
    counts = cnt[:, 0].astype(jnp.int32)
    padded = (counts + tb - 1) // tb * tb
    pend = jnp.cumsum(padded)
    pstart = pend - padded
    e_ids = route_t[R_E1:R_E2 + 1].astype(jnp.int32)
    ranks = route_t[R_RANK1:R_RANK2 + 1].astype(jnp.int32)
    expert_iota = jnp.arange(N_EXPERTS, dtype=jnp.int32)[None, :, None]
    seg_start = jnp.sum(jnp.where(e_ids[:, None, :] == expert_iota, pstart.astype(jnp.int32)[None, :, None], 0), axis=1)
    dest_flat = (seg_start + ranks).T.reshape(2 * n)
    nblocks = -(-(2 * n + N_EXPERTS * (tb - 1)) // tb)
    nused = (pend[-1] // tb).astype(jnp.int32)
    blk = jnp.minimum(jnp.arange(nblocks, dtype=jnp.int32), nused - 1) * tb
    block_e = jnp.minimum(jnp.sum((blk[:, None] >= pend[None, :]).astype(jnp.int32), axis=1), N_EXPERTS - 1)

    p_rows = nblocks * tb
    zero = jnp.zeros((1,), jnp.int32)
    xs = _dispatch(jnp.concatenate([counts, zero]), jnp.concatenate([padded, p_rows - pend[-1:]]),
                   jnp.concatenate([pstart, pend[-1:]]).astype(jnp.int32), dest_flat, hn, n, _token_rows(d),
                   p_rows, tiles.dispatch_rows)
    y = _experts(block_e, counts, nused.reshape(1), xs, w_gate[layer], w_up[layer], w_down[layer], tb)

    out = _final(dest_flat, route, x1, p.reshape(-1, p.shape[-1]), layer, w_ple[layer],
                 g_ple[layer].reshape(1, d), w_ple_gate[layer], g_final.reshape(1, d), y,
                 tiles.final_rows)
    return out.reshape(b, t, d)
```

```python
import functools
from typing import NamedTuple

import jax
import jax.numpy as jnp
from jax import lax
from jax.experimental import pallas as pl
from jax.experimental.pallas import tpu as pltpu

F32 = jnp.float32
BF16 = jnp.bfloat16
EPS = 1e-6

HEAD_DIM = 128
N_HEADS = 8
HG_WIDTH = N_HEADS * HEAD_DIM
SC_WIDTH = 1024
CHUNK = 128
N_GROUPS = 4
EXPERTS_PER_GROUP = 8
N_EXPERTS = N_GROUPS * EXPERTS_PER_GROUP
LANES = 128
EXPERT_LANE0 = EXPERTS_PER_GROUP
NEG_BIG = -1e30
VMEM_LIMIT = 56 * 1024 * 1024


def _rms(v, gain):
    return v * lax.rsqrt(jnp.mean(v * v, axis=-1, keepdims=True) + EPS) * gain


def _token_rows(d):
    return d // (2 * LANES)


def _pack_tokens(v, out_ref):
    rows, d = v.shape
    tr = _token_rows(d)
    bits = lax.bitcast_convert_type(v.astype(BF16).astype(F32), jnp.uint32)
    words = bits[:, :d // 2] | (bits[:, d // 2:] >> 16)
    for s in range(tr):
        out_ref[pl.ds(s, rows, stride=tr), :] = words[:, s * LANES:(s + 1) * LANES]


def _unpack_tokens(load_rows, tr):
    words = [load_rows(s) for s in range(tr)]
    high = [lax.bitcast_convert_type(w & jnp.uint32(0xFFFF0000), F32) for w in words]
    low = [lax.bitcast_convert_type(w << 16, F32) for w in words]
    return jnp.concatenate(high + low, axis=1)


def _params(sem):
    return pltpu.CompilerParams(dimension_semantics=sem, vmem_limit_bytes=VMEM_LIMIT)


def _inproj_kernel(x_ref, g_ref, w_ref, z_ref, xn_ref):
    @pl.when(pl.program_id(1) == 0)
    def _():
        xn_ref[...] = _rms(x_ref[...], g_ref[...]).astype(BF16)

    z_ref[...] = jnp.dot(xn_ref[...], w_ref[...], preferred_element_type=F32).astype(z_ref.dtype)


def _inproj(x2d, g_mix, w_in_bf16, tm, tn):
    n, d = x2d.shape
    cols = w_in_bf16.shape[1]
    return pl.pallas_call(
        _inproj_kernel,
        out_shape=jax.ShapeDtypeStruct((n, cols), BF16),
        grid=(n // tm, cols // tn),
        in_specs=[
            pl.BlockSpec((tm, d), lambda i, j: (i, 0)),
            pl.BlockSpec((1, d), lambda i, j: (0, 0)),
            pl.BlockSpec((d, tn), lambda i, j: (0, j)),
        ],
        out_specs=pl.BlockSpec((tm, tn), lambda i, j: (i, j)),
        scratch_shapes=[pltpu.VMEM((tm, d), BF16)],
        compiler_params=_params(("arbitrary", "arbitrary")),
        name="inproj",
    )(x2d, g_mix, w_in_bf16)


def _hgrn_kernel(q_ref, f_ref, i_ref, g_ref, lb_ref, gain_ref, o_ref, st_ref, *, nchunks):
    @pl.when(pl.program_id(2) == 0)
    def _():
        st_ref[...] = jnp.zeros_like(st_ref)

    c_len = CHUNK
    n_levels = c_len.bit_length() - 1
    lb = lb_ref[...]
    one_minus_lb = 1.0 - lb
    gain = gain_ref[...]
    row = lax.broadcasted_iota(jnp.int32, (c_len, c_len), 0)
    col = lax.broadcasted_iota(jnp.int32, (c_len, c_len), 1)
    tri = (row >= col).astype(BF16)
    tri2 = jnp.concatenate([tri, tri], axis=1)
    differ = jnp.where(col < row, row ^ col, 0)
    sub8 = lax.broadcasted_iota(jnp.int32, (c_len // 8, 8, HEAD_DIM), 1)
    scale = HEAD_DIM ** -0.5
    nt = (((1,), (1,)), ((), ()))
    sign_bit = jnp.uint32(0x80000000)

    def neg_abs(d):
        return lax.bitcast_convert_type(lax.bitcast_convert_type(d, jnp.uint32) | sign_bit, F32)

    def boundary(a, half):
        if half >= 4:
            nb = c_len // (2 * half)
            a3 = a.reshape(nb, 2 * half, HEAD_DIM)
            return jnp.broadcast_to(a3[:, half - 1:half, :], a3.shape).reshape(c_len, HEAD_DIM)
        assert half == 2
        a3 = a.reshape(c_len // 8, 8, HEAD_DIM)
        return jnp.where(sub8 < 4, a3[:, 1:2, :], a3[:, 5:6, :]).reshape(c_len, HEAD_DIM)

    def body(c, st):
        sl = pl.ds(pl.multiple_of(c * c_len, c_len), c_len)
        fz = f_ref[sl, :].astype(F32)
        sg = jax.nn.sigmoid(fz)
        f = lb + one_minus_lb * sg
        log2_f = jnp.log2(f)
        k = (one_minus_lb * (1.0 - sg)).astype(BF16)
        h1 = log2_f.astype(BF16)
        h2 = (log2_f - h1.astype(F32)).astype(BF16)
        a = jnp.dot(tri2, jnp.concatenate([h1, h2], axis=0), preferred_element_type=F32)
        a_last = a[c_len - 1:c_len, :]
        qz = q_ref[sl, :]
        q = qz * jax.nn.sigmoid(qz) * scale
        qa = q * jnp.exp2(a).astype(BF16)
        kd = k * jnp.exp2(a_last - a).astype(BF16)
        v = i_ref[sl, :]
        diag = jnp.sum(q.astype(F32) * k.astype(F32), axis=-1, keepdims=True)
        scores = jnp.where(row == col, diag, 0.0)
        for level in range(n_levels):
            if level == 0:
                ql, kl = q * f.astype(BF16), k
            else:
                decay = jnp.exp2(neg_abs(a - boundary(a, 1 << level))).astype(BF16)
                ql, kl = q * decay, k * decay
            gram = lax.dot_general(ql, kl, nt, preferred_element_type=F32)
            scores = jnp.where(differ >= (1 << level), gram, scores)
        p = scores.astype(BF16)
        o = lax.dot_general(qa, st.astype(BF16), nt, preferred_element_type=F32)
        o = o + jnp.dot(p, v, preferred_element_type=F32)
        v_t = v.astype(F32).T.astype(BF16)
        st_new = st * jnp.exp2(a_last) + jnp.dot(v_t, kd, preferred_element_type=F32)
        gz = g_ref[sl, :]
        gate = (gz * jax.nn.sigmoid(gz)).astype(F32)
        o_ref[sl, :] = (_rms(o, gain) * gate).astype(o_ref.dtype)
        return st_new

    st_ref[...] = lax.fori_loop(0, nchunks, body, st_ref[...], unroll=True)


def _hgrn(z3, lb, gain, tc):
    b, t, _ = z3.shape
    hb = HG_WIDTH // HEAD_DIM

    def zspec(k):
        return pl.BlockSpec((None, tc, HEAD_DIM), lambda bi, h, ti, k=k: (bi, ti, h + k * hb))

    return pl.pallas_call(
        functools.partial(_hgrn_kernel, nchunks=tc // CHUNK),
        out_shape=jax.ShapeDtypeStruct((b, t, HG_WIDTH), BF16),
        grid=(b, N_HEADS, t // tc),
        in_specs=[
            zspec(0), zspec(1), zspec(2), zspec(3),
            pl.BlockSpec((1, HEAD_DIM), lambda bi, h, ti: (0, h)),
            pl.BlockSpec((1, HEAD_DIM), lambda bi, h, ti: (0, 0)),
        ],
        out_specs=pl.BlockSpec((None, tc, HEAD_DIM), lambda bi, h, ti: (bi, ti, h)),
        scratch_shapes=[pltpu.VMEM((HEAD_DIM, HEAD_DIM), F32)],
        compiler_params=_params(("arbitrary", "arbitrary", "arbitrary")),
        name="hgrn",
    )(z3, z3, z3, z3, lb, gain)


SC_HALO = 8


def _short_conv(b_ref, c_ref, h_ref, w_ref, gain_ref, u_ref, new_sequence, tc):
    @pl.when(new_sequence)
    def _():
        u_ref[...] = jnp.zeros_like(u_ref)

    u = c_ref[...].astype(F32) * h_ref[...].astype(F32)
    prev = u_ref[...]
    sub = lax.broadcasted_iota(jnp.int32, (SC_HALO, SC_WIDTH), 0)

    def delayed(k):
        rolled = pltpu.roll(u, k, 0)
        head = jnp.where(sub < k, pltpu.roll(prev, k, 0), rolled[0:SC_HALO])
        return jnp.concatenate([head, rolled[SC_HALO:]], axis=0)

    w = w_ref[...]
    y = w[0:1, :] * delayed(2) + w[1:2, :] * delayed(1) + w[2:3, :] * u
    out = _rms(b_ref[...].astype(F32) * y, gain_ref[...]).astype(BF16)
    u_ref[...] = u[tc - SC_HALO:tc]
    return out


R_E1, R_E2, R_W1, R_W2, R_RANK1, R_RANK2 = 0, 1, 2, 3, 4, 5
ROUTE_ROWS = 8


def _outproj_kernel(x_ref, ohg_ref, zb_ref, zc_ref, zh_ref, convw_ref, scgain_ref, wo1_ref, wo2_ref, g_ref,
                    wr_hl_ref, wr_h_ref, x1_ref, hn_ref, route_ref, route_t_ref, cnt_ref, carry_ref, u_ref,
                    *, tm, tiles_per_seq):
    @pl.when(pl.program_id(0) == 0)
    def _():
        carry_ref[...] = jnp.zeros_like(carry_ref)

    osc = _short_conv(zb_ref, zc_ref, zh_ref, convw_ref, scgain_ref, u_ref,
                      pl.program_id(0) % tiles_per_seq == 0, tm)
    acc = jnp.dot(ohg_ref[...], wo1_ref[...], preferred_element_type=F32)
    acc = acc + jnp.dot(osc, wo2_ref[...], preferred_element_type=F32)
    x1 = x_ref[...] + acc
    x1_ref[...] = x1
    hn = _rms(x1, g_ref[...])
    _pack_tokens(hn, hn_ref)

    hi = hn.astype(BF16)
    lo = (hn - hi.astype(F32)).astype(BF16)
    both = jnp.dot(hi, wr_hl_ref[...], preferred_element_type=F32)
    logits = both[:, :LANES] + both[:, LANES:] + jnp.dot(lo, wr_h_ref[...], preferred_element_type=F32)

    lt = logits.T
    sub = lax.broadcasted_iota(jnp.int32, (EXPERTS_PER_GROUP, tm), 0)

    def first_argmax(vals, vmax):
        return jnp.min(jnp.where(vals == vmax, sub, EXPERTS_PER_GROUP), axis=0, keepdims=True)

    is_group = sub < N_GROUPS
    gl = jnp.where(is_group, lt[0:EXPERTS_PER_GROUP], NEG_BIG)
    gmax = jnp.max(gl, axis=0, keepdims=True)
    gidx = first_argmax(gl, gmax)
    g_p = 1.0 / jnp.sum(jnp.where(is_group, jnp.exp(gl - gmax), 0.0), axis=0, keepdims=True)

    el = lt[EXPERT_LANE0:EXPERT_LANE0 + EXPERTS_PER_GROUP]
    for g in range(1, N_GROUPS):
        lo_row = EXPERT_LANE0 + g * EXPERTS_PER_GROUP
        el = jnp.where(gidx == g, lt[lo_row:lo_row + EXPERTS_PER_GROUP], el)
    m1 = jnp.max(el, axis=0, keepdims=True)
    i1 = first_argmax(el, m1)
    el2 = jnp.where(sub == i1, NEG_BIG, el)
    m2 = jnp.max(el2, axis=0, keepdims=True)
    i2 = first_argmax(el2, m2)
    r = jnp.exp(m2 - m1)
    w1 = g_p / (1.0 + r)
    w2 = g_p * r / (1.0 + r)
    e1 = gidx * EXPERTS_PER_GROUP + i1
    e2 = gidx * EXPERTS_PER_GROUP + i2

    erow = lax.broadcasted_iota(jnp.int32, (N_EXPERTS, tm), 0)
    oh1 = erow == e1
    oh2 = erow == e2
    onehot = (oh1 | oh2).astype(BF16)
    trow = lax.broadcasted_iota(jnp.int32, (tm, tm), 0)
    tcol = lax.broadcasted_iota(jnp.int32, (tm, tm), 1)
    before = jnp.dot(onehot, (trow < tcol).astype(BF16), preferred_element_type=F32) + carry_ref[...]
    rank1 = jnp.sum(jnp.where(oh1, before, 0.0), axis=0, keepdims=True)
    rank2 = jnp.sum(jnp.where(oh2, before, 0.0), axis=0, keepdims=True)
    carry_ref[...] = carry_ref[...] + jnp.sum(onehot.astype(F32), axis=1, keepdims=True)
    cnt_ref[...] = carry_ref[...]

    rows = [e1.astype(F32), e2.astype(F32), w1, w2, rank1, rank2]
    table_t = jnp.concatenate(rows + [jnp.zeros((LANES - len(rows), tm), F32)], axis=0)
    route_t_ref[...] = table_t[0:ROUTE_ROWS]
    route_ref[...] = table_t.T


def _outproj(x2d, ohg, z2d, conv_w, sc_gain, wo1, wo2, g_ffn, wr_hl, wr_h, tm, seq_len):
    n, d = x2d.shape
    const = lambda i: (0, 0)
    sc_first = 4 * HG_WIDTH // SC_WIDTH

    def zspec(k):
        return pl.BlockSpec((tm, SC_WIDTH), lambda i, k=k: (i, sc_first + k))

    return pl.pallas_call(
        functools.partial(_outproj_kernel, tm=tm, tiles_per_seq=seq_len // tm),
        out_shape=(
            jax.ShapeDtypeStruct((n, d), F32),
            jax.ShapeDtypeStruct((n * _token_rows(d), LANES), jnp.uint32),
            jax.ShapeDtypeStruct((n, LANES), F32),
            jax.ShapeDtypeStruct((ROUTE_ROWS, n), F32),
            jax.ShapeDtypeStruct((N_EXPERTS, 1), F32),
        ),
        grid=(n // tm,),
        in_specs=[
            pl.BlockSpec((tm, d), lambda i: (i, 0)),
            pl.BlockSpec((tm, HG_WIDTH), lambda i: (i, 0)),
            zspec(0), zspec(1), zspec(2),
            pl.BlockSpec((3, SC_WIDTH), const),
            pl.BlockSpec((1, SC_WIDTH), const),
            pl.BlockSpec((HG_WIDTH, d), const),
            pl.BlockSpec((SC_WIDTH, d), lambda i: (HG_WIDTH // SC_WIDTH, 0)),
            pl.BlockSpec((1, d), const),
            pl.BlockSpec((d, 2 * LANES), const),
            pl.BlockSpec((d, LANES), const),
        ],
        out_specs=(
            pl.BlockSpec((tm, d), lambda i: (i, 0)),
            pl.BlockSpec((tm * _token_rows(d), LANES), lambda i: (i, 0)),
            pl.BlockSpec((tm, LANES), lambda i: (i, 0)),
            pl.BlockSpec((ROUTE_ROWS, tm), lambda i: (0, i)),
            pl.BlockSpec((N_EXPERTS, 1), const),
        ),
        scratch_shapes=[pltpu.VMEM((N_EXPERTS, 1), F32), pltpu.VMEM((SC_HALO, SC_WIDTH), F32)],
        compiler_params=_params(("arbitrary",)),
        name="outproj",
    )(x2d, ohg, z2d, z2d, z2d, conv_w, sc_gain, wo1, wo2, g_ffn, wr_hl, wr_h)


ZERO_ROWS = 512


def _dispatch_kernel(cnt_ref, pad_ref, start_ref, dest_hbm, hn_ref, xs_hbm, idx_ref, zero_ref, idx_sem, sem,
                     *, td, tr, nsteps, row_len):
    i = pl.program_id(0)
    idx_copies = [pltpu.make_async_copy(dest_hbm.at[pl.ds(k * row_len + i * td, td)],
                                        idx_ref.at[pl.ds(k * td, td)], idx_sem) for k in range(2)]
    for copy in idx_copies:
        copy.start()
    for copy in idx_copies:
        copy.wait()

    def token_copy(t, slot):
        src = hn_ref.at[pl.ds(pl.multiple_of(t * tr, tr), tr)]
        return pltpu.make_async_copy(src, xs_hbm.at[pl.ds(pl.multiple_of(slot * tr, tr), tr)], sem)

    def issue(t, carry):
        token_copy(t, idx_ref[t]).start(priority=0)
        token_copy(t, idx_ref[td + t]).start(priority=1)
        return carry

    lax.fori_loop(0, td, issue, 0, unroll=8)
    tile_copy = pltpu.make_async_copy(hn_ref, xs_hbm.at[pl.ds(0, td * tr)], sem)
    tile_copy.wait()
    tile_copy.wait()

    @pl.when(i == nsteps - 1)
    def _():
        zero_ref[...] = jnp.zeros_like(zero_ref)
        zrows = zero_ref.shape[0] // tr

        def run_copy(pos, rows):
            dst = xs_hbm.at[pl.ds(pl.multiple_of(pos * tr, tr), rows * tr)]
            return pltpu.make_async_copy(zero_ref.at[pl.ds(0, rows * tr)], dst, sem)

        def per_run(e, carry):
            n = pad_ref[e] - cnt_ref[e]
            pos0 = start_ref[e] + cnt_ref[e]
            nfull = n // zrows
            rem = n - nfull * zrows
            bits = [1 << s for s in reversed(range(zrows.bit_length() - 1))]

            def start_full(j, c2):
                run_copy(pos0 + j * zrows, zrows).start()
                return c2

            def wait_full(j, c2):
                run_copy(0, zrows).wait()
                return c2

            lax.fori_loop(0, nfull, start_full, 0)
            pos = pos0 + nfull * zrows
            for bit in bits:
                @pl.when((rem & bit) != 0)
                def _(pos=pos, bit=bit):
                    run_copy(pos, bit).start()

                pos = pos + (rem & bit)
            lax.fori_loop(0, nfull, wait_full, 0)
            for bit in bits:
                @pl.when((rem & bit) != 0)
                def _(bit=bit):
                    run_copy(0, bit).wait()

            return carry

        lax.fori_loop(0, N_EXPERTS + 1, per_run, 0)


def _dispatch(counts, padded, pstart, dest_kmajor, hn_packed, n, tr, p_rows, td):
    nsteps = n // td
    row_len = dest_kmajor.shape[0] // 2
    grid_spec = pltpu.PrefetchScalarGridSpec(
        num_scalar_prefetch=3,
        grid=(nsteps,),
        in_specs=[
            pl.BlockSpec(memory_space=pl.ANY),
            pl.BlockSpec((td * tr, LANES), lambda i, *_: (i, 0)),
        ],
        out_specs=pl.BlockSpec(memory_space=pl.ANY),
        scratch_shapes=[
            pltpu.SMEM((2 * td,), jnp.int32),
            pltpu.VMEM((ZERO_ROWS * tr, LANES), jnp.uint32),
            pltpu.SemaphoreType.DMA,
            pltpu.SemaphoreType.DMA,
        ],
    )
    return pl.pallas_call(
        functools.partial(_dispatch_kernel, td=td, tr=tr, nsteps=nsteps, row_len=row_len),
        out_shape=jax.ShapeDtypeStruct((p_rows * tr, LANES), jnp.uint32),
        grid_spec=grid_spec,
        compiler_params=_params(("arbitrary",)),
        name="dispatch",
    )(counts, padded, pstart, dest_kmajor, hn_packed)


def _experts_kernel(blk_e_ref, first_ref, slot_ref, next_ref, nused_ref, xs_ref, wg_hbm, wu_hbm, wd_hbm, y_ref,
                    wg_buf, wu_buf, wd_buf, sem, *, tb, tr):
    i = pl.program_id(0)

    def fetch(e, slot):
        pairs = ((wg_hbm, wg_buf), (wu_hbm, wu_buf), (wd_hbm, wd_buf))
        return [pltpu.make_async_copy(src.at[e], buf.at[slot], sem.at[slot, j]) for j, (src, buf) in enumerate(pairs)]

    @pl.when(i >= nused_ref[0])
    def _():
        y_ref[...] = jnp.zeros_like(y_ref)

    @pl.when(i < nused_ref[0])
    def _():
        slot = slot_ref[i]

        @pl.when(first_ref[i] == 1)
        def _():
            @pl.when(i == 0)
            def _():
                for copy in fetch(blk_e_ref[0], slot):
                    copy.start()

            for copy in fetch(blk_e_ref[i], slot):
                copy.wait()

            @pl.when(next_ref[i] >= 0)
            def _():
                for copy in fetch(next_ref[i], 1 - slot):
                    copy.start()

        x = _unpack_tokens(lambda s: xs_ref[pl.ds(s, tb, stride=tr), :], tr)
        a = jnp.dot(x, wg_buf[slot], preferred_element_type=F32)
        b = jnp.dot(x, wu_buf[slot], preferred_element_type=F32)
        h = a * jax.nn.sigmoid(a) * b
        _pack_tokens(jnp.dot(h, wd_buf[slot], preferred_element_type=F32), y_ref)


def _experts(block_e, counts, nused, xs_packed, wg, wu, wd, tb):
    d, ff = wg.shape[1], wg.shape[2]
    tr = _token_rows(d)
    nblocks = xs_packed.shape[0] // (tb * tr)

    blk = jnp.arange(nblocks, dtype=jnp.int32)
    first = ((blk == 0) | (block_e != jnp.roll(block_e, 1))) & (blk < nused[0])
    slot = (jnp.cumsum(first.astype(jnp.int32)) - 1) % 2
    e_iota = jnp.arange(N_EXPERTS, dtype=jnp.int32)
    later_used = (e_iota[None, :] > e_iota[:, None]) & (counts > 0)[None, :]
    next_used = jnp.min(jnp.where(later_used, e_iota[None, :], N_EXPERTS), axis=1)
    next_used = jnp.where(next_used == N_EXPERTS, -1, next_used)
    next_e = jnp.sum(jnp.where(block_e[:, None] == e_iota[None, :], next_used[None, :], 0), axis=1)

    def row_map(i, *prefetch):
        return (jnp.minimum(i, prefetch[-1][0] - 1), 0)

    grid_spec = pltpu.PrefetchScalarGridSpec(
        num_scalar_prefetch=5,
        grid=(nblocks,),
        in_specs=[
            pl.BlockSpec((tb * tr, LANES), row_map),
            pl.BlockSpec(memory_space=pl.ANY),
            pl.BlockSpec(memory_space=pl.ANY),
            pl.BlockSpec(memory_space=pl.ANY),
        ],
        out_specs=pl.BlockSpec((tb * tr, LANES), lambda i, *prefetch: (i, 0)),
        scratch_shapes=[
            pltpu.VMEM((2, d, ff), F32),
            pltpu.VMEM((2, d, ff), F32),
            pltpu.VMEM((2, ff, d), F32),
            pltpu.SemaphoreType.DMA((2, 3)),
        ],
    )
    return pl.pallas_call(
        functools.partial(_experts_kernel, tb=tb, tr=tr),
        out_shape=jax.ShapeDtypeStruct(xs_packed.shape, jnp.uint32),
        grid_spec=grid_spec,
        compiler_params=_params(("arbitrary",)),
        name="experts",
    )(block_e, first.astype(jnp.int32), slot.astype(jnp.int32), next_e.astype(jnp.int32), nused,
      xs_packed, wg, wu, wd)


def _final_kernel(dest_hbm, route_ref, x1_ref, p_ref, wple_ref, gple_ref, wgate_ref, gfin_ref, y_hbm,
                  o_ref, idx_a, idx_b, ybuf_a, ybuf_b, idx_sem, sem, *, tf, tr, nsteps, row_len):
    i = pl.program_id(0)
    halves = ((idx_a, ybuf_a), (idx_b, ybuf_b))

    def idx_copies(tile, h):
        return [pltpu.make_async_copy(dest_hbm.at[pl.ds(k * row_len + tile * tf, tf)],
                                      halves[h][0].at[pl.ds(k * tf, tf)], idx_sem.at[h]) for k in range(2)]

    def idx_start(tile, h):
        for copy in idx_copies(tile, h):
            copy.start()

    def idx_wait(tile, h):
        for copy in idx_copies(tile, h):
            copy.wait()

    def gather(h):
        idx_ref, ybuf_ref = halves[h]

        def issue(t, carry):
            for k in range(2):
                src = y_hbm.at[pl.ds(pl.multiple_of(idx_ref[k * tf + t] * tr, tr), tr)]
                dst = ybuf_ref.at[k, pl.ds(pl.multiple_of(t * tr, tr), tr)]
                pltpu.make_async_copy(src, dst, sem.at[h]).start(priority=k)
            return carry

        lax.fori_loop(0, tf, issue, 0, unroll=True)

    def wait_rows(h):
        for k in range(2):
            pltpu.make_async_copy(y_hbm.at[pl.ds(0, tf * tr)], halves[h][1].at[k], sem.at[h]).wait()

    def combine(h):
        ybuf_ref = halves[h][1]
        rows = pl.ds(h * tf, tf)
        wait_rows(h)
        route = route_ref[rows, :]
        w1 = route[:, R_W1:R_W1 + 1]
        w2 = route[:, R_W2:R_W2 + 1]
        y1 = _unpack_tokens(lambda s: ybuf_ref[0, pl.ds(s, tf, stride=tr), :], tr)
        y2 = _unpack_tokens(lambda s: ybuf_ref[1, pl.ds(s, tf, stride=tr), :], tr)
        return x1_ref[rows, :] + w1 * y1 + w2 * y2

    def finish(h, x2):
        rows = pl.ds(h * tf, tf)
        ple = _rms(jnp.dot(p_ref[rows, :], wple_ref[...], preferred_element_type=F32), gple_ref[...])
        gate = jax.nn.sigmoid(jnp.dot(x2, wgate_ref[...], preferred_element_type=F32))
        o_ref[rows, :] = _rms(x2 + gate * ple, gfin_ref[...])

    @pl.when(i == 0)
    def _():
        for h in range(2):
            idx_start(h, h)
            idx_wait(h, h)
            gather(h)

    for h in range(2):
        idx_start(2 * i + 2 + h, h)
    for h in range(2):
        x2 = combine(h)
        idx_wait(2 * i + 2 + h, h)
        gather(h)
        finish(h, x2)

    @pl.when(i == nsteps - 1)
    def _():
        for h in range(2):
            wait_rows(h)


def _final(dest_kmajor, route, x1, p2d, layer, wple, gple, wgate, gfin, y, tf):
    n, d = x1.shape
    pd = p2d.shape[1]
    tr = _token_rows(d)
    assert n % (2 * tf) == 0, (n, tf)
    nsteps = n // (2 * tf)
    p_block0 = layer * nsteps
    row_len = dest_kmajor.shape[0] // 2
    assert row_len >= n + 2 * tf
    const = lambda i: (0, 0)
    return pl.pallas_call(
        functools.partial(_final_kernel, tf=tf, tr=tr, nsteps=nsteps, row_len=row_len),
        out_shape=jax.ShapeDtypeStruct((n, d), F32),
        grid=(nsteps,),
        in_specs=[
            pl.BlockSpec(memory_space=pl.ANY),
            pl.BlockSpec((2 * tf, LANES), lambda i: (i, 0)),
            pl.BlockSpec((2 * tf, d), lambda i: (i, 0)),
            pl.BlockSpec((2 * tf, pd), lambda i: (i + p_block0, 0)),
            pl.BlockSpec((pd, d), const, pipeline_mode=pl.Buffered(1)),
            pl.BlockSpec((1, d), const),
            pl.BlockSpec((d, d), const, pipeline_mode=pl.Buffered(1)),
            pl.BlockSpec((1, d), const),
            pl.BlockSpec(memory_space=pl.ANY),
        ],
        out_specs=pl.BlockSpec((2 * tf, d), lambda i: (i, 0)),
        scratch_shapes=[
            pltpu.SMEM((2 * tf,), jnp.int32),
            pltpu.SMEM((2 * tf,), jnp.int32),
            pltpu.VMEM((2, tf * tr, LANES), jnp.uint32),
            pltpu.VMEM((2, tf * tr, LANES), jnp.uint32),
            pltpu.SemaphoreType.DMA((2,)),
            pltpu.SemaphoreType.DMA((2,)),
        ],
        compiler_params=_params(("arbitrary",)),
        name="final",
    )(dest_kmajor, route, x1, p2d, wple, gple, wgate, gfin, y)


def _tile(n, want):
    t = min(n, want)
    assert n % t == 0, (n, want)
    return t


class _Tiles(NamedTuple):
    inproj_rows: int
    inproj_cols: int
    hgrn_rows: int
    outproj_rows: int
    dispatch_rows: int
    expert_rows: int
    final_rows: int


def _tiles(b, t):
    n = b * t
    return _Tiles(inproj_rows=_tile(n, 1024), inproj_cols=1792, hgrn_rows=_tile(t, 4096),
                  outproj_rows=_tile(t, 512), dispatch_rows=_tile(n, 4096), expert_rows=512,
                  final_rows=_tile(n, 256))


def kernel(x, p, g_mix, w_in, lb_logits, hg_norm, conv_w, sc_norm, w_out, g_ffn, w_router_group,
           w_router_expert, w_gate, w_up, w_down, w_ple, g_ple, w_ple_gate, g_final):
    b, t, d = x.shape
    n = b * t
    layer = 0
    tiles = _tiles(b, t)
    assert w_in.shape[2] == 4 * HG_WIDTH + 3 * SC_WIDTH and w_out.shape[1] == HG_WIDTH + SC_WIDTH
    x2d = x.reshape(n, d)

    lower_bounds = jnp.cumsum(jax.nn.softmax(lb_logits.astype(F32), axis=0), axis=0)
    lb = lower_bounds[layer].reshape(1, HG_WIDTH)

    z = _inproj(x2d, g_mix[layer].reshape(1, d), w_in[layer].astype(BF16), tiles.inproj_rows, tiles.inproj_cols)
    z3 = z.reshape(b, t, z.shape[1])
    ohg = _hgrn(z3, lb, hg_norm[layer].reshape(1, HEAD_DIM), tiles.hgrn_rows).reshape(n, HG_WIDTH)

    wo = w_out[layer].astype(BF16)
    wr = jnp.concatenate([jnp.pad(w_router_group[layer], ((0, 0), (0, EXPERT_LANE0 - N_GROUPS))),
                          w_router_expert[layer]], axis=1).astype(F32)
    wr = jnp.pad(wr, ((0, 0), (0, LANES - wr.shape[1])))
    wr_hi = wr.astype(BF16)
    wr_lo = (wr - wr_hi.astype(F32)).astype(BF16)
    x1, hn, route, route_t, cnt = _outproj(
        x2d, ohg, z, conv_w[layer], sc_norm[layer].reshape(1, SC_WIDTH), wo, wo,
        g_ffn[layer].reshape(1, d), jnp.concatenate([wr_hi, wr_lo], axis=1), wr_hi, tiles.outproj_rows, t)

    tb = tiles.expert_rows
    counts = cnt[:, 0].astype(jnp.int32)
    padded = (counts + tb - 1) // tb * tb
    pend = jnp.cumsum(padded)
    pstart = pend - padded
    e_ids = route_t[R_E1:R_E2 + 1].astype(jnp.int32)
    ranks = route_t[R_RANK1:R_RANK2 + 1].astype(jnp.int32)
    expert_iota = jnp.arange(N_EXPERTS, dtype=jnp.int32)[None, :, None]
    seg_start = jnp.sum(jnp.where(e_ids[:, None, :] == expert_iota, pstart.astype(jnp.int32)[None, :, None], 0), axis=1)
    dest_kmajor = jnp.pad(seg_start + ranks, ((0, 0), (0, 2 * tiles.final_rows))).reshape(-1)
    nblocks = -(-(2 * n + N_EXPERTS * (tb - 1)) // tb)
    nused = (pend[-1] // tb).astype(jnp.int32)
    blk = jnp.minimum(jnp.arange(nblocks, dtype=jnp.int32), nused - 1) * tb
    block_e = jnp.minimum(jnp.sum((blk[:, None] >= pend[None, :]).astype(jnp.int32), axis=1), N_EXPERTS - 1)

    p_rows = nblocks * tb
    zero = jnp.zeros((1,), jnp.int32)
    xs = _dispatch(jnp.concatenate([counts, zero]), jnp.concatenate([padded, p_rows - pend[-1:]]),
                   jnp.concatenate([pstart, pend[-1:]]).astype(jnp.int32), dest_kmajor, hn, n, _token_rows(d),
                   p_rows, tiles.dispatch_rows)
    y = _experts(block_e, counts, nused.reshape(1), xs, w_gate[layer], w_up[layer], w_down[layer], tb)

    out = _final(dest_kmajor, route, x1, p.reshape(-1, p.shape[-1]), layer, w_ple[layer],
                 g_ple[layer].reshape(1, d), w_ple_gate[layer], g_final.reshape(1, d), y,
                 tiles.final_rows)
    return out.reshape(b, t, d)
```
